```python
import math
import jax, jax.numpy as jnp
from jax import lax
import numpy as np

D_MODEL = 2048
BATCH = 4
SEQ = 2048
DEPTH = 4
DEC_BATCH = 8
DEC_SEQ = 4
PAST_LEN = 16384
PAGE_SIZE = 128

N_EVEN = (DEPTH + 1) // 2
N_ODD = DEPTH // 2
A_WIDTH = D_MODEL // 2
SSM_GROUP = 16
SSM_GROUPS = A_WIDTH // SSM_GROUP
SSM_STATE = 64
B_WIDTH = D_MODEL - A_WIDTH
CONV_WIDTH = 31
HEAD_DIM = 64
V_DIM = 2 * HEAD_DIM
N_HEADS = D_MODEL // V_DIM
Q_WIDTH = 2 * N_HEADS * HEAD_DIM
V_WIDTH = N_HEADS * V_DIM
ROT_DIM = HEAD_DIM // 4
ROPE_THETA = 500000.0
FFN_HIDDEN = ((8 * D_MODEL + 3 * 256 - 1) // (3 * 256)) * 256
Q_BLOCK = 128
EPS = 1e-6
NEG_INF = -1e30

kernel_name = 'hybrid_s5_conformer_diffattn_decode_step'


def rms_norm(x, g):
    xf = x.astype(jnp.float32)
    y = xf * lax.rsqrt(jnp.mean(xf * xf, axis=-1, keepdims=True) + EPS)
    return (y * g.astype(jnp.float32)).astype(x.dtype)


def layer_norm(x, g, b):
    xf = x.astype(jnp.float32)
    mu = jnp.mean(xf, axis=-1, keepdims=True)
    var = jnp.mean(jnp.square(xf - mu), axis=-1, keepdims=True)
    y = (xf - mu) * lax.rsqrt(var + EPS)
    return (y * g.astype(jnp.float32) + b.astype(jnp.float32)).astype(x.dtype)


def swiglu_ffn(x, w_gate, w_up, w_down):
    return (jax.nn.silu(x @ w_gate) * (x @ w_up)) @ w_down


def partial_rope(x, positions):
    half = ROT_DIM // 2
    inv_freq = ROPE_THETA ** (-jnp.arange(half, dtype=jnp.float32) * 2.0 / ROT_DIM)
    ang = positions.astype(jnp.float32)[:, None] * inv_freq[None, :]
    cos = jnp.cos(ang)[None, :, None, :]
    sin = jnp.sin(ang)[None, :, None, :]
    xf = x.astype(jnp.float32)
    x1 = xf[..., :half]
    x2 = xf[..., half:ROT_DIM]
    rot = jnp.concatenate([x1 * cos - x2 * sin, x2 * cos + x1 * sin], axis=-1).astype(x.dtype)
    return jnp.concatenate([rot, x[..., ROT_DIM:]], axis=-1)


def _complex_affine_combine(e1, e2):
    a1r, a1i, b1r, b1i = e1
    a2r, a2i, b2r, b2i = e2
    ar = a1r * a2r - a1i * a2i
    ai = a1r * a2i + a1i * a2r
    br = a2r * b1r - a2i * b1i + b2r
    bi = a2r * b1i + a2i * b1r + b2i
    return ar, ai, br, bi


def s5_mixer(u, h0_re, h0_im, lam_re, lam_im, log_dt, b_re, b_im, c_re, c_im, d_skip, w_glu):
    f32 = jnp.float32
    bsz, t, _ = u.shape
    ug = u.astype(f32).reshape(bsz, t, SSM_GROUPS, SSM_GROUP)
    dt = jnp.exp(log_dt.astype(f32))[:, None]
    lr = lam_re.astype(f32)
    li = lam_im.astype(f32)
    mag = jnp.exp(lr * dt)
    ab_re = mag * jnp.cos(li * dt)
    ab_im = mag * jnp.sin(li * dt)
    den = lr * lr + li * li
    num_re = ab_re - 1.0
    coef_re = (num_re * lr + ab_im * li) / den
    coef_im = (ab_im * lr - num_re * li) / den
    br = b_re.astype(f32)
    bi = b_im.astype(f32)
    bb_re = coef_re[..., None] * br - coef_im[..., None] * bi
    bb_im = coef_re[..., None] * bi + coef_im[..., None] * br
    bu_re = jnp.einsum('btgc,gpc->btgp', ug, bb_re)
    bu_im = jnp.einsum('btgc,gpc->btgp', ug, bb_im)
    s0_re = h0_re.astype(f32)
    s0_im = h0_im.astype(f32)
    bu_re = bu_re.at[:, 0].add(ab_re * s0_re - ab_im * s0_im)
    bu_im = bu_im.at[:, 0].add(ab_re * s0_im + ab_im * s0_re)
    a_re = jnp.broadcast_to(ab_re, bu_re.shape)
    a_im = jnp.broadcast_to(ab_im, bu_im.shape)
    _, _, xs_re, xs_im = lax.associative_scan(
        _complex_affine_combine, (a_re, a_im, bu_re, bu_im), axis=1)
    y = (jnp.einsum('gcp,btgp->btgc', c_re.astype(f32), xs_re)
         - jnp.einsum('gcp,btgp->btgc', c_im.astype(f32), xs_im)
         + d_skip.astype(f32).reshape(SSM_GROUPS, SSM_GROUP) * ug)
    y = jax.nn.gelu(y.reshape(bsz, t, A_WIDTH))
    y = y * jax.nn.sigmoid(y @ w_glu.astype(f32))
    return y.astype(u.dtype), xs_re[:, -1], xs_im[:, -1]


def conv_module(v, g, buf, conv_w, conv_b, ln_g, ln_b):
    u = v * jax.nn.sigmoid(g)
    full = jnp.concatenate([buf.astype(u.dtype), u], axis=1)
    y = lax.conv_general_dilated(
        full, conv_w[:, None, :].astype(u.dtype), window_strides=(1,), padding='VALID',
        dimension_numbers=('NWC', 'WIO', 'NWC'), feature_group_count=B_WIDTH)
    y = jax.nn.silu(layer_norm(y + conv_b, ln_g, ln_b))
    return y, full[:, -(CONV_WIDTH - 1):]


def even_mixer(h, s_re, s_im, conv_buf, w_in, lam_re, lam_im, log_dt, b_re, b_im,
               c_re, c_im, d_skip, w_glu, conv_w, conv_b, ln_g, ln_b, w_out):
    proj = h @ w_in
    u_a, v_b, g_b = jnp.split(proj, [A_WIDTH, A_WIDTH + B_WIDTH], axis=-1)
    y_a, n_re, n_im = s5_mixer(u_a, s_re, s_im, lam_re, lam_im, log_dt, b_re, b_im,
                               c_re, c_im, d_skip, w_glu)
    y_b, n_buf = conv_module(v_b, g_b, conv_buf, conv_w, conv_b, ln_g, ln_b)
    out = jnp.concatenate([y_a, y_b], axis=-1) @ w_out
    return out, n_re, n_im, n_buf


def diff_weights(scores, lam):
    bsz, _, nq, nk = scores.shape
    p = jax.nn.softmax(scores, axis=-1).reshape(bsz, N_HEADS, 2, nq, nk)
    return p[:, :, 0] - lam * p[:, :, 1]


def prompt_diff_attention(q, k, v, lam):
    bsz, s = q.shape[:2]
    n_blocks = s // Q_BLOCK
    q_blocks = q.reshape(bsz, n_blocks, Q_BLOCK, 2 * N_HEADS, HEAD_DIM).transpose(1, 0, 2, 3, 4)
    k_pos = jnp.arange(s)
    vf = v.astype(jnp.float32)
    scale = HEAD_DIM ** -0.5

    def one_block(args):
        q_blk, blk = args
        sc = jnp.einsum('bqhd,bkhd->bhqk', q_blk, k).astype(jnp.float32) * scale
        q_pos = blk * Q_BLOCK + jnp.arange(Q_BLOCK)
        sc = jnp.where(k_pos[None, :] <= q_pos[:, None], sc, NEG_INF)
        return jnp.einsum('bhqk,bkhe->bqhe', diff_weights(sc, lam), vf)

    o = lax.map(one_block, (q_blocks, jnp.arange(n_blocks)))
    return o.transpose(1, 0, 2, 3, 4).reshape(bsz, s, N_HEADS, V_DIM)


def sample_diff_attention(q, k, v, k_past, v_past, lam):
    nq = q.shape[1]
    past = k_past.shape[1]
    scale = HEAD_DIM ** -0.5
    s_past = jnp.einsum('bqhd,bkhd->bhqk', q, k_past).astype(jnp.float32) * scale
    s_new = jnp.einsum('bqhd,bkhd->bhqk', q, k).astype(jnp.float32) * scale
    causal = jnp.tril(jnp.ones((nq, nq), dtype=bool))
    s_new = jnp.where(causal, s_new, NEG_INF)
    w = diff_weights(jnp.concatenate([s_past, s_new], axis=-1), lam)
    return (jnp.einsum('bhqk,bkhe->bqhe', w[..., :past], v_past.astype(jnp.float32))
            + jnp.einsum('bhqk,bkhe->bqhe', w[..., past:], v.astype(jnp.float32)))


def diff_attention_mixer(h, positions, kv_past, w_qkv, lq1, lk1, lq2, lk2, subln_g, w_o, lam_init):
    f32 = jnp.float32
    bsz, t, _ = h.shape
    q, k, v = jnp.split(h @ w_qkv, [Q_WIDTH, 2 * Q_WIDTH], axis=-1)
    q = partial_rope(q.reshape(bsz, t, 2 * N_HEADS, HEAD_DIM), positions)
    k = partial_rope(k.reshape(bsz, t, 2 * N_HEADS, HEAD_DIM), positions)
    v = v.reshape(bsz, t, N_HEADS, V_DIM)
    lam = (jnp.exp(jnp.sum(lq1.astype(f32) * lk1.astype(f32)))
           - jnp.exp(jnp.sum(lq2.astype(f32) * lk2.astype(f32))) + lam_init)
    if kv_past is None:
        o = prompt_diff_attention(q, k, v, lam)
    else:
        o = sample_diff_attention(q, k, v, kv_past[0], kv_past[1], lam)
    o = rms_norm(o, subln_g) * (1.0 - lam_init)
    out = o.reshape(bsz, t, V_WIDTH).astype(h.dtype) @ w_o
    return out, k, v


def setup_inputs(seed: int = 0) -> dict:
    key = jax.random.key(seed)
    ks = iter(jax.random.split(key, 48))
    f32 = jnp.float32

    def nrm(shape, scale):
        return jax.random.normal(next(ks), shape, f32) * scale

    def gain(shape):
        return 1.0 + nrm(shape, 0.01)

    n_pages = PAST_LEN // PAGE_SIZE
    n_used = DEC_BATCH * n_pages
    n_pool = n_used + n_used // 4
    page_table = jax.random.permutation(next(ks), n_pool)[:n_used].reshape(
        DEC_BATCH, n_pages).astype(jnp.int32)

    x_prompt = nrm((BATCH, SEQ, D_MODEL), 1.0)
    x_sample = nrm((DEC_BATCH, DEC_SEQ, D_MODEL), 1.0)
    cache_k = nrm((N_ODD, n_pool, PAGE_SIZE, 2 * N_HEADS, HEAD_DIM), 1.0)
    cache_v = nrm((N_ODD, n_pool, PAGE_SIZE, N_HEADS, V_DIM), 1.0)
    state_ssm_re = nrm((N_EVEN, DEC_BATCH, SSM_GROUPS, SSM_STATE), 0.1)
    state_ssm_im = nrm((N_EVEN, DEC_BATCH, SSM_GROUPS, SSM_STATE), 0.1)
    state_conv = nrm((N_EVEN, DEC_BATCH, CONV_WIDTH - 1, B_WIDTH), 0.5)

    norm_mix_pre = gain((DEPTH, D_MODEL))
    norm_mix_post = gain((DEPTH, D_MODEL))
    norm_ffn_pre = gain((DEPTH, D_MODEL))
    norm_ffn_post = gain((DEPTH, D_MODEL))

    w_in_even = nrm((N_EVEN, D_MODEL, A_WIDTH + 2 * B_WIDTH), D_MODEL ** -0.5)
    n_idx = jnp.arange(SSM_STATE, dtype=f32)
    ssm_lam_re = -0.5 + nrm((N_EVEN, SSM_GROUPS, SSM_STATE), 0.01)
    ssm_lam_im = math.pi * n_idx + nrm((N_EVEN, SSM_GROUPS, SSM_STATE), 0.01)
    ssm_log_dt = jax.random.uniform(next(ks), (N_EVEN, SSM_GROUPS), f32,
                                    math.log(1e-3), math.log(1e-1))
    ssm_b_re = nrm((N_EVEN, SSM_GROUPS, SSM_STATE, SSM_GROUP), (2 * SSM_GROUP) ** -0.5)
    ssm_b_im = nrm((N_EVEN, SSM_GROUPS, SSM_STATE, SSM_GROUP), (2 * SSM_GROUP) ** -0.5)
    ssm_c_re = nrm((N_EVEN, SSM_GROUPS, SSM_GROUP, SSM_STATE), SSM_STATE ** -0.5)
    ssm_c_im = nrm((N_EVEN, SSM_GROUPS, SSM_GROUP, SSM_STATE), SSM_STATE ** -0.5)
    ssm_d = nrm((N_EVEN, A_WIDTH), 1.0)
    w_glu = nrm((N_EVEN, A_WIDTH, A_WIDTH), A_WIDTH ** -0.5)
    conv_w = nrm((N_EVEN, CONV_WIDTH, B_WIDTH), CONV_WIDTH ** -0.5)
    conv_b = nrm((N_EVEN, B_WIDTH), 0.01)
    conv_ln_g = gain((N_EVEN, B_WIDTH))
    conv_ln_b = nrm((N_EVEN, B_WIDTH), 0.01)
    w_out_even = nrm((N_EVEN, A_WIDTH + B_WIDTH, D_MODEL), (A_WIDTH + B_WIDTH) ** -0.5)

    w_qkv = nrm((N_ODD, D_MODEL, 2 * Q_WIDTH + V_WIDTH), D_MODEL ** -0.5)
    lambda_q1 = nrm((N_ODD, HEAD_DIM), 0.1)
    lambda_k1 = nrm((N_ODD, HEAD_DIM), 0.1)
    lambda_q2 = nrm((N_ODD, HEAD_DIM), 0.1)
    lambda_k2 = nrm((N_ODD, HEAD_DIM), 0.1)
    subln_g = gain((N_ODD, V_DIM))
    w_o = nrm((N_ODD, V_WIDTH, D_MODEL), V_WIDTH ** -0.5)

    w_gate = nrm((DEPTH, D_MODEL, FFN_HIDDEN), D_MODEL ** -0.5)
    w_up = nrm((DEPTH, D_MODEL, FFN_HIDDEN), D_MODEL ** -0.5)
    w_down = nrm((DEPTH, FFN_HIDDEN, D_MODEL), FFN_HIDDEN ** -0.5)

    return {
        'x_prompt': x_prompt, 'x_sample': x_sample,
        'cache_k': cache_k, 'cache_v': cache_v,
        'state_ssm_re': state_ssm_re, 'state_ssm_im': state_ssm_im, 'state_conv': state_conv,
        'page_table': page_table,
        'norm_mix_pre': norm_mix_pre, 'norm_mix_post': norm_mix_post,
        'norm_ffn_pre': norm_ffn_pre, 'norm_ffn_post': norm_ffn_post,
        'w_in_even': w_in_even, 'ssm_lam_re': ssm_lam_re, 'ssm_lam_im': ssm_lam_im,
        'ssm_log_dt': ssm_log_dt, 'ssm_b_re': ssm_b_re, 'ssm_b_im': ssm_b_im,
        'ssm_c_re': ssm_c_re, 'ssm_c_im': ssm_c_im, 'ssm_d': ssm_d, 'w_glu': w_glu,
        'conv_w': conv_w, 'conv_b': conv_b, 'conv_ln_g': conv_ln_g, 'conv_ln_b': conv_ln_b,
        'w_out_even': w_out_even,
        'w_qkv': w_qkv, 'lambda_q1': lambda_q1, 'lambda_k1': lambda_k1,
        'lambda_q2': lambda_q2, 'lambda_k2': lambda_k2, 'subln_g': subln_g, 'w_o': w_o,
        'w_gate': w_gate, 'w_up': w_up, 'w_down': w_down,
    }


def reference(x_prompt, x_sample, cache_k, cache_v, state_ssm_re, state_ssm_im, state_conv,
              page_table, norm_mix_pre, norm_mix_post, norm_ffn_pre, norm_ffn_post,
              w_in_even, ssm_lam_re, ssm_lam_im, ssm_log_dt, ssm_b_re, ssm_b_im,
              ssm_c_re, ssm_c_im, ssm_d, w_glu, conv_w, conv_b, conv_ln_g, conv_ln_b,
              w_out_even, w_qkv, lambda_q1, lambda_k1, lambda_q2, lambda_k2, subln_g, w_o,
              w_gate, w_up, w_down):

    def run_trunk(h, positions, ssm_re0, ssm_im0, conv0, paged):
        new_re, new_im, new_conv, new_k, new_v = [], [], [], [], []
        for i in range(DEPTH):
            j = i // 2
            hn = rms_norm(h, norm_mix_pre[i])
            if i % 2 == 0:
                mix, s_re, s_im, buf = even_mixer(
                    hn, ssm_re0[j], ssm_im0[j], conv0[j], w_in_even[j], ssm_lam_re[j],
                    ssm_lam_im[j], ssm_log_dt[j], ssm_b_re[j], ssm_b_im[j], ssm_c_re[j],
                    ssm_c_im[j], ssm_d[j], w_glu[j], conv_w[j], conv_b[j], conv_ln_g[j],
                    conv_ln_b[j], w_out_even[j])
                new_re.append(s_re.astype(state_ssm_re.dtype))
                new_im.append(s_im.astype(state_ssm_im.dtype))
                new_conv.append(buf.astype(state_conv.dtype))
            else:
                if paged:
                    n_seq = page_table.shape[0]
                    k_past = cache_k[j, page_table].reshape(n_seq, -1, 2 * N_HEADS, HEAD_DIM)
                    v_past = cache_v[j, page_table].reshape(n_seq, -1, N_HEADS, V_DIM)
                    kv_past = (k_past, v_past)
                else:
                    kv_past = None
                lam_init = 0.8 - 0.6 * math.exp(-0.3 * i)
                mix, k_rows, v_rows = diff_attention_mixer(
                    hn, positions, kv_past, w_qkv[j], lambda_q1[j], lambda_k1[j],
                    lambda_q2[j], lambda_k2[j], subln_g[j], w_o[j], lam_init)
                new_k.append(k_rows.astype(cache_k.dtype))
                new_v.append(v_rows.astype(cache_v.dtype))
            h = h + rms_norm(mix, norm_mix_post[i])
            f = swiglu_ffn(rms_norm(h, norm_ffn_pre[i]), w_gate[i], w_up[i], w_down[i])
            h = h + rms_norm(f, norm_ffn_post[i])
        return (h, jnp.stack(new_re), jnp.stack(new_im), jnp.stack(new_conv),
                jnp.stack(new_k), jnp.stack(new_v))

    n_prompt = x_prompt.shape[0]
    zero_ssm = jnp.zeros((N_EVEN, n_prompt, SSM_GROUPS, SSM_STATE), jnp.float32)
    zero_conv = jnp.zeros((N_EVEN, n_prompt, CONV_WIDTH - 1, B_WIDTH), x_prompt.dtype)
    pos_prompt = jnp.arange(x_prompt.shape[1])
    pos_sample = PAST_LEN + jnp.arange(x_sample.shape[1])

    y_prompt, re_p, im_p, conv_p, k_p, v_p = run_trunk(
        x_prompt, pos_prompt, zero_ssm, zero_ssm, zero_conv, False)
    y_sample, re_s, im_s, conv_s, k_s, v_s = run_trunk(
        x_sample, pos_sample, state_ssm_re, state_ssm_im, state_conv, True)

    return (y_prompt, y_sample, re_p, im_p, conv_p, k_p, v_p, re_s, im_s, conv_s, k_s, v_s)
```

```python
import functools
import math

import jax
import jax.numpy as jnp
from jax import lax
from jax.experimental import pallas as pl
from jax.experimental.pallas import tpu as pltpu

F32 = jnp.float32
BF16 = jnp.bfloat16
EPS = 1e-6
NEG_INF = -1e30
ROPE_THETA = 500000.0
S5_CHUNK = 16
CONV_HALO = 32
VMEM_LIMIT = 56 * 1024 * 1024
HI = lax.Precision.HIGHEST


def _tile(n, pref):
    if n <= pref:
        return n
    t = pref
    while t >= 8:
        if n % t == 0:
            return t
        t //= 2
    return n


def _params(*sem):
    return pltpu.CompilerParams(dimension_semantics=sem, vmem_limit_bytes=VMEM_LIMIT)


def _rms_rows(x, g, eps=EPS):
    ms = jnp.mean(x * x, axis=-1, keepdims=True)
    return x * lax.rsqrt(ms + eps) * g


def _norm_matmul_kernel(x_ref, g_ref, w_ref, o_ref, hn_ref):
    @pl.when(pl.program_id(1) == 0)
    def _():
        hn_ref[...] = _rms_rows(x_ref[...], g_ref[...]).astype(BF16)

    o_ref[...] = jnp.dot(hn_ref[...], w_ref[...], preferred_element_type=F32)


def norm_matmul(x, g, w):
    t, d = x.shape
    n = w.shape[1]
    tm, tn = _tile(t, 1024), _tile(n, 512)
    return pl.pallas_call(
        _norm_matmul_kernel,
        grid=(t // tm, n // tn),
        in_specs=[pl.BlockSpec((tm, d), lambda i, j: (i, 0)),
                  pl.BlockSpec((1, d), lambda i, j: (0, 0)),
                  pl.BlockSpec((d, tn), lambda i, j: (0, j))],
        out_specs=pl.BlockSpec((tm, tn), lambda i, j: (i, j)),
        out_shape=jax.ShapeDtypeStruct((t, n), F32),
        scratch_shapes=[pltpu.VMEM((tm, d), BF16)],
        compiler_params=_params("parallel", "arbitrary"),
        name="norm_matmul",
    )(x, g.reshape(1, d), w)


def _qkv_kernel(x_ref, g_ref, pos_ref, invf_ref, sel_ref, w_ref,
                q_ref, k_ref, v_ref, kb_ref, vb_ref, hn_ref, tab_ref, *, nq_tiles, scale):
    j = pl.program_id(1)

    @pl.when(j == 0)
    def _():
        hn_ref[...] = _rms_rows(x_ref[...], g_ref[...]).astype(BF16)
        ang = pos_ref[...] * invf_ref[...]
        c, s = jnp.cos(ang), jnp.sin(ang)
        tab_ref[0] = jnp.where(sel_ref[0:1, :] > 0.5, c, 1.0)
        tab_ref[1] = s * sel_ref[1:2, :]
        tab_ref[2] = s * sel_ref[2:3, :]

    y = jnp.dot(hn_ref[...], w_ref[...], preferred_element_type=F32)
    tn = y.shape[1]
    half = 128 // 16

    def rope(y):
        parts = []
        for c0 in range(0, tn, 128):
            x = y[:, c0:c0 + 128]
            parts.append(x * tab_ref[0] + pltpu.roll(x, half, axis=1) * tab_ref[1]
                         + pltpu.roll(x, 128 - half, axis=1) * tab_ref[2])
        return jnp.concatenate(parts, axis=1) if len(parts) > 1 else parts[0]

    @pl.when(j < nq_tiles)
    def _():
        q_ref[...] = (rope(y) * scale).astype(BF16)

    @pl.when(jnp.logical_and(j >= nq_tiles, j < 2 * nq_tiles))
    def _():
        r = rope(y)
        k_ref[...] = r
        kb_ref[...] = r.astype(BF16)

    @pl.when(j >= 2 * nq_tiles)
    def _():
        v_ref[...] = y
        vb_ref[...] = y.astype(BF16)


def qkv_rope(x, g, w, pos, q_width, head_dim):
    t, d = x.shape
    n = w.shape[1]
    v_width = n - 2 * q_width
    rot = head_dim // 4
    assert rot == 16 and 128 % head_dim == 0
    tm = _tile(t, 1024)
    tn = _tile(math.gcd(q_width, v_width), 512)
    nq, nv = q_width // tn, v_width // tn
    lane = jnp.arange(128) % head_dim
    inv_freq = ROPE_THETA ** (-jnp.arange(rot // 2, dtype=F32) * 2.0 / rot)
    invf = jnp.where(lane < rot, inv_freq[lane % (rot // 2)], 0.0).reshape(1, 128).astype(F32)
    sel = jnp.stack([(lane < rot).astype(F32),
                     jnp.logical_and(lane >= rot // 2, lane < rot).astype(F32),
                     -(lane < rot // 2).astype(F32)])

    def clip(j, lo, cnt):
        return jnp.clip(j - lo, 0, cnt - 1)

    outs = pl.pallas_call(
        functools.partial(_qkv_kernel, nq_tiles=nq, scale=head_dim ** -0.5),
        grid=(t // tm, 2 * nq + nv),
        in_specs=[pl.BlockSpec((tm, d), lambda i, j: (i, 0)),
                  pl.BlockSpec((1, d), lambda i, j: (0, 0)),
                  pl.BlockSpec((tm, 1), lambda i, j: (i, 0)),
                  pl.BlockSpec((1, 128), lambda i, j: (0, 0)),
                  pl.BlockSpec((3, 128), lambda i, j: (0, 0)),
                  pl.BlockSpec((d, tn), lambda i, j: (0, j))],
        out_specs=[pl.BlockSpec((tm, tn), lambda i, j: (i, clip(j, 0, nq))),
                   pl.BlockSpec((tm, tn), lambda i, j: (i, clip(j, nq, nq))),
                   pl.BlockSpec((tm, tn), lambda i, j: (i, clip(j, 2 * nq, nv))),
                   pl.BlockSpec((tm, tn), lambda i, j: (i, clip(j, nq, nq))),
                   pl.BlockSpec((tm, tn), lambda i, j: (i, clip(j, 2 * nq, nv)))],
        out_shape=[jax.ShapeDtypeStruct((t, q_width), BF16),
                   jax.ShapeDtypeStruct((t, q_width), F32),
                   jax.ShapeDtypeStruct((t, v_width), F32),
                   jax.ShapeDtypeStruct((t, q_width), BF16),
                   jax.ShapeDtypeStruct((t, v_width), BF16)],
        scratch_shapes=[pltpu.VMEM((tm, d), BF16), pltpu.VMEM((3, tm, 128), F32)],
        compiler_params=_params("parallel", "arbitrary"),
        name="qkv_rope",
    )(x, g.reshape(1, d), pos, invf, sel, w)
    return outs


def _matmul_norm_res_kernel(a_ref, w_ref, g_ref, h_ref, o_ref, acc_ref, *, nj, tn):
    j = pl.program_id(1)
    acc_ref[j] = jnp.dot(a_ref[...], w_ref[...], preferred_element_type=F32)

    @pl.when(j == nj - 1)
    def _():
        ss = jnp.zeros((acc_ref.shape[1], 1), F32)
        for jj in range(nj):
            y = acc_ref[jj]
            ss = ss + jnp.sum(y * y, axis=-1, keepdims=True)
        inv = lax.rsqrt(ss / (nj * tn) + EPS)
        for jj in range(nj):
            sl = slice(jj * tn, (jj + 1) * tn)
            o_ref[:, sl] = h_ref[:, sl] + acc_ref[jj] * inv * g_ref[:, sl]


def matmul_norm_residual(a, w, g, h):
    t, k = a.shape
    d = w.shape[1]
    tm, tn = _tile(t, 512), _tile(d, 512)
    nj = d // tn
    return pl.pallas_call(
        functools.partial(_matmul_norm_res_kernel, nj=nj, tn=tn),
        grid=(t // tm, nj),
        in_specs=[pl.BlockSpec((tm, k), lambda i, j: (i, 0)),
                  pl.BlockSpec((k, tn), lambda i, j: (0, j)),
                  pl.BlockSpec((1, d), lambda i, j: (0, 0)),
                  pl.BlockSpec((tm, d), lambda i, j: (i, 0))],
        out_specs=pl.BlockSpec((tm, d), lambda i, j: (i, 0)),
        out_shape=jax.ShapeDtypeStruct((t, d), F32),
        scratch_shapes=[pltpu.VMEM((nj, tm, tn), F32)],
        compiler_params=_params("parallel", "arbitrary"),
        name="matmul_norm_residual",
    )(a, w, g.reshape(1, d), h)


def _ffn_kernel(h_ref, gpre_ref, gpost_ref, wg_ref, wu_ref, wd_ref, o_ref, hn_ref, *, nf):
    f = pl.program_id(1)

    @pl.when(f == 0)
    def _():
        hn_ref[...] = _rms_rows(h_ref[...], gpre_ref[...]).astype(BF16)

    hn = hn_ref[...]
    gate = jnp.dot(hn, wg_ref[...], preferred_element_type=F32)
    up = jnp.dot(hn, wu_ref[...], preferred_element_type=F32)
    act = (gate * jax.nn.sigmoid(gate) * up).astype(BF16)
    part = jnp.dot(act, wd_ref[...], preferred_element_type=F32)

    @pl.when(f == 0)
    def _():
        o_ref[...] = part

    @pl.when(f > 0)
    def _():
        o_ref[...] += part

    @pl.when(f == nf - 1)
    def _():
        o_ref[...] = h_ref[...] + _rms_rows(o_ref[...], gpost_ref[...])


def ffn(h, g_pre, g_post, wg, wu, wd):
    t, d = h.shape
    fh = wg.shape[1]
    tm = _tile(t, 1024)
    tf = 256 if fh % 256 == 0 else fh
    nf = fh // tf
    return pl.pallas_call(
        functools.partial(_ffn_kernel, nf=nf),
        grid=(t // tm, nf),
        in_specs=[pl.BlockSpec((tm, d), lambda i, f: (i, 0), pipeline_mode=pl.Buffered(1)),
                  pl.BlockSpec((1, d), lambda i, f: (0, 0)),
                  pl.BlockSpec((1, d), lambda i, f: (0, 0)),
                  pl.BlockSpec((d, tf), lambda i, f: (0, f)),
                  pl.BlockSpec((d, tf), lambda i, f: (0, f)),
                  pl.BlockSpec((tf, d), lambda i, f: (f, 0))],
        out_specs=pl.BlockSpec((tm, d), lambda i, f: (i, 0)),
        out_shape=jax.ShapeDtypeStruct((t, d), F32),
        scratch_shapes=[pltpu.VMEM((tm, d), BF16)],
        compiler_params=_params("parallel", "arbitrary"),
        name="ffn",
    )(h, g_pre.reshape(1, d), g_post.reshape(1, d), wg, wu, wd)


def _glu_kernel(y_ref, w_ref, o_ref):
    y = y_ref[...]
    z = jnp.dot(y.astype(BF16), w_ref[...], preferred_element_type=F32)
    o_ref[...] = (y * jax.nn.sigmoid(z)).astype(o_ref.dtype)


def s5_glu(y, w):
    t, a = y.shape
    tm = _tile(t, 1024)
    return pl.pallas_call(
        _glu_kernel,
        grid=(t // tm,),
        in_specs=[pl.BlockSpec((tm, a), lambda i: (i, 0)),
                  pl.BlockSpec((a, a), lambda i: (0, 0))],
        out_specs=pl.BlockSpec((tm, a), lambda i: (i, 0)),
        out_shape=jax.ShapeDtypeStruct((t, a), BF16),
        compiler_params=_params("parallel"),
        name="s5_glu",
    )(y, w)


def _s5_prep_kernel(lam_row_ref, lam_col_ref, bt_ref, ct_ref, e_ref, tt_ref,
                    g0_ref, p_ref, q_ref, al_ref, *, chunk):
    lr, li, ldt = lam_row_ref[0:1, :], lam_row_ref[1:2, :], lam_row_ref[2:3, :]
    dt = jnp.exp(ldt)
    mag = jnp.exp(lr * dt)
    ab_re, ab_im = mag * jnp.cos(li * dt), mag * jnp.sin(li * dt)
    den = lr * lr + li * li
    num_re = ab_re - 1.0
    coef_re = (num_re * lr + ab_im * li) / den
    coef_im = (ab_im * lr - num_re * li) / den
    bt_re, bt_im = bt_ref[0], bt_ref[1]
    bb_re = coef_re * bt_re - coef_im * bt_im
    bb_im = coef_re * bt_im + coef_im * bt_re

    magl = jnp.exp(lr * dt * chunk)
    al_ref[0:1, :] = magl * jnp.cos(li * dt * chunk)
    al_ref[1:2, :] = magl * jnp.sin(li * dt * chunk)

    e = e_ref[...]
    pm = jnp.exp(lr * dt * e)
    pw_re, pw_im = pm * jnp.cos(li * dt * e), pm * jnp.sin(li * dt * e)
    bbt_re = jnp.concatenate([bb_re] * chunk, axis=0)
    bbt_im = jnp.concatenate([bb_im] * chunk, axis=0)
    p_ref[0] = pw_re * bbt_re - pw_im * bbt_im
    p_ref[1] = pw_re * bbt_im + pw_im * bbt_re

    lrc, lic, ldtc = lam_col_ref[0], lam_col_ref[1], lam_col_ref[2]
    dtc = jnp.exp(ldtc)
    tt = tt_ref[...]
    fm = jnp.exp(lrc * dtc * tt)
    fw_re, fw_im = fm * jnp.cos(lic * dtc * tt), fm * jnp.sin(lic * dtc * tt)
    ct_re, ct_im = ct_ref[0], ct_ref[1]
    f_re = ct_re * fw_re - ct_im * fw_im
    f_im = ct_re * fw_im + ct_im * fw_re
    g0_ref[...] = (jnp.dot(bb_re, f_re, precision=HI, preferred_element_type=F32)
                   - jnp.dot(bb_im, f_im, precision=HI, preferred_element_type=F32))
    magc = jnp.exp(lrc * dtc)
    abc_re, abc_im = magc * jnp.cos(lic * dtc), magc * jnp.sin(lic * dtc)
    q_ref[0] = f_re * abc_re - f_im * abc_im
    q_ref[1] = -(f_re * abc_im + f_im * abc_re)


def _s5_main_kernel(u_ref, w_ref, p_ref, q_ref, al_ref, h0_ref, d_ref,
                    y_ref, hn_ref, bc_re, bc_im, xs_re, xs_im, *, nct, bz):
    u = u_ref[...]
    bc_re[...] = jnp.dot(u, p_ref[0], precision=HI, preferred_element_type=F32)
    bc_im[...] = jnp.dot(u, p_ref[1], precision=HI, preferred_element_type=F32)
    a_re, a_im = al_ref[0:1, :], al_ref[1:2, :]

    def step(j, carry):
        x_re, x_im = carry
        rows = pl.ds(j * bz, bz)
        xs_re[rows, :] = x_re
        xs_im[rows, :] = x_im
        n_re = a_re * x_re - a_im * x_im + bc_re[rows, :]
        n_im = a_re * x_im + a_im * x_re + bc_im[rows, :]
        return n_re, n_im

    x_re, x_im = lax.fori_loop(0, nct, step, (h0_ref[0], h0_ref[1]))
    hn_ref[0] = x_re
    hn_ref[1] = x_im
    y = (jnp.dot(u, w_ref[...], precision=HI, preferred_element_type=F32)
         + jnp.dot(xs_re[...], q_ref[0], precision=HI, preferred_element_type=F32)
         + jnp.dot(xs_im[...], q_ref[1], precision=HI, preferred_element_type=F32)
         + d_ref[...] * u)
    y_ref[...] = jax.nn.gelu(y)


def s5_mixer(u, h0_re, h0_im, lam_re, lam_im, log_dt, b_re, b_im, c_re, c_im, d_skip):
    bz, t, a = u.shape
    g, p = lam_re.shape
    gc = a // g
    chunk = S5_CHUNK if t % S5_CHUNK == 0 else t
    nct = t // chunk
    nc, lc = nct * bz, chunk * gc

    lam_row = jnp.stack([lam_re, lam_im, jnp.broadcast_to(log_dt[:, None], (g, p))], axis=1)
    lam_col = jnp.broadcast_to(lam_row[:, :, :, None], (g, 3, p, lc))
    bt = jnp.stack([b_re, b_im], axis=1).transpose(0, 1, 3, 2)
    ct = jnp.tile(jnp.stack([c_re, c_im], axis=1).transpose(0, 1, 3, 2), (1, 1, 1, chunk))

    step = jnp.arange(lc, dtype=jnp.int32) // gc
    e_col = (chunk - 1 - step).astype(F32).reshape(lc, 1)
    tt_row = step.astype(F32).reshape(1, lc)

    g0, pmat, qmat, al = pl.pallas_call(
        functools.partial(_s5_prep_kernel, chunk=chunk),
        grid=(g,),
        in_specs=[pl.BlockSpec((None, 3, p), lambda i: (i, 0, 0)),
                  pl.BlockSpec((None, 3, p, lc), lambda i: (i, 0, 0, 0)),
                  pl.BlockSpec((None, 2, gc, p), lambda i: (i, 0, 0, 0)),
                  pl.BlockSpec((None, 2, p, lc), lambda i: (i, 0, 0, 0)),
                  pl.BlockSpec((lc, 1), lambda i: (0, 0)),
                  pl.BlockSpec((1, lc), lambda i: (0, 0))],
        out_specs=[pl.BlockSpec((None, gc, lc), lambda i: (i, 0, 0)),
                   pl.BlockSpec((None, 2, lc, p), lambda i: (i, 0, 0, 0)),
                   pl.BlockSpec((None, 2, p, lc), lambda i: (i, 0, 0, 0)),
                   pl.BlockSpec((None, 2, p), lambda i: (i, 0, 0))],
        out_shape=[jax.ShapeDtypeStruct((g, gc, lc), F32),
                   jax.ShapeDtypeStruct((g, 2, lc, p), F32),
                   jax.ShapeDtypeStruct((g, 2, p, lc), F32),
                   jax.ShapeDtypeStruct((g, 2, p), F32)],
        compiler_params=_params("parallel"),
        name="s5_prep",
    )(lam_row, lam_col, bt, ct, e_col, tt_row)

    g0p = jnp.pad(g0.reshape(g, gc, chunk, gc), ((0, 0), (0, 0), (chunk, 0), (0, 0)))
    wmat = jnp.stack([g0p[:, :, chunk - s:2 * chunk - s, :] for s in range(chunk)], axis=1).reshape(g, lc, lc)

    ug = u.reshape(bz, nct, chunk, g, gc).transpose(3, 1, 0, 2, 4).reshape(g, nc, lc)
    h0 = jnp.stack([h0_re, h0_im], axis=0).transpose(2, 0, 1, 3)
    dt = jnp.tile(d_skip.reshape(g, 1, gc), (1, 1, chunk))

    y, hn = pl.pallas_call(
        functools.partial(_s5_main_kernel, nct=nct, bz=bz),
        grid=(g,),
        in_specs=[pl.BlockSpec((None, nc, lc), lambda i: (i, 0, 0)),
                  pl.BlockSpec((None, lc, lc), lambda i: (i, 0, 0)),
                  pl.BlockSpec((None, 2, lc, p), lambda i: (i, 0, 0, 0)),
                  pl.BlockSpec((None, 2, p, lc), lambda i: (i, 0, 0, 0)),
                  pl.BlockSpec((None, 2, p), lambda i: (i, 0, 0)),
                  pl.BlockSpec((None, 2, bz, p), lambda i: (i, 0, 0, 0)),
                  pl.BlockSpec((None, 1, lc), lambda i: (i, 0, 0))],
        out_specs=[pl.BlockSpec((None, nc, lc), lambda i: (i, 0, 0)),
                   pl.BlockSpec((None, 2, bz, p), lambda i: (i, 0, 0, 0))],
        out_shape=[jax.ShapeDtypeStruct((g, nc, lc), F32),
                   jax.ShapeDtypeStruct((g, 2, bz, p), F32)],
        scratch_shapes=[pltpu.VMEM((nc, p), F32)] * 4,
        compiler_params=_params("parallel"),
        name="s5_main",
    )(ug, wmat, pmat, qmat, al, h0, dt)

    y = y.reshape(g, nct, bz, chunk, gc).transpose(2, 1, 3, 0, 4).reshape(bz * t, a)
    hn = hn.transpose(1, 2, 0, 3)
    return y, hn[0], hn[1]


def _conv_kernel(v_ref, g_ref, buf_ref, w_ref, b_ref, lg_ref, lb_ref, y_ref, nbuf_ref, win_ref, acc_ref,
                 *, tt, width, nt):
    i = pl.program_id(1)
    keep = width - 1
    off = CONV_HALO - keep

    @pl.when(i == 0)
    def _():
        win_ref[0:off, :] = jnp.zeros((off, win_ref.shape[1]), F32)
        win_ref[off:CONV_HALO, :] = buf_ref[...]

    v = v_ref[...]
    win_ref[CONV_HALO:CONV_HALO + tt, :] = v * jax.nn.sigmoid(g_ref[...])

    bw = win_ref.shape[1]
    rb = min(tt, 64)
    for r0 in range(0, tt, rb):
        for c0 in range(0, bw, 128):
            acc = jnp.zeros((rb, 128), F32)
            for k in range(width):
                acc = acc + w_ref[k:k + 1, c0:c0 + 128] * win_ref[off + r0 + k:off + r0 + k + rb, c0:c0 + 128]
            acc_ref[r0:r0 + rb, c0:c0 + 128] = acc

    y = acc_ref[...] + b_ref[...]
    mu = jnp.mean(y, axis=-1, keepdims=True)
    var = jnp.mean(jnp.square(y - mu), axis=-1, keepdims=True)
    z = (y - mu) * lax.rsqrt(var + EPS) * lg_ref[...] + lb_ref[...]
    y_ref[...] = (z * jax.nn.sigmoid(z)).astype(y_ref.dtype)

    @pl.when(i == nt - 1)
    def _():
        nbuf_ref[...] = win_ref[tt + off:tt + CONV_HALO, :]

    tail = win_ref[tt:tt + CONV_HALO, :]
    win_ref[0:CONV_HALO, :] = tail


def conv_module(proj, buf, conv_w, conv_b, ln_g, ln_b, bz, t, a_width):
    width, bw = conv_w.shape
    assert a_width % bw == 0 and width - 1 <= CONV_HALO
    tt = _tile(t, 128)
    nt = t // tt
    vcol = a_width // bw
    proj = proj.reshape(bz, t, proj.shape[-1])
    y, nbuf = pl.pallas_call(
        functools.partial(_conv_kernel, tt=tt, width=width, nt=nt),
        grid=(bz, nt),
        in_specs=[pl.BlockSpec((None, tt, bw), lambda b, i: (b, i, vcol)),
                  pl.BlockSpec((None, tt, bw), lambda b, i: (b, i, vcol + 1)),
                  pl.BlockSpec((None, width - 1, bw), lambda b, i: (b, 0, 0)),
                  pl.BlockSpec((width, bw), lambda b, i: (0, 0)),
                  pl.BlockSpec((1, bw), lambda b, i: (0, 0)),
                  pl.BlockSpec((1, bw), lambda b, i: (0, 0)),
                  pl.BlockSpec((1, bw), lambda b, i: (0, 0))],
        out_specs=[pl.BlockSpec((None, tt, bw), lambda b, i: (b, i, 0)),
                   pl.BlockSpec((None, width - 1, bw), lambda b, i: (b, 0, 0))],
        out_shape=[jax.ShapeDtypeStruct((bz, t, bw), BF16),
                   jax.ShapeDtypeStruct((bz, width - 1, bw), F32)],
        scratch_shapes=[pltpu.VMEM((CONV_HALO + tt, bw), F32), pltpu.VMEM((tt, bw), F32)],
        compiler_params=_params("parallel", "arbitrary"),
        name="conv_module",
    )(proj, proj, buf, conv_w, conv_b.reshape(1, bw), ln_g.reshape(1, bw), ln_b.reshape(1, bw))
    return y.reshape(bz * t, bw), nbuf


def _diff_lambda(lam_ref, lam_init):
    s1 = jnp.sum(lam_ref[0:1, :] * lam_ref[1:2, :], axis=-1, keepdims=True)
    s2 = jnp.sum(lam_ref[2:3, :] * lam_ref[3:4, :], axis=-1, keepdims=True)
    return jnp.exp(s1) - jnp.exp(s2) + lam_init


def _flash_kernel(lam_ref, g_ref, q_ref, k_ref, v_ref, o_ref, qs_ref, m_ref, l_ref, acc_ref,
                  *, tq, hd, lam_init):
    qi, ki = pl.program_id(2), pl.program_id(3)

    @pl.when(ki == 0)
    def _():
        q = q_ref[...]
        lane = lax.broadcasted_iota(jnp.int32, q.shape, 1)
        zero = jnp.zeros_like(q)
        qs_ref[0:tq, :] = jnp.where(lane < hd, q, zero)
        qs_ref[tq:2 * tq, :] = jnp.where(lane >= hd, q, zero)
        m_ref[...] = jnp.full(m_ref.shape, NEG_INF, F32)
        l_ref[...] = jnp.zeros(l_ref.shape, F32)
        acc_ref[...] = jnp.zeros(acc_ref.shape, F32)

    def step(masked):
        s = lax.dot_general(qs_ref[...], k_ref[...], (((1,), (1,)), ((), ())),
                            preferred_element_type=F32)
        if masked:
            row = lax.broadcasted_iota(jnp.int32, s.shape, 0)
            col = lax.broadcasted_iota(jnp.int32, s.shape, 1)
            qpos = jnp.where(row >= tq, row - tq, row)
            s = jnp.where(col <= qpos, s, NEG_INF)
        m_old = m_ref[...]
        m_new = jnp.maximum(m_old, jnp.max(s, axis=-1, keepdims=True))
        alpha = jnp.exp(m_old - m_new)
        p = jnp.exp(s - m_new)
        l_ref[...] = alpha * l_ref[...] + jnp.sum(p, axis=-1, keepdims=True)
        acc_ref[...] = alpha * acc_ref[...] + jnp.dot(p.astype(BF16), v_ref[...],
                                                      preferred_element_type=F32)
        m_ref[...] = m_new

    @pl.when(ki < qi)
    def _():
        step(False)

    @pl.when(ki == qi)
    def _():
        step(True)
        o = acc_ref[...] / l_ref[...]
        lam = _diff_lambda(lam_ref, lam_init)
        d = o[0:tq] - lam * o[tq:2 * tq]
        o_ref[...] = (_rms_rows(d, g_ref[...]) * (1.0 - lam_init)).astype(o_ref.dtype)


def prompt_attention(q, k, v, lam_vecs, subln_g, bz, s, hd, lam_init):
    t, qw = q.shape
    vd = 2 * hd
    nh = qw // vd
    tq = _tile(s, 512)
    nq = s // tq
    return pl.pallas_call(
        functools.partial(_flash_kernel, tq=tq, hd=hd, lam_init=lam_init),
        grid=(bz, nh, nq, nq),
        in_specs=[pl.BlockSpec((4, hd), lambda b, h, qi, ki: (0, 0)),
                  pl.BlockSpec((1, vd), lambda b, h, qi, ki: (0, 0)),
                  pl.BlockSpec((tq, vd), lambda b, h, qi, ki: (b * nq + qi, h)),
                  pl.BlockSpec((tq, vd), lambda b, h, qi, ki: (b * nq + jnp.minimum(ki, qi), h)),
                  pl.BlockSpec((tq, vd), lambda b, h, qi, ki: (b * nq + jnp.minimum(ki, qi), h))],
        out_specs=pl.BlockSpec((tq, vd), lambda b, h, qi, ki: (b * nq + qi, h)),
        out_shape=jax.ShapeDtypeStruct((t, nh * vd), BF16),
        scratch_shapes=[pltpu.VMEM((2 * tq, vd), BF16), pltpu.VMEM((2 * tq, 1), F32),
                        pltpu.VMEM((2 * tq, 1), F32), pltpu.VMEM((2 * tq, vd), F32)],
        compiler_params=_params("parallel", "parallel", "parallel", "arbitrary"),
        name="prompt_attention",
    )(lam_vecs, subln_g.reshape(1, vd), q, k, v)


def _paged_kernel(pt_ref, lam_ref, g_ref, qidx_ref, q_ref, kc_ref, vc_ref, kn_ref, vn_ref, o_ref,
                  m_ref, l_ref, acc_ref, *, n_pages, nh, vd, rpb, lam_init):
    p = pl.program_id(1)

    @pl.when(p == 0)
    def _():
        m_ref[...] = jnp.full(m_ref.shape, NEG_INF, F32)
        l_ref[...] = jnp.zeros(l_ref.shape, F32)
        acc_ref[...] = jnp.zeros(acc_ref.shape, F32)

    def update(k, v, masked):
        s = lax.dot_general(q_ref[...], k.astype(BF16), (((1,), (1,)), ((), ())),
                            preferred_element_type=F32)
        if masked:
            col = lax.broadcasted_iota(jnp.int32, s.shape, 1)
            s = jnp.where(col <= qidx_ref[...], s, NEG_INF)
        m_old = m_ref[...]
        m_new = jnp.maximum(m_old, jnp.max(s, axis=-1, keepdims=True))
        alpha = jnp.exp(m_old - m_new)
        pr = jnp.exp(s - m_new)
        l_ref[...] = alpha * l_ref[...] + jnp.sum(pr, axis=-1, keepdims=True)
        pv = jnp.dot(pr.astype(BF16), v.astype(BF16), preferred_element_type=F32)
        for h in range(nh):
            rows = slice(h * rpb, (h + 1) * rpb)
            acc_ref[rows, :] = alpha[rows] * acc_ref[rows, :] + pv[rows, h * vd:(h + 1) * vd]
        m_ref[...] = m_new

    @pl.when(p < n_pages)
    def _():
        update(kc_ref[...], vc_ref[...], False)

    @pl.when(p == n_pages)
    def _():
        update(kn_ref[...], vn_ref[...], True)
        o = acc_ref[...] / l_ref[...]
        lam = _diff_lambda(lam_ref, lam_init)
        r = o.shape[0]
        d = o - lam * pltpu.roll(o, r - rpb // 2, axis=0)
        o_ref[...] = _rms_rows(d, g_ref[...]) * (1.0 - lam_init)


def sample_attention(q, k_new, v_new, cache_k, cache_v, layer, page_table, lam_vecs, subln_g,
                     bz, nq, hd, lam_init):
    vd = 2 * hd
    qw = q.shape[1]
    nh = qw // vd
    n_odd, n_pool, page = cache_k.shape[:3]
    n_pages = page_table.shape[1]
    assert nq <= page
    rpb = 2 * nq
    r = nh * rpb
    ck = cache_k.reshape(n_odd, n_pool, page, qw)
    cv = cache_v.reshape(n_odd, n_pool, page, nh * vd)
    q4 = q.reshape(bz, nq, nh, 2, hd).transpose(0, 2, 3, 1, 4)
    eye = jnp.eye(2 * nh, dtype=q.dtype).reshape(nh, 2, 1, 2 * nh, 1)
    qbd = (q4.reshape(bz, nh, 2, nq, 1, hd) * eye[None]).reshape(bz, r, qw)
    pad = ((0, 0), (0, page - nq), (0, 0))
    kn = jnp.pad(k_new.reshape(bz, nq, qw), pad)
    vn = jnp.pad(v_new.reshape(bz, nq, nh * vd), pad)
    qidx = (jnp.arange(r, dtype=jnp.int32) % nq).reshape(r, 1)
    pt = page_table.reshape(-1).astype(jnp.int32)

    def page_idx(b, p, pt):
        return pt[b * n_pages + jnp.minimum(p, n_pages - 1)]

    grid_spec = pltpu.PrefetchScalarGridSpec(
        num_scalar_prefetch=1,
        grid=(bz, n_pages + 1),
        in_specs=[pl.BlockSpec((4, hd), lambda b, p, pt: (0, 0)),
                  pl.BlockSpec((1, vd), lambda b, p, pt: (0, 0)),
                  pl.BlockSpec((r, 1), lambda b, p, pt: (0, 0)),
                  pl.BlockSpec((None, r, qw), lambda b, p, pt: (b, 0, 0)),
                  pl.BlockSpec((None, None, page, qw), lambda b, p, pt: (layer, page_idx(b, p, pt), 0, 0)),
                  pl.BlockSpec((None, None, page, nh * vd), lambda b, p, pt: (layer, page_idx(b, p, pt), 0, 0)),
                  pl.BlockSpec((None, page, qw), lambda b, p, pt: (b, 0, 0)),
                  pl.BlockSpec((None, page, nh * vd), lambda b, p, pt: (b, 0, 0))],
        out_specs=pl.BlockSpec((None, r, vd), lambda b, p, pt: (b, 0, 0)),
        scratch_shapes=[pltpu.VMEM((r, 1), F32), pltpu.VMEM((r, 1), F32), pltpu.VMEM((r, vd), F32)],
    )
    o = pl.pallas_call(
        functools.partial(_paged_kernel, n_pages=n_pages, nh=nh, vd=vd, rpb=rpb, lam_init=lam_init),
        grid_spec=grid_spec,
        out_shape=jax.ShapeDtypeStruct((bz, r, vd), F32),
        compiler_params=_params("parallel", "arbitrary"),
        name="sample_attention",
    )(pt, lam_vecs, subln_g.reshape(1, vd), qidx, qbd, ck, cv, kn, vn)
    o = o.reshape(bz, nh, 2, nq, vd)[:, :, 0]
    return o.transpose(0, 2, 1, 3).reshape(bz * nq, nh * vd).astype(BF16)


def kernel(x_prompt, x_sample, cache_k, cache_v, state_ssm_re, state_ssm_im, state_conv, page_table, norm_mix_pre, norm_mix_post, norm_ffn_pre, norm_ffn_post, w_in_even, ssm_lam_re, ssm_lam_im, ssm_log_dt, ssm_b_re, ssm_b_im, ssm_c_re, ssm_c_im, ssm_d, w_glu, conv_w, conv_b, conv_ln_g, conv_ln_b, w_out_even, w_qkv, lambda_q1, lambda_k1, lambda_q2, lambda_k2, subln_g, w_o, w_gate, w_up, w_down):
    depth = norm_mix_pre.shape[0]
    d_model = x_prompt.shape[-1]
    a_width = w_glu.shape[1]
    n_groups, n_state = ssm_lam_re.shape[1:]
    bw = conv_w.shape[2]
    page = cache_k.shape[2]
    head_dim = cache_k.shape[-1]
    q_width = cache_k.shape[-2] * head_dim
    past_len = page_table.shape[1] * page

    w_in_b, w_glu_b, w_out_b = w_in_even.astype(BF16), w_glu.astype(BF16), w_out_even.astype(BF16)
    w_qkv_b, w_o_b = w_qkv.astype(BF16), w_o.astype(BF16)
    w_gate_b, w_up_b, w_down_b = w_gate.astype(BF16), w_up.astype(BF16), w_down.astype(BF16)
    lam_vecs = jnp.stack([lambda_q1, lambda_k1, lambda_q2, lambda_k2], axis=1)

    def run_trunk(x, pos0, ssm_re0, ssm_im0, conv0, paged):
        bz, t, _ = x.shape
        h = x.reshape(bz * t, d_model)
        pos = jnp.tile(pos0 + jnp.arange(t, dtype=F32), bz).reshape(bz * t, 1)
        new_re, new_im, new_conv, new_k, new_v = [], [], [], [], []
        for i in range(depth):
            j = i // 2
            if i % 2 == 0:
                proj = norm_matmul(h, norm_mix_pre[i], w_in_b[j])
                y_a, s_re, s_im = s5_mixer(
                    proj[:, :a_width].reshape(bz, t, a_width), ssm_re0[j], ssm_im0[j],
                    ssm_lam_re[j], ssm_lam_im[j], ssm_log_dt[j], ssm_b_re[j], ssm_b_im[j],
                    ssm_c_re[j], ssm_c_im[j], ssm_d[j])
                y_a = s5_glu(y_a, w_glu_b[j])
                y_b, buf = conv_module(proj, conv0[j], conv_w[j], conv_b[j], conv_ln_g[j],
                                       conv_ln_b[j], bz, t, a_width)
                new_re.append(s_re)
                new_im.append(s_im)
                new_conv.append(buf)
                mix_in, w_mix = jnp.concatenate([y_a, y_b], axis=-1), w_out_b[j]
            else:
                lam_init = 0.8 - 0.6 * math.exp(-0.3 * i)
                q, k, v, kb, vb = qkv_rope(h, norm_mix_pre[i], w_qkv_b[j], pos, q_width, head_dim)
                if paged:
                    mix_in = sample_attention(q, k, v, cache_k, cache_v, j, page_table, lam_vecs[j],
                                              subln_g[j], bz, t, head_dim, lam_init)
                else:
                    mix_in = prompt_attention(q, kb, vb, lam_vecs[j], subln_g[j], bz, t, head_dim,
                                              lam_init)
                new_k.append(k.reshape(bz, t, q_width // head_dim, head_dim))
                new_v.append(v.reshape(bz, t, -1, 2 * head_dim))
                w_mix = w_o_b[j]
            h = matmul_norm_residual(mix_in, w_mix, norm_mix_post[i], h)
            h = ffn(h, norm_ffn_pre[i], norm_ffn_post[i], w_gate_b[i], w_up_b[i], w_down_b[i])
        return (h.reshape(bz, t, d_model), jnp.stack(new_re), jnp.stack(new_im), jnp.stack(new_conv),
                jnp.stack(new_k), jnp.stack(new_v))

    n_prompt = x_prompt.shape[0]
    n_even = state_ssm_re.shape[0]
    zero_ssm = jnp.zeros((n_even, n_prompt, n_groups, n_state), F32)
    zero_conv = jnp.zeros((n_even, n_prompt, conv_w.shape[1] - 1, bw), F32)
    y_p, re_p, im_p, conv_p, k_p, v_p = run_trunk(x_prompt, 0.0, zero_ssm, zero_ssm, zero_conv, False)
    y_s, re_s, im_s, conv_s, k_s, v_s = run_trunk(x_sample, float(past_len), state_ssm_re, state_ssm_im,
                                                  state_conv, True)
    return (y_p, y_s, re_p, im_p, conv_p, k_p, v_p, re_s, im_s, conv_s, k_s, v_s)
```

```python
import functools
import math

import jax
import jax.numpy as jnp
from jax import lax
from jax.experimental import pallas as pl
from jax.experimental.pallas import tpu as pltpu

F32 = jnp.float32
BF16 = jnp.bfloat16
EPS = 1e-6
NEG_INF = -1e30
ROPE_THETA = 500000.0
S5_CHUNK = 16
CONV_HALO = 32
LANES = 128
FLASH_ROWS = 256
VMEM_LIMIT = 56 * 1024 * 1024
HI = lax.Precision.HIGHEST


def _tile(n, pref):
    if n <= pref:
        return n
    t = pref
    while t >= 8:
        if n % t == 0:
            return t
        t //= 2
    return n


def _params(*sem):
    return pltpu.CompilerParams(dimension_semantics=sem, vmem_limit_bytes=VMEM_LIMIT)


def _rms_rows(x, g, eps=EPS):
    ms = jnp.mean(x * x, axis=-1, keepdims=True)
    return x * lax.rsqrt(ms + eps) * g


def _norm_matmul_kernel(x_ref, g_ref, w_ref, o_ref, hn_ref):
    @pl.when(pl.program_id(1) == 0)
    def _():
        hn_ref[...] = _rms_rows(x_ref[...], g_ref[...]).astype(BF16)

    o_ref[...] = jnp.dot(hn_ref[...], w_ref[...], preferred_element_type=F32)


def norm_matmul(x, g, w):
    t, d = x.shape
    n = w.shape[1]
    tm, tn = _tile(t, 1024), _tile(n, 512)
    return pl.pallas_call(
        _norm_matmul_kernel,
        grid=(t // tm, n // tn),
        in_specs=[pl.BlockSpec((tm, d), lambda i, j: (i, 0)),
                  pl.BlockSpec((1, d), lambda i, j: (0, 0)),
                  pl.BlockSpec((d, tn), lambda i, j: (0, j))],
        out_specs=pl.BlockSpec((tm, tn), lambda i, j: (i, j)),
        out_shape=jax.ShapeDtypeStruct((t, n), F32),
        scratch_shapes=[pltpu.VMEM((tm, d), BF16)],
        compiler_params=_params("parallel", "arbitrary"),
        name="norm_matmul",
    )(x, g.reshape(1, d), w)


def _qkv_kernel(x_ref, g_ref, pos_ref, invf_ref, sel_ref, w_ref,
                q_ref, k_ref, v_ref, kb_ref, vb_ref, hn_ref, tab_ref, *, nq_tiles, scale):
    j = pl.program_id(1)

    @pl.when(j == 0)
    def _():
        hn_ref[...] = _rms_rows(x_ref[...], g_ref[...]).astype(BF16)
        ang = pos_ref[...] * invf_ref[...]
        c, s = jnp.cos(ang), jnp.sin(ang)
        tab_ref[0] = jnp.where(sel_ref[0:1, :] > 0.5, c, 1.0)
        tab_ref[1] = s * sel_ref[1:2, :]
        tab_ref[2] = s * sel_ref[2:3, :]

    y = jnp.dot(hn_ref[...], w_ref[...], preferred_element_type=F32)
    tn = y.shape[1]
    half = 128 // 16

    def rope(y):
        parts = []
        for c0 in range(0, tn, 128):
            x = y[:, c0:c0 + 128]
            parts.append(x * tab_ref[0] + pltpu.roll(x, half, axis=1) * tab_ref[1]
                         + pltpu.roll(x, 128 - half, axis=1) * tab_ref[2])
        return jnp.concatenate(parts, axis=1) if len(parts) > 1 else parts[0]

    @pl.when(j < nq_tiles)
    def _():
        q_ref[...] = (rope(y) * scale).astype(BF16)

    @pl.when(jnp.logical_and(j >= nq_tiles, j < 2 * nq_tiles))
    def _():
        r = rope(y)
        k_ref[...] = r
        kb_ref[...] = r.astype(BF16)

    @pl.when(j >= 2 * nq_tiles)
    def _():
        v_ref[...] = y
        vb_ref[...] = y.astype(BF16)


def qkv_rope(x, g, w, pos, q_width, head_dim):
    t, d = x.shape
    n = w.shape[1]
    v_width = n - 2 * q_width
    rot = head_dim // 4
    assert rot == 16 and 128 % head_dim == 0
    tm = _tile(t, 1024)
    tn = _tile(math.gcd(q_width, v_width), 512)
    nq, nv = q_width // tn, v_width // tn
    lane = jnp.arange(128) % head_dim
    inv_freq = ROPE_THETA ** (-jnp.arange(rot // 2, dtype=F32) * 2.0 / rot)
    invf = jnp.where(lane < rot, inv_freq[lane % (rot // 2)], 0.0).reshape(1, 128).astype(F32)
    sel = jnp.stack([(lane < rot).astype(F32),
                     jnp.logical_and(lane >= rot // 2, lane < rot).astype(F32),
                     -(lane < rot // 2).astype(F32)])

    def clip(j, lo, cnt):
        return jnp.clip(j - lo, 0, cnt - 1)

    outs = pl.pallas_call(
        functools.partial(_qkv_kernel, nq_tiles=nq, scale=head_dim ** -0.5),
        grid=(t // tm, 2 * nq + nv),
        in_specs=[pl.BlockSpec((tm, d), lambda i, j: (i, 0)),
                  pl.BlockSpec((1, d), lambda i, j: (0, 0)),
                  pl.BlockSpec((tm, 1), lambda i, j: (i, 0)),
                  pl.BlockSpec((1, 128), lambda i, j: (0, 0)),
                  pl.BlockSpec((3, 128), lambda i, j: (0, 0)),
                  pl.BlockSpec((d, tn), lambda i, j: (0, j))],
        out_specs=[pl.BlockSpec((tm, tn), lambda i, j: (i, clip(j, 0, nq))),
                   pl.BlockSpec((tm, tn), lambda i, j: (i, clip(j, nq, nq))),
                   pl.BlockSpec((tm, tn), lambda i, j: (i, clip(j, 2 * nq, nv))),
                   pl.BlockSpec((tm, tn), lambda i, j: (i, clip(j, nq, nq))),
                   pl.BlockSpec((tm, tn), lambda i, j: (i, clip(j, 2 * nq, nv)))],
        out_shape=[jax.ShapeDtypeStruct((t, q_width), BF16),
                   jax.ShapeDtypeStruct((t, q_width), F32),
                   jax.ShapeDtypeStruct((t, v_width), F32),
                   jax.ShapeDtypeStruct((t, q_width), BF16),
                   jax.ShapeDtypeStruct((t, v_width), BF16)],
        scratch_shapes=[pltpu.VMEM((tm, d), BF16), pltpu.VMEM((3, tm, 128), F32)],
        compiler_params=_params("parallel", "arbitrary"),
        name="qkv_rope",
    )(x, g.reshape(1, d), pos, invf, sel, w)
    return outs


def _matmul_norm_res_kernel(a_ref, w_ref, g_ref, h_ref, o_ref, acc_ref, *, nj, tn):
    j = pl.program_id(1)
    acc_ref[j] = jnp.dot(a_ref[...], w_ref[...], preferred_element_type=F32)

    @pl.when(j == nj - 1)
    def _():
        ss = jnp.zeros((acc_ref.shape[1], 1), F32)
        for jj in range(nj):
            y = acc_ref[jj]
            ss = ss + jnp.sum(y * y, axis=-1, keepdims=True)
        inv = lax.rsqrt(ss / (nj * tn) + EPS)
        for jj in range(nj):
            sl = slice(jj * tn, (jj + 1) * tn)
            o_ref[:, sl] = h_ref[:, sl] + acc_ref[jj] * inv * g_ref[:, sl]


def matmul_norm_residual(a, w, g, h):
    t, k = a.shape
    d = w.shape[1]
    tm, tn = _tile(t, 512), _tile(d, 512)
    nj = d // tn
    return pl.pallas_call(
        functools.partial(_matmul_norm_res_kernel, nj=nj, tn=tn),
        grid=(t // tm, nj),
        in_specs=[pl.BlockSpec((tm, k), lambda i, j: (i, 0)),
                  pl.BlockSpec((k, tn), lambda i, j: (0, j)),
                  pl.BlockSpec((1, d), lambda i, j: (0, 0)),
                  pl.BlockSpec((tm, d), lambda i, j: (i, 0))],
        out_specs=pl.BlockSpec((tm, d), lambda i, j: (i, 0)),
        out_shape=jax.ShapeDtypeStruct((t, d), F32),
        scratch_shapes=[pltpu.VMEM((nj, tm, tn), F32)],
        compiler_params=_params("parallel", "arbitrary"),
        name="matmul_norm_residual",
    )(a, w, g.reshape(1, d), h)


def _ffn_kernel(h_ref, gpre_ref, gpost_ref, wg_ref, wu_ref, wd_ref, o_ref, hn_ref, *, nf):
    f = pl.program_id(1)

    @pl.when(f == 0)
    def _():
        hn_ref[...] = _rms_rows(h_ref[...], gpre_ref[...]).astype(BF16)
        o_ref[...] = jnp.zeros(o_ref.shape, F32)

    hn = hn_ref[...]
    gate = jnp.dot(hn, wg_ref[...], preferred_element_type=F32)
    up = jnp.dot(hn, wu_ref[...], preferred_element_type=F32)
    act = (gate * jax.nn.sigmoid(gate) * up).astype(BF16)
    o_ref[...] += jnp.dot(act, wd_ref[...], preferred_element_type=F32)

    @pl.when(f == nf - 1)
    def _():
        o_ref[...] = h_ref[...] + _rms_rows(o_ref[...], gpost_ref[...])


def ffn(h, g_pre, g_post, wg, wu, wd):
    t, d = h.shape
    fh = wg.shape[1]
    tm = _tile(t, 1024)
    tf = 256 if fh % 256 == 0 else fh
    nf = fh // tf
    return pl.pallas_call(
        functools.partial(_ffn_kernel, nf=nf),
        grid=(t // tm, nf),
        in_specs=[pl.BlockSpec((tm, d), lambda i, f: (i, 0), pipeline_mode=pl.Buffered(1)),
                  pl.BlockSpec((1, d), lambda i, f: (0, 0)),
                  pl.BlockSpec((1, d), lambda i, f: (0, 0)),
                  pl.BlockSpec((d, tf), lambda i, f: (0, f)),
                  pl.BlockSpec((d, tf), lambda i, f: (0, f)),
                  pl.BlockSpec((tf, d), lambda i, f: (f, 0))],
        out_specs=pl.BlockSpec((tm, d), lambda i, f: (i, 0)),
        out_shape=jax.ShapeDtypeStruct((t, d), F32),
        scratch_shapes=[pltpu.VMEM((tm, d), BF16)],
        compiler_params=_params("parallel", "arbitrary"),
        name="ffn",
    )(h, g_pre.reshape(1, d), g_post.reshape(1, d), wg, wu, wd)


def _glu_kernel(y_ref, w_ref, o_ref):
    y = y_ref[...]
    z = jnp.dot(y.astype(BF16), w_ref[...], preferred_element_type=F32)
    o_ref[...] = (y * jax.nn.sigmoid(z)).astype(o_ref.dtype)


def s5_glu(y, w):
    t, a = y.shape
    tm = _tile(t, 1024)
    return pl.pallas_call(
        _glu_kernel,
        grid=(t // tm,),
        in_specs=[pl.BlockSpec((tm, a), lambda i: (i, 0)),
                  pl.BlockSpec((a, a), lambda i: (0, 0))],
        out_specs=pl.BlockSpec((tm, a), lambda i: (i, 0)),
        out_shape=jax.ShapeDtypeStruct((t, a), BF16),
        compiler_params=_params("parallel"),
        name="s5_glu",
    )(y, w)


def _s5_prep_kernel(lam_row_ref, lam_col_ref, bt_ref, ct_ref, e_ref, tt_ref,
                    g0_ref, p_ref, q_ref, al_ref, *, chunk):
    lr, li, ldt = lam_row_ref[0:1, :], lam_row_ref[1:2, :], lam_row_ref[2:3, :]
    dt = jnp.exp(ldt)
    mag = jnp.exp(lr * dt)
    ab_re, ab_im = mag * jnp.cos(li * dt), mag * jnp.sin(li * dt)
    den = lr * lr + li * li
    num_re = ab_re - 1.0
    coef_re = (num_re * lr + ab_im * li) / den
    coef_im = (ab_im * lr - num_re * li) / den
    bt_re, bt_im = bt_ref[0], bt_ref[1]
    bb_re = coef_re * bt_re - coef_im * bt_im
    bb_im = coef_re * bt_im + coef_im * bt_re

    magl = jnp.exp(lr * dt * chunk)
    al_ref[0:1, :] = magl * jnp.cos(li * dt * chunk)
    al_ref[1:2, :] = magl * jnp.sin(li * dt * chunk)

    e = e_ref[...]
    pm = jnp.exp(lr * dt * e)
    pw_re, pw_im = pm * jnp.cos(li * dt * e), pm * jnp.sin(li * dt * e)
    bbt_re = jnp.concatenate([bb_re] * chunk, axis=0)
    bbt_im = jnp.concatenate([bb_im] * chunk, axis=0)
    p_ref[0] = pw_re * bbt_re - pw_im * bbt_im
    p_ref[1] = pw_re * bbt_im + pw_im * bbt_re

    lrc, lic, ldtc = lam_col_ref[0], lam_col_ref[1], lam_col_ref[2]
    dtc = jnp.exp(ldtc)
    tt = tt_ref[...]
    fm = jnp.exp(lrc * dtc * tt)
    fw_re, fw_im = fm * jnp.cos(lic * dtc * tt), fm * jnp.sin(lic * dtc * tt)
    ct_re, ct_im = ct_ref[0], ct_ref[1]
    f_re = ct_re * fw_re - ct_im * fw_im
    f_im = ct_re * fw_im + ct_im * fw_re
    g0_ref[...] = (jnp.dot(bb_re, f_re, precision=HI, preferred_element_type=F32)
                   - jnp.dot(bb_im, f_im, precision=HI, preferred_element_type=F32))
    magc = jnp.exp(lrc * dtc)
    abc_re, abc_im = magc * jnp.cos(lic * dtc), magc * jnp.sin(lic * dtc)
    q_ref[0] = f_re * abc_re - f_im * abc_im
    q_ref[1] = -(f_re * abc_im + f_im * abc_re)


def _s5_main_kernel(u_ref, w_ref, p_ref, q_ref, al_ref, h0_ref, d_ref,
                    y_ref, hn_ref, bc_re, bc_im, xs_re, xs_im, *, nct, bz):
    u = u_ref[...]
    bc_re[...] = jnp.dot(u, p_ref[0], precision=HI, preferred_element_type=F32)
    bc_im[...] = jnp.dot(u, p_ref[1], precision=HI, preferred_element_type=F32)
    a_re, a_im = al_ref[0:1, :], al_ref[1:2, :]

    def step(j, carry):
        x_re, x_im = carry
        rows = pl.ds(j * bz, bz)
        xs_re[rows, :] = x_re
        xs_im[rows, :] = x_im
        n_re = a_re * x_re - a_im * x_im + bc_re[rows, :]
        n_im = a_re * x_im + a_im * x_re + bc_im[rows, :]
        return n_re, n_im

    x_re, x_im = lax.fori_loop(0, nct, step, (h0_ref[0], h0_ref[1]))
    hn_ref[0] = x_re
    hn_ref[1] = x_im
    y = (jnp.dot(u, w_ref[...], precision=HI, preferred_element_type=F32)
         + jnp.dot(xs_re[...], q_ref[0], precision=HI, preferred_element_type=F32)
         + jnp.dot(xs_im[...], q_ref[1], precision=HI, preferred_element_type=F32)
         + d_ref[...] * u)
    y_ref[...] = jax.nn.gelu(y)


def s5_mixer(u, h0_re, h0_im, lam_re, lam_im, log_dt, b_re, b_im, c_re, c_im, d_skip):
    bz, t, a = u.shape
    g, p = lam_re.shape
    gc = a // g
    chunk = S5_CHUNK if t % S5_CHUNK == 0 else t
    nct = t // chunk
    nc, lc = nct * bz, chunk * gc

    lam_row = jnp.stack([lam_re, lam_im, jnp.broadcast_to(log_dt[:, None], (g, p))], axis=1)
    lam_col = jnp.broadcast_to(lam_row[:, :, :, None], (g, 3, p, lc))
    bt = jnp.stack([b_re, b_im], axis=1).transpose(0, 1, 3, 2)
    ct = jnp.tile(jnp.stack([c_re, c_im], axis=1).transpose(0, 1, 3, 2), (1, 1, 1, chunk))

    step = jnp.arange(lc, dtype=jnp.int32) // gc
    e_col = (chunk - 1 - step).astype(F32).reshape(lc, 1)
    tt_row = step.astype(F32).reshape(1, lc)

    g0, pmat, qmat, al = pl.pallas_call(
        functools.partial(_s5_prep_kernel, chunk=chunk),
        grid=(g,),
        in_specs=[pl.BlockSpec((None, 3, p), lambda i: (i, 0, 0)),
                  pl.BlockSpec((None, 3, p, lc), lambda i: (i, 0, 0, 0)),
                  pl.BlockSpec((None, 2, gc, p), lambda i: (i, 0, 0, 0)),
                  pl.BlockSpec((None, 2, p, lc), lambda i: (i, 0, 0, 0)),
                  pl.BlockSpec((lc, 1), lambda i: (0, 0)),
                  pl.BlockSpec((1, lc), lambda i: (0, 0))],
        out_specs=[pl.BlockSpec((None, gc, lc), lambda i: (i, 0, 0)),
                   pl.BlockSpec((None, 2, lc, p), lambda i: (i, 0, 0, 0)),
                   pl.BlockSpec((None, 2, p, lc), lambda i: (i, 0, 0, 0)),
                   pl.BlockSpec((None, 2, p), lambda i: (i, 0, 0))],
        out_shape=[jax.ShapeDtypeStruct((g, gc, lc), F32),
                   jax.ShapeDtypeStruct((g, 2, lc, p), F32),
                   jax.ShapeDtypeStruct((g, 2, p, lc), F32),
                   jax.ShapeDtypeStruct((g, 2, p), F32)],
        compiler_params=_params("parallel"),
        name="s5_prep",
    )(lam_row, lam_col, bt, ct, e_col, tt_row)

    g0p = jnp.pad(g0.reshape(g, gc, chunk, gc), ((0, 0), (0, 0), (chunk, 0), (0, 0)))
    wmat = jnp.stack([g0p[:, :, chunk - s:2 * chunk - s, :] for s in range(chunk)], axis=1).reshape(g, lc, lc)

    ug = u.reshape(bz, nct, chunk, g, gc).transpose(3, 1, 0, 2, 4).reshape(g, nc, lc)
    h0 = jnp.stack([h0_re, h0_im], axis=0).transpose(2, 0, 1, 3)
    dt = jnp.tile(d_skip.reshape(g, 1, gc), (1, 1, chunk))

    y, hn = pl.pallas_call(
        functools.partial(_s5_main_kernel, nct=nct, bz=bz),
        grid=(g,),
        in_specs=[pl.BlockSpec((None, nc, lc), lambda i: (i, 0, 0)),
                  pl.BlockSpec((None, lc, lc), lambda i: (i, 0, 0)),
                  pl.BlockSpec((None, 2, lc, p), lambda i: (i, 0, 0, 0)),
                  pl.BlockSpec((None, 2, p, lc), lambda i: (i, 0, 0, 0)),
                  pl.BlockSpec((None, 2, p), lambda i: (i, 0, 0)),
                  pl.BlockSpec((None, 2, bz, p), lambda i: (i, 0, 0, 0)),
                  pl.BlockSpec((None, 1, lc), lambda i: (i, 0, 0))],
        out_specs=[pl.BlockSpec((None, nc, lc), lambda i: (i, 0, 0)),
                   pl.BlockSpec((None, 2, bz, p), lambda i: (i, 0, 0, 0))],
        out_shape=[jax.ShapeDtypeStruct((g, nc, lc), F32),
                   jax.ShapeDtypeStruct((g, 2, bz, p), F32)],
        scratch_shapes=[pltpu.VMEM((nc, p), F32)] * 4,
        compiler_params=_params("parallel"),
        name="s5_main",
    )(ug, wmat, pmat, qmat, al, h0, dt)

    y = y.reshape(g, nct, bz, chunk, gc).transpose(2, 1, 3, 0, 4).reshape(bz * t, a)
    hn = hn.transpose(1, 2, 0, 3)
    return y, hn[0], hn[1]


def _conv_kernel(v_ref, g_ref, buf_ref, w_ref, b_ref, lg_ref, lb_ref, y_ref, nbuf_ref, win_ref, acc_ref,
                 *, tt, width, nt):
    i = pl.program_id(1)
    keep = width - 1
    off = CONV_HALO - keep

    @pl.when(i == 0)
    def _():
        win_ref[0:off, :] = jnp.zeros((off, win_ref.shape[1]), F32)
        win_ref[off:CONV_HALO, :] = buf_ref[...]

    v = v_ref[...]
    win_ref[CONV_HALO:CONV_HALO + tt, :] = v * jax.nn.sigmoid(g_ref[...])

    bw = win_ref.shape[1]
    rb = min(tt, 64)
    for r0 in range(0, tt, rb):
        for c0 in range(0, bw, 128):
            acc = jnp.zeros((rb, 128), F32)
            for k in range(width):
                acc = acc + w_ref[k:k + 1, c0:c0 + 128] * win_ref[off + r0 + k:off + r0 + k + rb, c0:c0 + 128]
            acc_ref[r0:r0 + rb, c0:c0 + 128] = acc

    y = acc_ref[...] + b_ref[...]
    mu = jnp.mean(y, axis=-1, keepdims=True)
    var = jnp.mean(jnp.square(y - mu), axis=-1, keepdims=True)
    z = (y - mu) * lax.rsqrt(var + EPS) * lg_ref[...] + lb_ref[...]
    y_ref[...] = (z * jax.nn.sigmoid(z)).astype(y_ref.dtype)

    @pl.when(i == nt - 1)
    def _():
        nbuf_ref[...] = win_ref[tt + off:tt + CONV_HALO, :]

    tail = win_ref[tt:tt + CONV_HALO, :]
    win_ref[0:CONV_HALO, :] = tail


def conv_module(proj, buf, conv_w, conv_b, ln_g, ln_b, bz, t, a_width):
    width, bw = conv_w.shape
    assert a_width % bw == 0 and width - 1 <= CONV_HALO
    tt = _tile(t, 128)
    nt = t // tt
    vcol = a_width // bw
    proj = proj.reshape(bz, t, proj.shape[-1])
    y, nbuf = pl.pallas_call(
        functools.partial(_conv_kernel, tt=tt, width=width, nt=nt),
        grid=(bz, nt),
        in_specs=[pl.BlockSpec((None, tt, bw), lambda b, i: (b, i, vcol)),
                  pl.BlockSpec((None, tt, bw), lambda b, i: (b, i, vcol + 1)),
                  pl.BlockSpec((None, width - 1, bw), lambda b, i: (b, 0, 0)),
                  pl.BlockSpec((width, bw), lambda b, i: (0, 0)),
                  pl.BlockSpec((1, bw), lambda b, i: (0, 0)),
                  pl.BlockSpec((1, bw), lambda b, i: (0, 0)),
                  pl.BlockSpec((1, bw), lambda b, i: (0, 0))],
        out_specs=[pl.BlockSpec((None, tt, bw), lambda b, i: (b, i, 0)),
                   pl.BlockSpec((None, width - 1, bw), lambda b, i: (b, 0, 0))],
        out_shape=[jax.ShapeDtypeStruct((bz, t, bw), BF16),
                   jax.ShapeDtypeStruct((bz, width - 1, bw), F32)],
        scratch_shapes=[pltpu.VMEM((CONV_HALO + tt, bw), F32), pltpu.VMEM((tt, bw), F32)],
        compiler_params=_params("parallel", "arbitrary"),
        name="conv_module",
    )(proj, proj, buf, conv_w, conv_b.reshape(1, bw), ln_g.reshape(1, bw), ln_b.reshape(1, bw))
    return y.reshape(bz * t, bw), nbuf


def _diff_lambda(lam_ref, lam_init):
    s1 = jnp.sum(lam_ref[0:1, :] * lam_ref[1:2, :], axis=-1, keepdims=True)
    s2 = jnp.sum(lam_ref[2:3, :] * lam_ref[3:4, :], axis=-1, keepdims=True)
    return jnp.exp(s1) - jnp.exp(s2) + lam_init


def _flash_kernel(lam_ref, g_ref, q_ref, k_ref, v_ref, o_ref, qs_ref, m_ref, l_ref, acc_ref,
                  *, tq, hd, lam_init):
    qi, ki = pl.program_id(2), pl.program_id(3)
    rc = min(tq, FLASH_ROWS)

    @pl.when(ki == 0)
    def _():
        q = q_ref[...]
        lane = lax.broadcasted_iota(jnp.int32, q.shape, 1)
        zero = jnp.zeros_like(q)
        qs_ref[0:tq, :] = jnp.where(lane < hd, q, zero)
        qs_ref[tq:2 * tq, :] = jnp.where(lane >= hd, q, zero)
        m_ref[...] = jnp.full(m_ref.shape, NEG_INF, F32)
        l_ref[...] = jnp.zeros(l_ref.shape, F32)
        acc_ref[...] = jnp.zeros(acc_ref.shape, F32)

    def step(masked):
        n_chunks = 2 * tq // rc

        def scores(c):
            q0 = (c * rc) % tq
            kc = q0 + rc if masked else tq
            return lax.dot_general(qs_ref[c * rc:(c + 1) * rc, :], k_ref[0:kc, :],
                                   (((1,), (1,)), ((), ())), preferred_element_type=F32)

        s_next = scores(0)
        for c in range(n_chunks):
            rows = slice(c * rc, (c + 1) * rc)
            q0 = (c * rc) % tq
            kc = q0 + rc if masked else tq
            s = s_next
            if c + 1 < n_chunks:
                s_next = scores(c + 1)
            if masked:
                row = lax.broadcasted_iota(jnp.int32, s.shape, 0) + q0
                col = lax.broadcasted_iota(jnp.int32, s.shape, 1)
                s = jnp.where(col <= row, s, NEG_INF)
            cols = [s[:, j:j + LANES] for j in range(0, kc, LANES)]
            m_old = m_ref[rows, :]
            m_new = jnp.maximum(m_old, jnp.max(functools.reduce(jnp.maximum, cols), axis=-1, keepdims=True))
            alpha = jnp.exp(m_old - m_new)
            ps = [jnp.exp(cj - m_new) for cj in cols]
            l_ref[rows, :] = alpha * l_ref[rows, :] + jnp.sum(functools.reduce(jnp.add, ps), axis=-1,
                                                               keepdims=True)
            p = jnp.concatenate(ps, axis=1) if len(ps) > 1 else ps[0]
            acc_ref[rows, :] = alpha * acc_ref[rows, :] + jnp.dot(p.astype(BF16), v_ref[0:kc, :],
                                                                  preferred_element_type=F32)
            m_ref[rows, :] = m_new

    @pl.when(ki < qi)
    def _():
        step(False)

    @pl.when(ki == qi)
    def _():
        step(True)
        o = acc_ref[...] / l_ref[...]
        lam = _diff_lambda(lam_ref, lam_init)
        d = o[0:tq] - lam * o[tq:2 * tq]
        o_ref[...] = (_rms_rows(d, g_ref[...]) * (1.0 - lam_init)).astype(o_ref.dtype)


def prompt_attention(q, k, v, lam_vecs, subln_g, bz, s, hd, lam_init):
    t, qw = q.shape
    vd = 2 * hd
    nh = qw // vd
    tq = _tile(s, 512)
    nq = s // tq
    assert vd == LANES and tq % min(tq, FLASH_ROWS) == 0 and min(tq, FLASH_ROWS) % LANES == 0
    return pl.pallas_call(
        functools.partial(_flash_kernel, tq=tq, hd=hd, lam_init=lam_init),
        grid=(bz, nh, nq, nq),
        in_specs=[pl.BlockSpec((4, hd), lambda b, h, qi, ki: (0, 0)),
                  pl.BlockSpec((1, vd), lambda b, h, qi, ki: (0, 0)),
                  pl.BlockSpec((tq, vd), lambda b, h, qi, ki: (b * nq + qi, h)),
                  pl.BlockSpec((tq, vd), lambda b, h, qi, ki: (b * nq + jnp.minimum(ki, qi), h)),
                  pl.BlockSpec((tq, vd), lambda b, h, qi, ki: (b * nq + jnp.minimum(ki, qi), h))],
        out_specs=pl.BlockSpec((tq, vd), lambda b, h, qi, ki: (b * nq + qi, h)),
        out_shape=jax.ShapeDtypeStruct((t, nh * vd), BF16),
        scratch_shapes=[pltpu.VMEM((2 * tq, vd), BF16), pltpu.VMEM((2 * tq, LANES), F32),
                        pltpu.VMEM((2 * tq, LANES), F32), pltpu.VMEM((2 * tq, vd), F32)],
        compiler_params=_params("parallel", "parallel", "parallel", "arbitrary"),
        name="prompt_attention",
    )(lam_vecs, subln_g.reshape(1, vd), q, k, v)


def _paged_kernel(pt_ref, lam_ref, g_ref, qidx_ref, q_ref, e_ref, msk_ref, *refs,
                  n_steps, pp, rpb, lam_init):
    k_refs, v_refs = refs[:pp], refs[pp:2 * pp]
    kn_ref, vn_ref, o_ref, m_ref, l_ref, acc_ref = refs[2 * pp:]
    p = pl.program_id(1)

    @pl.when(p == 0)
    def _():
        m_ref[...] = jnp.full(m_ref.shape, NEG_INF, F32)
        l_ref[...] = jnp.zeros(l_ref.shape, F32)
        acc_ref[...] = jnp.zeros(acc_ref.shape, F32)

    def update(kt, vf, masked):
        s = jnp.dot(q_ref[...], kt.astype(BF16), preferred_element_type=F32)
        if masked:
            col = lax.broadcasted_iota(jnp.int32, s.shape, 1)
            s = jnp.where(col <= qidx_ref[...], s, NEG_INF)
        m_old = m_ref[...]
        m_new = jnp.maximum(m_old, jnp.max(s, axis=-1, keepdims=True))
        alpha = jnp.exp(m_old - m_new)
        pr = jnp.exp(s - m_new)
        l_ref[...] = alpha * l_ref[...] + jnp.sum(pr, axis=-1, keepdims=True)
        pe = jnp.dot(pr.astype(BF16), e_ref[...], preferred_element_type=F32)
        pe = pe.astype(BF16) * msk_ref[...]
        pv = jnp.dot(pe, vf.astype(BF16), preferred_element_type=F32)
        acc_ref[...] = alpha * acc_ref[...] + pv
        m_ref[...] = m_new

    @pl.when(p < n_steps)
    def _():
        for kr, vr in zip(k_refs, v_refs):
            update(kr[...], vr[...], False)

    @pl.when(p == n_steps)
    def _():
        update(kn_ref[...], vn_ref[...], True)
        o = acc_ref[...] / l_ref[...]
        lam = _diff_lambda(lam_ref, lam_init)
        r = o.shape[0]
        d = o - lam * pltpu.roll(o, r - rpb // 2, axis=0)
        o_ref[...] = _rms_rows(d, g_ref[...]) * (1.0 - lam_init)


def sample_attention(q, k_new, v_new, cache_k, cache_v, layer, page_table, lam_vecs, subln_g,
                     bz, nq, hd, lam_init):
    vd = 2 * hd
    qw = q.shape[1]
    nh = qw // vd
    n_odd, n_pool, page = cache_k.shape[:3]
    n_pages = page_table.shape[1]
    assert nq <= page
    pp = next(c for c in (4, 2, 1) if n_pages % c == 0 and n_pages // c >= min(2, n_pages))
    n_steps = n_pages // pp
    rpb = 2 * nq
    r = nh * rpb
    ckt = cache_k.transpose(0, 1, 3, 4, 2).reshape(n_odd, n_pool, qw, page)
    cvf = cache_v.reshape(n_odd, n_pool, page * nh, vd)
    q4 = q.reshape(bz, nq, nh, 2, hd).transpose(0, 2, 3, 1, 4)
    eye = jnp.eye(2 * nh, dtype=q.dtype).reshape(nh, 2, 1, 2 * nh, 1)
    qbd = (q4.reshape(bz, nh, 2, nq, 1, hd) * eye[None]).reshape(bz, r, qw)
    knt = jnp.pad(k_new.reshape(bz, nq, qw).transpose(0, 2, 1), ((0, 0), (0, 0), (0, page - nq)))
    vnf = jnp.pad(v_new.reshape(bz, nq * nh, vd), ((0, 0), (0, (page - nq) * nh), (0, 0)))
    qidx = (jnp.arange(r, dtype=jnp.int32) % nq).reshape(r, 1)
    expand = jnp.repeat(jnp.eye(page, dtype=BF16), nh, axis=1)
    own = (jnp.arange(page * nh)[None, :] % nh == jnp.arange(r)[:, None] // rpb).astype(BF16)
    pt = page_table.reshape(-1).astype(jnp.int32)

    def page_map(c):
        def index(b, p, pt):
            return (layer, pt[b * n_pages + jnp.minimum(p, n_steps - 1) * pp + c], 0, 0)
        return index

    grid_spec = pltpu.PrefetchScalarGridSpec(
        num_scalar_prefetch=1,
        grid=(bz, n_steps + 1),
        in_specs=([pl.BlockSpec((4, hd), lambda b, p, pt: (0, 0)),
                   pl.BlockSpec((1, vd), lambda b, p, pt: (0, 0)),
                   pl.BlockSpec((r, 1), lambda b, p, pt: (0, 0)),
                   pl.BlockSpec((None, r, qw), lambda b, p, pt: (b, 0, 0)),
                   pl.BlockSpec((page, page * nh), lambda b, p, pt: (0, 0)),
                   pl.BlockSpec((r, page * nh), lambda b, p, pt: (0, 0))]
                  + [pl.BlockSpec((None, None, qw, page), page_map(c)) for c in range(pp)]
                  + [pl.BlockSpec((None, None, page * nh, vd), page_map(c)) for c in range(pp)]
                  + [pl.BlockSpec((None, qw, page), lambda b, p, pt: (b, 0, 0)),
                     pl.BlockSpec((None, page * nh, vd), lambda b, p, pt: (b, 0, 0))]),
        out_specs=pl.BlockSpec((None, r, vd), lambda b, p, pt: (b, 0, 0)),
        scratch_shapes=[pltpu.VMEM((r, 1), F32), pltpu.VMEM((r, 1), F32), pltpu.VMEM((r, vd), F32)],
    )
    o = pl.pallas_call(
        functools.partial(_paged_kernel, n_steps=n_steps, pp=pp, rpb=rpb, lam_init=lam_init),
        grid_spec=grid_spec,
        out_shape=jax.ShapeDtypeStruct((bz, r, vd), F32),
        compiler_params=_params("parallel", "arbitrary"),
        name="sample_attention",
    )(pt, lam_vecs, subln_g.reshape(1, vd), qidx, qbd, expand, own,
      *([ckt] * pp), *([cvf] * pp), knt, vnf)
    o = o.reshape(bz, nh, 2, nq, vd)[:, :, 0]
    return o.transpose(0, 2, 1, 3).reshape(bz * nq, nh * vd).astype(BF16)


def kernel(x_prompt, x_sample, cache_k, cache_v, state_ssm_re, state_ssm_im, state_conv, page_table, norm_mix_pre, norm_mix_post, norm_ffn_pre, norm_ffn_post, w_in_even, ssm_lam_re, ssm_lam_im, ssm_log_dt, ssm_b_re, ssm_b_im, ssm_c_re, ssm_c_im, ssm_d, w_glu, conv_w, conv_b, conv_ln_g, conv_ln_b, w_out_even, w_qkv, lambda_q1, lambda_k1, lambda_q2, lambda_k2, subln_g, w_o, w_gate, w_up, w_down):
    depth = norm_mix_pre.shape[0]
    d_model = x_prompt.shape[-1]
    a_width = w_glu.shape[1]
    n_groups, n_state = ssm_lam_re.shape[1:]
    bw = conv_w.shape[2]
    page = cache_k.shape[2]
    head_dim = cache_k.shape[-1]
    q_width = cache_k.shape[-2] * head_dim
    past_len = page_table.shape[1] * page

    w_in_b, w_glu_b, w_out_b = w_in_even.astype(BF16), w_glu.astype(BF16), w_out_even.astype(BF16)
    w_qkv_b, w_o_b = w_qkv.astype(BF16), w_o.astype(BF16)
    w_gate_b, w_up_b, w_down_b = w_gate.astype(BF16), w_up.astype(BF16), w_down.astype(BF16)
    lam_vecs = jnp.stack([lambda_q1, lambda_k1, lambda_q2, lambda_k2], axis=1)

    def run_trunk(x, pos0, ssm_re0, ssm_im0, conv0, paged):
        bz, t, _ = x.shape
        h = x.reshape(bz * t, d_model)
        pos = jnp.tile(pos0 + jnp.arange(t, dtype=F32), bz).reshape(bz * t, 1)
        new_re, new_im, new_conv, new_k, new_v = [], [], [], [], []
        for i in range(depth):
            j = i // 2
            if i % 2 == 0:
                proj = norm_matmul(h, norm_mix_pre[i], w_in_b[j])
                y_a, s_re, s_im = s5_mixer(
                    proj[:, :a_width].reshape(bz, t, a_width), ssm_re0[j], ssm_im0[j],
                    ssm_lam_re[j], ssm_lam_im[j], ssm_log_dt[j], ssm_b_re[j], ssm_b_im[j],
                    ssm_c_re[j], ssm_c_im[j], ssm_d[j])
                y_a = s5_glu(y_a, w_glu_b[j])
                y_b, buf = conv_module(proj, conv0[j], conv_w[j], conv_b[j], conv_ln_g[j],
                                       conv_ln_b[j], bz, t, a_width)
                new_re.append(s_re)
                new_im.append(s_im)
                new_conv.append(buf)
                mix_in, w_mix = jnp.concatenate([y_a, y_b], axis=-1), w_out_b[j]
            else:
                lam_init = 0.8 - 0.6 * math.exp(-0.3 * i)
                q, k, v, kb, vb = qkv_rope(h, norm_mix_pre[i], w_qkv_b[j], pos, q_width, head_dim)
                if paged:
                    mix_in = sample_attention(q, k, v, cache_k, cache_v, j, page_table, lam_vecs[j],
                                              subln_g[j], bz, t, head_dim, lam_init)
                else:
                    mix_in = prompt_attention(q, kb, vb, lam_vecs[j], subln_g[j], bz, t, head_dim,
                                              lam_init)
                new_k.append(k.reshape(bz, t, q_width // head_dim, head_dim))
                new_v.append(v.reshape(bz, t, -1, 2 * head_dim))
                w_mix = w_o_b[j]
            h = matmul_norm_residual(mix_in, w_mix, norm_mix_post[i], h)
            h = ffn(h, norm_ffn_pre[i], norm_ffn_post[i], w_gate_b[i], w_up_b[i], w_down_b[i])
        return (h.reshape(bz, t, d_model), jnp.stack(new_re), jnp.stack(new_im), jnp.stack(new_conv),
                jnp.stack(new_k), jnp.stack(new_v))

    n_prompt = x_prompt.shape[0]
    n_even = state_ssm_re.shape[0]
    zero_ssm = jnp.zeros((n_even, n_prompt, n_groups, n_state), F32)
    zero_conv = jnp.zeros((n_even, n_prompt, conv_w.shape[1] - 1, bw), F32)
    y_p, re_p, im_p, conv_p, k_p, v_p = run_trunk(x_prompt, 0.0, zero_ssm, zero_ssm, zero_conv, False)
    y_s, re_s, im_s, conv_s, k_s, v_s = run_trunk(x_sample, float(past_len), state_ssm_re, state_ssm_im,
                                                  state_conv, True)
    return (y_p, y_s, re_p, im_p, conv_p, k_p, v_p, re_s, im_s, conv_s, k_s, v_s)
```

```python
import functools
import math

import jax
import jax.numpy as jnp
from jax import lax
from jax.experimental import pallas as pl
from jax.experimental.pallas import tpu as pltpu

F32 = jnp.float32
BF16 = jnp.bfloat16
EPS = 1e-6
NEG_INF = -1e30
ROPE_THETA = 500000.0
S5_CHUNK = 16
CONV_HALO = 32
LANES = 128
FLASH_ROWS = 256
PAGE_GROUP = 4
VMEM_LIMIT = 56 * 1024 * 1024
HI = lax.Precision.HIGHEST


def _tile(n, pref):
    if n <= pref:
        return n
    t = pref
    while t >= 8:
        if n % t == 0:
            return t
        t //= 2
    return n


def _params(*sem):
    return pltpu.CompilerParams(dimension_semantics=sem, vmem_limit_bytes=VMEM_LIMIT)


def _rms_rows(x, g, eps=EPS):
    ms = jnp.mean(x * x, axis=-1, keepdims=True)
    return x * lax.rsqrt(ms + eps) * g


def _norm_matmul_kernel(x_ref, g_ref, w_ref, o_ref, hn_ref):
    @pl.when(pl.program_id(1) == 0)
    def _():
        hn_ref[...] = _rms_rows(x_ref[...], g_ref[...]).astype(BF16)

    o_ref[...] = jnp.dot(hn_ref[...], w_ref[...], preferred_element_type=F32)


def norm_matmul(x, g, w):
    t, d = x.shape
    n = w.shape[1]
    tm, tn = _tile(t, 1024), _tile(n, 512)
    return pl.pallas_call(
        _norm_matmul_kernel,
        grid=(t // tm, n // tn),
        in_specs=[pl.BlockSpec((tm, d), lambda i, j: (i, 0)),
                  pl.BlockSpec((1, d), lambda i, j: (0, 0)),
                  pl.BlockSpec((d, tn), lambda i, j: (0, j))],
        out_specs=pl.BlockSpec((tm, tn), lambda i, j: (i, j)),
        out_shape=jax.ShapeDtypeStruct((t, n), F32),
        scratch_shapes=[pltpu.VMEM((tm, d), BF16)],
        compiler_params=_params("parallel", "arbitrary"),
        name="norm_matmul",
    )(x, g.reshape(1, d), w)


def _qkv_kernel(x_ref, g_ref, pos_ref, invf_ref, sel_ref, w_ref,
                q_ref, k_ref, v_ref, kb_ref, vb_ref, hn_ref, tab_ref, *, nq_tiles, scale):
    j = pl.program_id(1)

    @pl.when(j == 0)
    def _():
        hn_ref[...] = _rms_rows(x_ref[...], g_ref[...]).astype(BF16)
        ang = pos_ref[...] * invf_ref[...]
        c, s = jnp.cos(ang), jnp.sin(ang)
        tab_ref[0] = jnp.where(sel_ref[0:1, :] > 0.5, c, 1.0)
        tab_ref[1] = s * sel_ref[1:2, :]
        tab_ref[2] = s * sel_ref[2:3, :]

    y = jnp.dot(hn_ref[...], w_ref[...], preferred_element_type=F32)
    tn = y.shape[1]
    half = 128 // 16

    def rope(y):
        parts = []
        for c0 in range(0, tn, 128):
            x = y[:, c0:c0 + 128]
            parts.append(x * tab_ref[0] + pltpu.roll(x, half, axis=1) * tab_ref[1]
                         + pltpu.roll(x, 128 - half, axis=1) * tab_ref[2])
        return jnp.concatenate(parts, axis=1) if len(parts) > 1 else parts[0]

    @pl.when(j < nq_tiles)
    def _():
        q_ref[...] = (rope(y) * scale).astype(BF16)

    @pl.when(jnp.logical_and(j >= nq_tiles, j < 2 * nq_tiles))
    def _():
        r = rope(y)
        k_ref[...] = r
        kb_ref[...] = r.astype(BF16)

    @pl.when(j >= 2 * nq_tiles)
    def _():
        v_ref[...] = y
        vb_ref[...] = y.astype(BF16)


def qkv_rope(x, g, w, pos, q_width, head_dim):
    t, d = x.shape
    n = w.shape[1]
    v_width = n - 2 * q_width
    rot = head_dim // 4
    assert rot == 16 and 128 % head_dim == 0
    tm = _tile(t, 1024)
    tn = _tile(math.gcd(q_width, v_width), 512)
    nq, nv = q_width // tn, v_width // tn
    lane = jnp.arange(128) % head_dim
    inv_freq = ROPE_THETA ** (-jnp.arange(rot // 2, dtype=F32) * 2.0 / rot)
    invf = jnp.where(lane < rot, inv_freq[lane % (rot // 2)], 0.0).reshape(1, 128).astype(F32)
    sel = jnp.stack([(lane < rot).astype(F32),
                     jnp.logical_and(lane >= rot // 2, lane < rot).astype(F32),
                     -(lane < rot // 2).astype(F32)])

    def clip(j, lo, cnt):
        return jnp.clip(j - lo, 0, cnt - 1)

    outs = pl.pallas_call(
        functools.partial(_qkv_kernel, nq_tiles=nq, scale=head_dim ** -0.5),
        grid=(t // tm, 2 * nq + nv),
        in_specs=[pl.BlockSpec((tm, d), lambda i, j: (i, 0)),
                  pl.BlockSpec((1, d), lambda i, j: (0, 0)),
                  pl.BlockSpec((tm, 1), lambda i, j: (i, 0)),
                  pl.BlockSpec((1, 128), lambda i, j: (0, 0)),
                  pl.BlockSpec((3, 128), lambda i, j: (0, 0)),
                  pl.BlockSpec((d, tn), lambda i, j: (0, j))],
        out_specs=[pl.BlockSpec((tm, tn), lambda i, j: (i, clip(j, 0, nq))),
                   pl.BlockSpec((tm, tn), lambda i, j: (i, clip(j, nq, nq))),
                   pl.BlockSpec((tm, tn), lambda i, j: (i, clip(j, 2 * nq, nv))),
                   pl.BlockSpec((tm, tn), lambda i, j: (i, clip(j, nq, nq))),
                   pl.BlockSpec((tm, tn), lambda i, j: (i, clip(j, 2 * nq, nv)))],
        out_shape=[jax.ShapeDtypeStruct((t, q_width), BF16),
                   jax.ShapeDtypeStruct((t, q_width), F32),
                   jax.ShapeDtypeStruct((t, v_width), F32),
                   jax.ShapeDtypeStruct((t, q_width), BF16),
                   jax.ShapeDtypeStruct((t, v_width), BF16)],
        scratch_shapes=[pltpu.VMEM((tm, d), BF16), pltpu.VMEM((3, tm, 128), F32)],
        compiler_params=_params("parallel", "arbitrary"),
        name="qkv_rope",
    )(x, g.reshape(1, d), pos, invf, sel, w)
    return outs


def _matmul_norm_res_kernel(a_ref, w_ref, g_ref, h_ref, o_ref, acc_ref, *, nj, tn):
    j = pl.program_id(1)
    acc_ref[j] = jnp.dot(a_ref[...], w_ref[...], preferred_element_type=F32)

    @pl.when(j == nj - 1)
    def _():
        ss = jnp.zeros((acc_ref.shape[1], 1), F32)
        for jj in range(nj):
            y = acc_ref[jj]
            ss = ss + jnp.sum(y * y, axis=-1, keepdims=True)
        inv = lax.rsqrt(ss / (nj * tn) + EPS)
        for jj in range(nj):
            sl = slice(jj * tn, (jj + 1) * tn)
            o_ref[:, sl] = h_ref[:, sl] + acc_ref[jj] * inv * g_ref[:, sl]


def matmul_norm_residual(a, w, g, h):
    t, k = a.shape
    d = w.shape[1]
    tm, tn = _tile(t, 512), _tile(d, 512)
    nj = d // tn
    return pl.pallas_call(
        functools.partial(_matmul_norm_res_kernel, nj=nj, tn=tn),
        grid=(t // tm, nj),
        in_specs=[pl.BlockSpec((tm, k), lambda i, j: (i, 0)),
                  pl.BlockSpec((k, tn), lambda i, j: (0, j)),
                  pl.BlockSpec((1, d), lambda i, j: (0, 0)),
                  pl.BlockSpec((tm, d), lambda i, j: (i, 0))],
        out_specs=pl.BlockSpec((tm, d), lambda i, j: (i, 0)),
        out_shape=jax.ShapeDtypeStruct((t, d), F32),
        scratch_shapes=[pltpu.VMEM((nj, tm, tn), F32)],
        compiler_params=_params("parallel", "arbitrary"),
        name="matmul_norm_residual",
    )(a, w, g.reshape(1, d), h)


def _ffn_kernel(h_ref, gpre_ref, gpost_ref, wg_ref, wu_ref, wd_ref, o_ref, hn_ref, *, nf):
    f = pl.program_id(1)

    @pl.when(f == 0)
    def _():
        hn_ref[...] = _rms_rows(h_ref[...], gpre_ref[...]).astype(BF16)
        o_ref[...] = jnp.zeros(o_ref.shape, F32)

    hn = hn_ref[...]
    gate = jnp.dot(hn, wg_ref[...], preferred_element_type=F32)
    up = jnp.dot(hn, wu_ref[...], preferred_element_type=F32)
    act = (gate * jax.nn.sigmoid(gate) * up).astype(BF16)
    o_ref[...] += jnp.dot(act, wd_ref[...], preferred_element_type=F32)

    @pl.when(f == nf - 1)
    def _():
        o_ref[...] = h_ref[...] + _rms_rows(o_ref[...], gpost_ref[...])


def ffn(h, g_pre, g_post, wg, wu, wd):
    t, d = h.shape
    fh = wg.shape[1]
    tm = _tile(t, 1024)
    tf = next((c for c in (512, 256) if fh % c == 0), fh)
    nf = fh // tf
    return pl.pallas_call(
        functools.partial(_ffn_kernel, nf=nf),
        grid=(t // tm, nf),
        in_specs=[pl.BlockSpec((tm, d), lambda i, f: (i, 0), pipeline_mode=pl.Buffered(1)),
                  pl.BlockSpec((1, d), lambda i, f: (0, 0)),
                  pl.BlockSpec((1, d), lambda i, f: (0, 0)),
                  pl.BlockSpec((d, tf), lambda i, f: (0, f)),
                  pl.BlockSpec((d, tf), lambda i, f: (0, f)),
                  pl.BlockSpec((tf, d), lambda i, f: (f, 0))],
        out_specs=pl.BlockSpec((tm, d), lambda i, f: (i, 0)),
        out_shape=jax.ShapeDtypeStruct((t, d), F32),
        scratch_shapes=[pltpu.VMEM((tm, d), BF16)],
        compiler_params=_params("parallel", "arbitrary"),
        name="ffn",
    )(h, g_pre.reshape(1, d), g_post.reshape(1, d), wg, wu, wd)


def _glu_kernel(y_ref, w_ref, o_ref):
    y = y_ref[...]
    z = jnp.dot(y.astype(BF16), w_ref[...], preferred_element_type=F32)
    o_ref[...] = (y * jax.nn.sigmoid(z)).astype(o_ref.dtype)


def s5_glu(y, w):
    t, a = y.shape
    tm = _tile(t, 1024)
    return pl.pallas_call(
        _glu_kernel,
        grid=(t // tm,),
        in_specs=[pl.BlockSpec((tm, a), lambda i: (i, 0)),
                  pl.BlockSpec((a, a), lambda i: (0, 0))],
        out_specs=pl.BlockSpec((tm, a), lambda i: (i, 0)),
        out_shape=jax.ShapeDtypeStruct((t, a), BF16),
        compiler_params=_params("parallel"),
        name="s5_glu",
    )(y, w)


def _s5_prep_kernel(lam_row_ref, lam_col_ref, bt_ref, ct_ref, e_ref, tt_ref,
                    g0_ref, p_ref, q_ref, al_ref, *, chunk):
    lr, li, ldt = lam_row_ref[0:1, :], lam_row_ref[1:2, :], lam_row_ref[2:3, :]
    dt = jnp.exp(ldt)
    mag = jnp.exp(lr * dt)
    ab_re, ab_im = mag * jnp.cos(li * dt), mag * jnp.sin(li * dt)
    den = lr * lr + li * li
    num_re = ab_re - 1.0
    coef_re = (num_re * lr + ab_im * li) / den
    coef_im = (ab_im * lr - num_re * li) / den
    bt_re, bt_im = bt_ref[0], bt_ref[1]
    bb_re = coef_re * bt_re - coef_im * bt_im
    bb_im = coef_re * bt_im + coef_im * bt_re

    magl = jnp.exp(lr * dt * chunk)
    al_ref[0:1, :] = magl * jnp.cos(li * dt * chunk)
    al_ref[1:2, :] = magl * jnp.sin(li * dt * chunk)

    e = e_ref[...]
    pm = jnp.exp(lr * dt * e)
    pw_re, pw_im = pm * jnp.cos(li * dt * e), pm * jnp.sin(li * dt * e)
    bbt_re = jnp.concatenate([bb_re] * chunk, axis=0)
    bbt_im = jnp.concatenate([bb_im] * chunk, axis=0)
    p_ref[0] = pw_re * bbt_re - pw_im * bbt_im
    p_ref[1] = pw_re * bbt_im + pw_im * bbt_re

    lrc, lic, ldtc = lam_col_ref[0], lam_col_ref[1], lam_col_ref[2]
    dtc = jnp.exp(ldtc)
    tt = tt_ref[...]
    fm = jnp.exp(lrc * dtc * tt)
    fw_re, fw_im = fm * jnp.cos(lic * dtc * tt), fm * jnp.sin(lic * dtc * tt)
    ct_re, ct_im = ct_ref[0], ct_ref[1]
    f_re = ct_re * fw_re - ct_im * fw_im
    f_im = ct_re * fw_im + ct_im * fw_re
    g0_ref[...] = (jnp.dot(bb_re, f_re, precision=HI, preferred_element_type=F32)
                   - jnp.dot(bb_im, f_im, precision=HI, preferred_element_type=F32))
    magc = jnp.exp(lrc * dtc)
    abc_re, abc_im = magc * jnp.cos(lic * dtc), magc * jnp.sin(lic * dtc)
    q_ref[0] = (f_re * abc_re - f_im * abc_im).astype(q_ref.dtype)
    q_ref[1] = (-(f_re * abc_im + f_im * abc_re)).astype(q_ref.dtype)


def _s5_main_kernel(u_ref, w_ref, p_ref, q_ref, al_ref, h0_ref, d_ref,
                    y_ref, hn_ref, bc_re, bc_im, xs_re, xs_im, *, nct, bz):
    u = u_ref[...]
    bc_re[...] = jnp.dot(u, p_ref[0], precision=HI, preferred_element_type=F32)
    bc_im[...] = jnp.dot(u, p_ref[1], precision=HI, preferred_element_type=F32)
    a_re, a_im = al_ref[0:1, :], al_ref[1:2, :]

    def step(j, carry):
        x_re, x_im = carry
        rows = pl.ds(j * bz, bz)
        xs_re[rows, :] = x_re
        xs_im[rows, :] = x_im
        n_re = a_re * x_re - a_im * x_im + bc_re[rows, :]
        n_im = a_re * x_im + a_im * x_re + bc_im[rows, :]
        return n_re, n_im

    x_re, x_im = lax.fori_loop(0, nct, step, (h0_ref[0], h0_ref[1]))
    hn_ref[0] = x_re
    hn_ref[1] = x_im
    y = (jnp.dot(u.astype(BF16), w_ref[...], preferred_element_type=F32)
         + jnp.dot(xs_re[...].astype(BF16), q_ref[0], preferred_element_type=F32)
         + jnp.dot(xs_im[...].astype(BF16), q_ref[1], preferred_element_type=F32)
         + d_ref[...] * u)
    y_ref[...] = jax.nn.gelu(y)


def s5_mixer(u, h0_re, h0_im, lam_re, lam_im, log_dt, b_re, b_im, c_re, c_im, d_skip):
    bz, t, a = u.shape
    g, p = lam_re.shape
    gc = a // g
    chunk = S5_CHUNK if t % S5_CHUNK == 0 else t
    nct = t // chunk
    nc, lc = nct * bz, chunk * gc

    lam_row = jnp.stack([lam_re, lam_im, jnp.broadcast_to(log_dt[:, None], (g, p))], axis=1)
    lam_col = jnp.broadcast_to(lam_row[:, :, :, None], (g, 3, p, lc))
    bt = jnp.stack([b_re, b_im], axis=1).transpose(0, 1, 3, 2)
    ct = jnp.tile(jnp.stack([c_re, c_im], axis=1).transpose(0, 1, 3, 2), (1, 1, 1, chunk))

    step = jnp.arange(lc, dtype=jnp.int32) // gc
    e_col = (chunk - 1 - step).astype(F32).reshape(lc, 1)
    tt_row = step.astype(F32).reshape(1, lc)

    g0, pmat, qmat, al = pl.pallas_call(
        functools.partial(_s5_prep_kernel, chunk=chunk),
        grid=(g,),
        in_specs=[pl.BlockSpec((None, 3, p), lambda i: (i, 0, 0)),
                  pl.BlockSpec((None, 3, p, lc), lambda i: (i, 0, 0, 0)),
                  pl.BlockSpec((None, 2, gc, p), lambda i: (i, 0, 0, 0)),
                  pl.BlockSpec((None, 2, p, lc), lambda i: (i, 0, 0, 0)),
                  pl.BlockSpec((lc, 1), lambda i: (0, 0)),
                  pl.BlockSpec((1, lc), lambda i: (0, 0))],
        out_specs=[pl.BlockSpec((None, gc, lc), lambda i: (i, 0, 0)),
                   pl.BlockSpec((None, 2, lc, p), lambda i: (i, 0, 0, 0)),
                   pl.BlockSpec((None, 2, p, lc), lambda i: (i, 0, 0, 0)),
                   pl.BlockSpec((None, 2, p), lambda i: (i, 0, 0))],
        out_shape=[jax.ShapeDtypeStruct((g, gc, lc), F32),
                   jax.ShapeDtypeStruct((g, 2, lc, p), F32),
                   jax.ShapeDtypeStruct((g, 2, p, lc), BF16),
                   jax.ShapeDtypeStruct((g, 2, p), F32)],
        compiler_params=_params("parallel"),
        name="s5_prep",
    )(lam_row, lam_col, bt, ct, e_col, tt_row)

    g0p = jnp.pad(g0.reshape(g, gc, chunk, gc), ((0, 0), (0, 0), (chunk, 0), (0, 0)))
    wmat = jnp.stack([g0p[:, :, chunk - s:2 * chunk - s, :] for s in range(chunk)], axis=1)
    wmat = wmat.reshape(g, lc, lc).astype(BF16)

    ug = u.reshape(bz, nct, chunk, g, gc).transpose(3, 1, 0, 2, 4).reshape(g, nc, lc)
    h0 = jnp.stack([h0_re, h0_im], axis=0).transpose(2, 0, 1, 3)
    dt = jnp.tile(d_skip.reshape(g, 1, gc), (1, 1, chunk))

    y, hn = pl.pallas_call(
        functools.partial(_s5_main_kernel, nct=nct, bz=bz),
        grid=(g,),
        in_specs=[pl.BlockSpec((None, nc, lc), lambda i: (i, 0, 0)),
                  pl.BlockSpec((None, lc, lc), lambda i: (i, 0, 0)),
                  pl.BlockSpec((None, 2, lc, p), lambda i: (i, 0, 0, 0)),
                  pl.BlockSpec((None, 2, p, lc), lambda i: (i, 0, 0, 0)),
                  pl.BlockSpec((None, 2, p), lambda i: (i, 0, 0)),
                  pl.BlockSpec((None, 2, bz, p), lambda i: (i, 0, 0, 0)),
                  pl.BlockSpec((None, 1, lc), lambda i: (i, 0, 0))],
        out_specs=[pl.BlockSpec((None, nc, lc), lambda i: (i, 0, 0)),
                   pl.BlockSpec((None, 2, bz, p), lambda i: (i, 0, 0, 0))],
        out_shape=[jax.ShapeDtypeStruct((g, nc, lc), F32),
                   jax.ShapeDtypeStruct((g, 2, bz, p), F32)],
        scratch_shapes=[pltpu.VMEM((nc, p), F32)] * 4,
        compiler_params=_params("parallel"),
        name="s5_main",
    )(ug, wmat, pmat, qmat, al, h0, dt)

    y = y.reshape(g, nct, bz, chunk, gc).transpose(2, 1, 3, 0, 4).reshape(bz * t, a)
    hn = hn.transpose(1, 2, 0, 3)
    return y, hn[0], hn[1]


def _conv_kernel(v_ref, g_ref, buf_ref, w_ref, b_ref, lg_ref, lb_ref, y_ref, nbuf_ref, win_ref, acc_ref,
                 *, tt, width, nt):
    i = pl.program_id(1)
    keep = width - 1
    off = CONV_HALO - keep

    @pl.when(i == 0)
    def _():
        win_ref[0:off, :] = jnp.zeros((off, win_ref.shape[1]), F32)
        win_ref[off:CONV_HALO, :] = buf_ref[...]

    v = v_ref[...]
    win_ref[CONV_HALO:CONV_HALO + tt, :] = v * jax.nn.sigmoid(g_ref[...])

    bw = win_ref.shape[1]
    rb = min(tt, 64)
    for r0 in range(0, tt, rb):
        for c0 in range(0, bw, 128):
            acc = jnp.zeros((rb, 128), F32)
            for k in range(width):
                acc = acc + w_ref[k:k + 1, c0:c0 + 128] * win_ref[off + r0 + k:off + r0 + k + rb, c0:c0 + 128]
            acc_ref[r0:r0 + rb, c0:c0 + 128] = acc

    y = acc_ref[...] + b_ref[...]
    mu = jnp.mean(y, axis=-1, keepdims=True)
    var = jnp.mean(jnp.square(y - mu), axis=-1, keepdims=True)
    z = (y - mu) * lax.rsqrt(var + EPS) * lg_ref[...] + lb_ref[...]
    y_ref[...] = (z * jax.nn.sigmoid(z)).astype(y_ref.dtype)

    @pl.when(i == nt - 1)
    def _():
        nbuf_ref[...] = win_ref[tt + off:tt + CONV_HALO, :]

    tail = win_ref[tt:tt + CONV_HALO, :]
    win_ref[0:CONV_HALO, :] = tail


def conv_module(proj, buf, conv_w, conv_b, ln_g, ln_b, bz, t, a_width):
    width, bw = conv_w.shape
    assert a_width % bw == 0 and width - 1 <= CONV_HALO
    tt = _tile(t, 128)
    nt = t // tt
    vcol = a_width // bw
    proj = proj.reshape(bz, t, proj.shape[-1])
    y, nbuf = pl.pallas_call(
        functools.partial(_conv_kernel, tt=tt, width=width, nt=nt),
        grid=(bz, nt),
        in_specs=[pl.BlockSpec((None, tt, bw), lambda b, i: (b, i, vcol)),
                  pl.BlockSpec((None, tt, bw), lambda b, i: (b, i, vcol + 1)),
                  pl.BlockSpec((None, width - 1, bw), lambda b, i: (b, 0, 0)),
                  pl.BlockSpec((width, bw), lambda b, i: (0, 0)),
                  pl.BlockSpec((1, bw), lambda b, i: (0, 0)),
                  pl.BlockSpec((1, bw), lambda b, i: (0, 0)),
                  pl.BlockSpec((1, bw), lambda b, i: (0, 0))],
        out_specs=[pl.BlockSpec((None, tt, bw), lambda b, i: (b, i, 0)),
                   pl.BlockSpec((None, width - 1, bw), lambda b, i: (b, 0, 0))],
        out_shape=[jax.ShapeDtypeStruct((bz, t, bw), BF16),
                   jax.ShapeDtypeStruct((bz, width - 1, bw), F32)],
        scratch_shapes=[pltpu.VMEM((CONV_HALO + tt, bw), F32), pltpu.VMEM((tt, bw), F32)],
        compiler_params=_params("parallel", "arbitrary"),
        name="conv_module",
    )(proj, proj, buf, conv_w, conv_b.reshape(1, bw), ln_g.reshape(1, bw), ln_b.reshape(1, bw))
    return y.reshape(bz * t, bw), nbuf


def _diff_lambda(lam_ref, lam_init):
    s1 = jnp.sum(lam_ref[0:1, :] * lam_ref[1:2, :], axis=-1, keepdims=True)
    s2 = jnp.sum(lam_ref[2:3, :] * lam_ref[3:4, :], axis=-1, keepdims=True)
    return jnp.exp(s1) - jnp.exp(s2) + lam_init


def _flash_kernel(qi_ref, ki_ref, lam_ref, g_ref, q_ref, k_ref, v_ref, o_ref, qs_ref, m_ref, l_ref, acc_ref,
                  *, tq, hd, hps, lam_init):
    pair = pl.program_id(2)
    qi, ki = qi_ref[pair], ki_ref[pair]
    rc = min(tq, FLASH_ROWS)

    @pl.when(ki == 0)
    def _():
        for hh in range(hps):
            q = q_ref[:, hh * LANES:(hh + 1) * LANES]
            lane = lax.broadcasted_iota(jnp.int32, q.shape, 1)
            zero = jnp.zeros_like(q)
            qs_ref[hh, 0:tq, :] = jnp.where(lane < hd, q, zero)
            qs_ref[hh, tq:2 * tq, :] = jnp.where(lane >= hd, q, zero)
        m_ref[...] = jnp.full(m_ref.shape, NEG_INF, F32)
        l_ref[...] = jnp.zeros(l_ref.shape, F32)
        acc_ref[...] = jnp.zeros(acc_ref.shape, F32)

    def step(masked):
        items = [(hh, c) for hh in range(hps) for c in range(2 * tq // rc)]

        def scores(hh, c):
            q0 = (c * rc) % tq
            kc = q0 + rc if masked else tq
            return lax.dot_general(qs_ref[hh, c * rc:(c + 1) * rc, :],
                                   k_ref[0:kc, hh * LANES:(hh + 1) * LANES],
                                   (((1,), (1,)), ((), ())), preferred_element_type=F32)

        s_next = scores(*items[0])
        for n, (hh, c) in enumerate(items):
            rows = slice(c * rc, (c + 1) * rc)
            q0 = (c * rc) % tq
            kc = q0 + rc if masked else tq
            s = s_next
            if n + 1 < len(items):
                s_next = scores(*items[n + 1])
            if masked:
                row = lax.broadcasted_iota(jnp.int32, s.shape, 0) + q0
                col = lax.broadcasted_iota(jnp.int32, s.shape, 1)
                s = jnp.where(col <= row, s, NEG_INF)
            cols = [s[:, j:j + LANES] for j in range(0, kc, LANES)]
            m_old = m_ref[hh, rows, :]
            m_new = jnp.maximum(m_old, jnp.max(functools.reduce(jnp.maximum, cols), axis=-1, keepdims=True))
            alpha = jnp.exp(m_old - m_new)
            ps = [jnp.exp(cj - m_new) for cj in cols]
            l_ref[hh, rows, :] = alpha * l_ref[hh, rows, :] + jnp.sum(functools.reduce(jnp.add, ps), axis=-1,
                                                                       keepdims=True)
            p = jnp.concatenate(ps, axis=1) if len(ps) > 1 else ps[0]
            pv = jnp.dot(p.astype(BF16), v_ref[0:kc, hh * LANES:(hh + 1) * LANES],
                         preferred_element_type=F32)
            acc_ref[hh, rows, :] = alpha * acc_ref[hh, rows, :] + pv
            m_ref[hh, rows, :] = m_new

    @pl.when(ki < qi)
    def _():
        step(False)

    @pl.when(ki == qi)
    def _():
        step(True)
        lam = _diff_lambda(lam_ref, lam_init)
        for hh in range(hps):
            o = acc_ref[hh] / l_ref[hh]
            d = o[0:tq] - lam * o[tq:2 * tq]
            o_ref[:, hh * LANES:(hh + 1) * LANES] = (_rms_rows(d, g_ref[...])
                                                     * (1.0 - lam_init)).astype(o_ref.dtype)


def prompt_attention(q, k, v, lam_vecs, subln_g, bz, s, hd, lam_init):
    t, qw = q.shape
    vd = 2 * hd
    nh = qw // vd
    tq = _tile(s, 512)
    nq = s // tq
    assert vd == LANES and tq % min(tq, FLASH_ROWS) == 0 and min(tq, FLASH_ROWS) % LANES == 0
    hps = 2 if nh % 2 == 0 else 1
    bw = hps * vd
    pairs = [(a, c) for a in range(nq) for c in range(a + 1)]
    qi_tab = jnp.asarray([a for a, _ in pairs], jnp.int32)
    ki_tab = jnp.asarray([c for _, c in pairs], jnp.int32)
    grid_spec = pltpu.PrefetchScalarGridSpec(
        num_scalar_prefetch=2,
        grid=(bz, nh // hps, len(pairs)),
        in_specs=[pl.BlockSpec((4, hd), lambda b, h, p, qt, kt: (0, 0)),
                  pl.BlockSpec((1, vd), lambda b, h, p, qt, kt: (0, 0)),
                  pl.BlockSpec((tq, bw), lambda b, h, p, qt, kt: (b * nq + qt[p], h)),
                  pl.BlockSpec((tq, bw), lambda b, h, p, qt, kt: (b * nq + kt[p], h)),
                  pl.BlockSpec((tq, bw), lambda b, h, p, qt, kt: (b * nq + kt[p], h))],
        out_specs=pl.BlockSpec((tq, bw), lambda b, h, p, qt, kt: (b * nq + qt[p], h)),
        scratch_shapes=[pltpu.VMEM((hps, 2 * tq, vd), BF16), pltpu.VMEM((hps, 2 * tq, LANES), F32),
                        pltpu.VMEM((hps, 2 * tq, LANES), F32), pltpu.VMEM((hps, 2 * tq, vd), F32)],
    )
    return pl.pallas_call(
        functools.partial(_flash_kernel, tq=tq, hd=hd, hps=hps, lam_init=lam_init),
        grid_spec=grid_spec,
        out_shape=jax.ShapeDtypeStruct((t, nh * vd), BF16),
        compiler_params=_params("parallel", "parallel", "arbitrary"),
        name="prompt_attention",
    )(qi_tab, ki_tab, lam_vecs, subln_g.reshape(1, vd), q, k, v)


def _paged_kernel(pt_ref, lam_ref, g_ref, qidx_ref, q_ref, e_ref, msk_ref, *refs,
                  n_steps, pp, rpb, lam_init):
    k_refs, v_refs = refs[:pp], refs[pp:2 * pp]
    kn_ref, vn_ref, o_ref, m_ref, l_ref, acc_ref = refs[2 * pp:]
    p = pl.program_id(1)

    @pl.when(p == 0)
    def _():
        m_ref[...] = jnp.full(m_ref.shape, NEG_INF, F32)
        l_ref[...] = jnp.zeros(l_ref.shape, F32)
        acc_ref[...] = jnp.zeros(acc_ref.shape, F32)

    def update(kts, vfs, masked):
        q = q_ref[...]
        ss = [jnp.dot(q, kt[...].astype(BF16), preferred_element_type=F32) for kt in kts]
        if masked:
            col = lax.broadcasted_iota(jnp.int32, ss[0].shape, 1)
            ss = [jnp.where(col <= qidx_ref[...], s, NEG_INF) for s in ss]
        m_old = m_ref[...]
        m_new = jnp.maximum(m_old, jnp.max(functools.reduce(jnp.maximum, ss), axis=-1, keepdims=True))
        alpha = jnp.exp(m_old - m_new)
        prs = [jnp.exp(s - m_new) for s in ss]
        l_ref[...] = alpha * l_ref[...] + jnp.sum(functools.reduce(jnp.add, prs), axis=-1, keepdims=True)
        pv = None
        for pr, vf in zip(prs, vfs):
            pe = jnp.dot(pr.astype(BF16), e_ref[...], preferred_element_type=F32)
            pe = pe.astype(BF16) * msk_ref[...]
            part = jnp.dot(pe, vf[...].astype(BF16), preferred_element_type=F32)
            pv = part if pv is None else pv + part
        acc_ref[...] = alpha * acc_ref[...] + pv
        m_ref[...] = m_new

    @pl.when(p < n_steps)
    def _():
        for c0 in range(0, pp, PAGE_GROUP):
            update(k_refs[c0:c0 + PAGE_GROUP], v_refs[c0:c0 + PAGE_GROUP], False)

    @pl.when(p == n_steps)
    def _():
        update([kn_ref], [vn_ref], True)
        o = acc_ref[...] / l_ref[...]
        lam = _diff_lambda(lam_ref, lam_init)
        r = o.shape[0]
        d = o - lam * pltpu.roll(o, r - rpb // 2, axis=0)
        o_ref[...] = _rms_rows(d, g_ref[...]) * (1.0 - lam_init)


def sample_attention(q, k_new, v_new, cache_k, cache_v, layer, page_table, lam_vecs, subln_g,
                     bz, nq, hd, lam_init):
    vd = 2 * hd
    qw = q.shape[1]
    nh = qw // vd
    n_odd, n_pool, page = cache_k.shape[:3]
    n_pages = page_table.shape[1]
    assert nq <= page
    pp = next(c for c in (8, 4, 2, 1) if n_pages % c == 0 and n_pages // c >= min(2, n_pages))
    n_steps = n_pages // pp
    rpb = 2 * nq
    r = nh * rpb
    ckt = cache_k.transpose(0, 1, 3, 4, 2).reshape(n_odd, n_pool, qw, page)
    cvf = cache_v.reshape(n_odd, n_pool, page * nh, vd)
    q4 = q.reshape(bz, nq, nh, 2, hd).transpose(0, 2, 3, 1, 4)
    eye = jnp.eye(2 * nh, dtype=q.dtype).reshape(nh, 2, 1, 2 * nh, 1)
    qbd = (q4.reshape(bz, nh, 2, nq, 1, hd) * eye[None]).reshape(bz, r, qw)
    knt = jnp.pad(k_new.reshape(bz, nq, qw).transpose(0, 2, 1), ((0, 0), (0, 0), (0, page - nq)))
    vnf = jnp.pad(v_new.reshape(bz, nq * nh, vd), ((0, 0), (0, (page - nq) * nh), (0, 0)))
    qidx = (jnp.arange(r, dtype=jnp.int32) % nq).reshape(r, 1)
    expand = jnp.repeat(jnp.eye(page, dtype=BF16), nh, axis=1)
    own = (jnp.arange(page * nh)[None, :] % nh == jnp.arange(r)[:, None] // rpb).astype(BF16)
    pt = page_table.reshape(-1).astype(jnp.int32)

    def page_map(c):
        def index(b, p, pt):
            return (layer, pt[b * n_pages + jnp.minimum(p, n_steps - 1) * pp + c], 0, 0)
        return index

    grid_spec = pltpu.PrefetchScalarGridSpec(
        num_scalar_prefetch=1,
        grid=(bz, n_steps + 1),
        in_specs=([pl.BlockSpec((4, hd), lambda b, p, pt: (0, 0)),
                   pl.BlockSpec((1, vd), lambda b, p, pt: (0, 0)),
                   pl.BlockSpec((r, 1), lambda b, p, pt: (0, 0)),
                   pl.BlockSpec((None, r, qw), lambda b, p, pt: (b, 0, 0)),
                   pl.BlockSpec((page, page * nh), lambda b, p, pt: (0, 0)),
                   pl.BlockSpec((r, page * nh), lambda b, p, pt: (0, 0))]
                  + [pl.BlockSpec((None, None, qw, page), page_map(c)) for c in range(pp)]
                  + [pl.BlockSpec((None, None, page * nh, vd), page_map(c)) for c in range(pp)]
                  + [pl.BlockSpec((None, qw, page), lambda b, p, pt: (b, 0, 0)),
                     pl.BlockSpec((None, page * nh, vd), lambda b, p, pt: (b, 0, 0))]),
        out_specs=pl.BlockSpec((None, r, vd), lambda b, p, pt: (b, 0, 0)),
        scratch_shapes=[pltpu.VMEM((r, 1), F32), pltpu.VMEM((r, 1), F32), pltpu.VMEM((r, vd), F32)],
    )
    o = pl.pallas_call(
        functools.partial(_paged_kernel, n_steps=n_steps, pp=pp, rpb=rpb, lam_init=lam_init),
        grid_spec=grid_spec,
        out_shape=jax.ShapeDtypeStruct((bz, r, vd), F32),
        compiler_params=_params("parallel", "arbitrary"),
        name="sample_attention",
    )(pt, lam_vecs, subln_g.reshape(1, vd), qidx, qbd, expand, own,
      *([ckt] * pp), *([cvf] * pp), knt, vnf)
    o = o.reshape(bz, nh, 2, nq, vd)[:, :, 0]
    return o.transpose(0, 2, 1, 3).reshape(bz * nq, nh * vd).astype(BF16)


def kernel(x_prompt, x_sample, cache_k, cache_v, state_ssm_re, state_ssm_im, state_conv, page_table, norm_mix_pre, norm_mix_post, norm_ffn_pre, norm_ffn_post, w_in_even, ssm_lam_re, ssm_lam_im, ssm_log_dt, ssm_b_re, ssm_b_im, ssm_c_re, ssm_c_im, ssm_d, w_glu, conv_w, conv_b, conv_ln_g, conv_ln_b, w_out_even, w_qkv, lambda_q1, lambda_k1, lambda_q2, lambda_k2, subln_g, w_o, w_gate, w_up, w_down):
    depth = norm_mix_pre.shape[0]
    d_model = x_prompt.shape[-1]
    a_width = w_glu.shape[1]
    n_groups, n_state = ssm_lam_re.shape[1:]
    bw = conv_w.shape[2]
    page = cache_k.shape[2]
    head_dim = cache_k.shape[-1]
    q_width = cache_k.shape[-2] * head_dim
    past_len = page_table.shape[1] * page

    w_in_b, w_glu_b, w_out_b = w_in_even.astype(BF16), w_glu.astype(BF16), w_out_even.astype(BF16)
    w_qkv_b, w_o_b = w_qkv.astype(BF16), w_o.astype(BF16)
    w_gate_b, w_up_b, w_down_b = w_gate.astype(BF16), w_up.astype(BF16), w_down.astype(BF16)
    lam_vecs = jnp.stack([lambda_q1, lambda_k1, lambda_q2, lambda_k2], axis=1)

    def run_trunk(x, pos0, ssm_re0, ssm_im0, conv0, paged):
        bz, t, _ = x.shape
        h = x.reshape(bz * t, d_model)
        pos = jnp.tile(pos0 + jnp.arange(t, dtype=F32), bz).reshape(bz * t, 1)
        new_re, new_im, new_conv, new_k, new_v = [], [], [], [], []
        for i in range(depth):
            j = i // 2
            if i % 2 == 0:
                proj = norm_matmul(h, norm_mix_pre[i], w_in_b[j])
                y_a, s_re, s_im = s5_mixer(
                    proj[:, :a_width].reshape(bz, t, a_width), ssm_re0[j], ssm_im0[j],
                    ssm_lam_re[j], ssm_lam_im[j], ssm_log_dt[j], ssm_b_re[j], ssm_b_im[j],
                    ssm_c_re[j], ssm_c_im[j], ssm_d[j])
                y_a = s5_glu(y_a, w_glu_b[j])
                y_b, buf = conv_module(proj, conv0[j], conv_w[j], conv_b[j], conv_ln_g[j],
                                       conv_ln_b[j], bz, t, a_width)
                new_re.append(s_re)
                new_im.append(s_im)
                new_conv.append(buf)
                mix_in, w_mix = jnp.concatenate([y_a, y_b], axis=-1), w_out_b[j]
            else:
                lam_init = 0.8 - 0.6 * math.exp(-0.3 * i)
                q, k, v, kb, vb = qkv_rope(h, norm_mix_pre[i], w_qkv_b[j], pos, q_width, head_dim)
                if paged:
                    mix_in = sample_attention(q, k, v, cache_k, cache_v, j, page_table, lam_vecs[j],
                                              subln_g[j], bz, t, head_dim, lam_init)
                else:
                    mix_in = prompt_attention(q, kb, vb, lam_vecs[j], subln_g[j], bz, t, head_dim,
                                              lam_init)
                new_k.append(k.reshape(bz, t, q_width // head_dim, head_dim))
                new_v.append(v.reshape(bz, t, -1, 2 * head_dim))
                w_mix = w_o_b[j]
            h = matmul_norm_residual(mix_in, w_mix, norm_mix_post[i], h)
            h = ffn(h, norm_ffn_pre[i], norm_ffn_post[i], w_gate_b[i], w_up_b[i], w_down_b[i])
        return (h.reshape(bz, t, d_model), jnp.stack(new_re), jnp.stack(new_im), jnp.stack(new_conv),
                jnp.stack(new_k), jnp.stack(new_v))

    n_prompt = x_prompt.shape[0]
    n_even = state_ssm_re.shape[0]
    zero_ssm = jnp.zeros((n_even, n_prompt, n_groups, n_state), F32)
    zero_conv = jnp.zeros((n_even, n_prompt, conv_w.shape[1] - 1, bw), F32)
    y_p, re_p, im_p, conv_p, k_p, v_p = run_trunk(x_prompt, 0.0, zero_ssm, zero_ssm, zero_conv, False)
    y_s, re_s, im_s, conv_s, k_s, v_s = run_trunk(x_sample, float(past_len), state_ssm_re, state_ssm_im,
                                                  state_conv, True)
    return (y_p, y_s, re_p, im_p, conv_p, k_p, v_p, re_s, im_s, conv_s, k_s, v_s)
```

```python
import functools
import math

import jax
import jax.numpy as jnp
from jax import lax
from jax.experimental import pallas as pl
from jax.experimental.pallas import tpu as pltpu

F32 = jnp.float32
BF16 = jnp.bfloat16
EPS = 1e-6
NEG_INF = -1e30
ROPE_THETA = 500000.0
S5_CHUNK = 16
S5_ROWS = 256
CONV_HALO = 32
LANES = 128
FLASH_ROWS = 256
PAGE_GROUP = 4
VMEM_LIMIT = 56 * 1024 * 1024
HI = lax.Precision.HIGHEST


def _tile(n, pref):
    if n <= pref:
        return n
    t = pref
    while t >= 8:
        if n % t == 0:
            return t
        t //= 2
    return n


def _params(*sem):
    return pltpu.CompilerParams(dimension_semantics=sem, vmem_limit_bytes=VMEM_LIMIT)


def _rms_rows(x, g, eps=EPS):
    ms = jnp.mean(x * x, axis=-1, keepdims=True)
    return x * lax.rsqrt(ms + eps) * g


def _norm_matmul_kernel(x_ref, g_ref, w_ref, o_ref, hn_ref):
    @pl.when(pl.program_id(1) == 0)
    def _():
        hn_ref[...] = _rms_rows(x_ref[...], g_ref[...]).astype(BF16)

    o_ref[...] = jnp.dot(hn_ref[...], w_ref[...], preferred_element_type=F32)


def norm_matmul(x, g, w):
    t, d = x.shape
    n = w.shape[1]
    tm, tn = _tile(t, 1024), _tile(n, 512)
    return pl.pallas_call(
        _norm_matmul_kernel,
        grid=(t // tm, n // tn),
        in_specs=[pl.BlockSpec((tm, d), lambda i, j: (i, 0)),
                  pl.BlockSpec((1, d), lambda i, j: (0, 0)),
                  pl.BlockSpec((d, tn), lambda i, j: (0, j))],
        out_specs=pl.BlockSpec((tm, tn), lambda i, j: (i, j)),
        out_shape=jax.ShapeDtypeStruct((t, n), F32),
        scratch_shapes=[pltpu.VMEM((tm, d), BF16)],
        compiler_params=_params("parallel", "arbitrary"),
        name="norm_matmul",
    )(x, g.reshape(1, d), w)


def _qkv_kernel(x_ref, g_ref, pos_ref, invf_ref, sel_ref, w_ref,
                q_ref, k_ref, v_ref, kb_ref, vb_ref, hn_ref, tab_ref, *, nq_tiles, scale):
    j = pl.program_id(1)

    @pl.when(j == 0)
    def _():
        hn_ref[...] = _rms_rows(x_ref[...], g_ref[...]).astype(BF16)
        ang = pos_ref[...] * invf_ref[...]
        c, s = jnp.cos(ang), jnp.sin(ang)
        tab_ref[0] = jnp.where(sel_ref[0:1, :] > 0.5, c, 1.0)
        tab_ref[1] = s * sel_ref[1:2, :]
        tab_ref[2] = s * sel_ref[2:3, :]

    tn = w_ref.shape[1]
    cw = min(tn, 2 * LANES)
    half = LANES // 16

    def rope(y):
        parts = []
        for c0 in range(0, y.shape[1], LANES):
            x = y[:, c0:c0 + LANES]
            parts.append(x * tab_ref[0] + pltpu.roll(x, half, axis=1) * tab_ref[1]
                         + pltpu.roll(x, LANES - half, axis=1) * tab_ref[2])
        return jnp.concatenate(parts, axis=1) if len(parts) > 1 else parts[0]

    def sweep(emit):
        hn = hn_ref[...]
        chunks = list(range(0, tn, cw))
        y_next = jnp.dot(hn, w_ref[:, 0:cw], preferred_element_type=F32)
        for n, c0 in enumerate(chunks):
            y = y_next
            if n + 1 < len(chunks):
                y_next = jnp.dot(hn, w_ref[:, c0 + cw:c0 + 2 * cw], preferred_element_type=F32)
            emit(slice(c0, c0 + cw), y)

    @pl.when(j < nq_tiles)
    def _():
        def emit(cols, y):
            q_ref[:, cols] = (rope(y) * scale).astype(BF16)
        sweep(emit)

    @pl.when(jnp.logical_and(j >= nq_tiles, j < 2 * nq_tiles))
    def _():
        def emit(cols, y):
            r = rope(y)
            k_ref[:, cols] = r
            kb_ref[:, cols] = r.astype(BF16)
        sweep(emit)

    @pl.when(j >= 2 * nq_tiles)
    def _():
        def emit(cols, y):
            v_ref[:, cols] = y
            vb_ref[:, cols] = y.astype(BF16)
        sweep(emit)


def qkv_rope(x, g, w, pos, q_width, head_dim):
    t, d = x.shape
    n = w.shape[1]
    v_width = n - 2 * q_width
    rot = head_dim // 4
    assert rot == 16 and 128 % head_dim == 0
    tm = _tile(t, 512)
    tn = _tile(math.gcd(q_width, v_width), 1024)
    nq, nv = q_width // tn, v_width // tn
    lane = jnp.arange(128) % head_dim
    inv_freq = ROPE_THETA ** (-jnp.arange(rot // 2, dtype=F32) * 2.0 / rot)
    invf = jnp.where(lane < rot, inv_freq[lane % (rot // 2)], 0.0).reshape(1, 128).astype(F32)
    sel = jnp.stack([(lane < rot).astype(F32),
                     jnp.logical_and(lane >= rot // 2, lane < rot).astype(F32),
                     -(lane < rot // 2).astype(F32)])

    def clip(j, lo, cnt):
        return jnp.clip(j - lo, 0, cnt - 1)

    outs = pl.pallas_call(
        functools.partial(_qkv_kernel, nq_tiles=nq, scale=head_dim ** -0.5),
        grid=(t // tm, 2 * nq + nv),
        in_specs=[pl.BlockSpec((tm, d), lambda i, j: (i, 0)),
                  pl.BlockSpec((1, d), lambda i, j: (0, 0)),
                  pl.BlockSpec((tm, 1), lambda i, j: (i, 0)),
                  pl.BlockSpec((1, 128), lambda i, j: (0, 0)),
                  pl.BlockSpec((3, 128), lambda i, j: (0, 0)),
                  pl.BlockSpec((d, tn), lambda i, j: (0, j))],
        out_specs=[pl.BlockSpec((tm, tn), lambda i, j: (i, clip(j, 0, nq))),
                   pl.BlockSpec((tm, tn), lambda i, j: (i, clip(j, nq, nq))),
                   pl.BlockSpec((tm, tn), lambda i, j: (i, clip(j, 2 * nq, nv))),
                   pl.BlockSpec((tm, tn), lambda i, j: (i, clip(j, nq, nq))),
                   pl.BlockSpec((tm, tn), lambda i, j: (i, clip(j, 2 * nq, nv)))],
        out_shape=[jax.ShapeDtypeStruct((t, q_width), BF16),
                   jax.ShapeDtypeStruct((t, q_width), F32),
                   jax.ShapeDtypeStruct((t, v_width), F32),
                   jax.ShapeDtypeStruct((t, q_width), BF16),
                   jax.ShapeDtypeStruct((t, v_width), BF16)],
        scratch_shapes=[pltpu.VMEM((tm, d), BF16), pltpu.VMEM((3, tm, 128), F32)],
        compiler_params=_params("parallel", "arbitrary"),
        name="qkv_rope",
    )(x, g.reshape(1, d), pos, invf, sel, w)
    return outs


def _matmul_norm_res_kernel(a_ref, w_ref, g_ref, h_ref, o_ref, acc_ref, *, nj, tn):
    j = pl.program_id(1)
    acc_ref[j] = jnp.dot(a_ref[...], w_ref[...], preferred_element_type=F32)

    @pl.when(j == nj - 1)
    def _():
        ss = jnp.zeros((acc_ref.shape[1], 1), F32)
        for jj in range(nj):
            y = acc_ref[jj]
            ss = ss + jnp.sum(y * y, axis=-1, keepdims=True)
        inv = lax.rsqrt(ss / (nj * tn) + EPS)
        for jj in range(nj):
            sl = slice(jj * tn, (jj + 1) * tn)
            o_ref[:, sl] = h_ref[:, sl] + acc_ref[jj] * inv * g_ref[:, sl]


def matmul_norm_residual(a, w, g, h):
    t, k = a.shape
    d = w.shape[1]
    tm, tn = _tile(t, 512), _tile(d, 512)
    nj = d // tn
    return pl.pallas_call(
        functools.partial(_matmul_norm_res_kernel, nj=nj, tn=tn),
        grid=(t // tm, nj),
        in_specs=[pl.BlockSpec((tm, k), lambda i, j: (i, 0)),
                  pl.BlockSpec((k, tn), lambda i, j: (0, j)),
                  pl.BlockSpec((1, d), lambda i, j: (0, 0)),
                  pl.BlockSpec((tm, d), lambda i, j: (i, 0))],
        out_specs=pl.BlockSpec((tm, d), lambda i, j: (i, 0)),
        out_shape=jax.ShapeDtypeStruct((t, d), F32),
        scratch_shapes=[pltpu.VMEM((nj, tm, tn), F32)],
        compiler_params=_params("parallel", "arbitrary"),
        name="matmul_norm_residual",
    )(a, w, g.reshape(1, d), h)


def _ffn_kernel(h_ref, gpre_ref, gpost_ref, wg_ref, wu_ref, wd_ref, o_ref, hn_ref, *, nf):
    f = pl.program_id(1)

    @pl.when(f == 0)
    def _():
        hn_ref[...] = _rms_rows(h_ref[...], gpre_ref[...]).astype(BF16)
        o_ref[...] = jnp.zeros(o_ref.shape, F32)

    hn = hn_ref[...]
    gate = jnp.dot(hn, wg_ref[...], preferred_element_type=F32)
    up = jnp.dot(hn, wu_ref[...], preferred_element_type=F32)
    act = (gate * jax.nn.sigmoid(gate) * up).astype(BF16)
    o_ref[...] += jnp.dot(act, wd_ref[...], preferred_element_type=F32)

    @pl.when(f == nf - 1)
    def _():
        o_ref[...] = h_ref[...] + _rms_rows(o_ref[...], gpost_ref[...])


def ffn(h, g_pre, g_post, wg, wu, wd):
    t, d = h.shape
    fh = wg.shape[1]
    tm = _tile(t, 1024)
    tf = next((c for c in (512, 256) if fh % c == 0), fh)
    nf = fh // tf
    return pl.pallas_call(
        functools.partial(_ffn_kernel, nf=nf),
        grid=(t // tm, nf),
        in_specs=[pl.BlockSpec((tm, d), lambda i, f: (i, 0), pipeline_mode=pl.Buffered(1)),
                  pl.BlockSpec((1, d), lambda i, f: (0, 0)),
                  pl.BlockSpec((1, d), lambda i, f: (0, 0)),
                  pl.BlockSpec((d, tf), lambda i, f: (0, f)),
                  pl.BlockSpec((d, tf), lambda i, f: (0, f)),
                  pl.BlockSpec((tf, d), lambda i, f: (f, 0))],
        out_specs=pl.BlockSpec((tm, d), lambda i, f: (i, 0)),
        out_shape=jax.ShapeDtypeStruct((t, d), F32),
        scratch_shapes=[pltpu.VMEM((tm, d), BF16)],
        compiler_params=_params("parallel", "arbitrary"),
        name="ffn",
    )(h, g_pre.reshape(1, d), g_post.reshape(1, d), wg, wu, wd)


def _glu_kernel(y_ref, w_ref, o_ref):
    y = y_ref[...]
    z = jnp.dot(y.astype(BF16), w_ref[...], preferred_element_type=F32)
    o_ref[...] = (y * jax.nn.sigmoid(z)).astype(o_ref.dtype)


def s5_glu(y, w):
    t, a = y.shape
    tm = _tile(t, 1024)
    return pl.pallas_call(
        _glu_kernel,
        grid=(t // tm,),
        in_specs=[pl.BlockSpec((tm, a), lambda i: (i, 0)),
                  pl.BlockSpec((a, a), lambda i: (0, 0))],
        out_specs=pl.BlockSpec((tm, a), lambda i: (i, 0)),
        out_shape=jax.ShapeDtypeStruct((t, a), BF16),
        compiler_params=_params("parallel"),
        name="s5_glu",
    )(y, w)


def _s5_prep_kernel(lam_row_ref, lam_col_ref, bt_ref, ct_ref, e_ref, tt_ref,
                    g0_ref, p_ref, q_ref, al_ref, *, chunk):
    lr, li, ldt = lam_row_ref[0:1, :], lam_row_ref[1:2, :], lam_row_ref[2:3, :]
    dt = jnp.exp(ldt)
    mag = jnp.exp(lr * dt)
    ab_re, ab_im = mag * jnp.cos(li * dt), mag * jnp.sin(li * dt)
    den = lr * lr + li * li
    num_re = ab_re - 1.0
    coef_re = (num_re * lr + ab_im * li) / den
    coef_im = (ab_im * lr - num_re * li) / den
    bt_re, bt_im = bt_ref[0], bt_ref[1]
    bb_re = coef_re * bt_re - coef_im * bt_im
    bb_im = coef_re * bt_im + coef_im * bt_re

    magl = jnp.exp(lr * dt * chunk)
    al_ref[0:1, :] = magl * jnp.cos(li * dt * chunk)
    al_ref[1:2, :] = magl * jnp.sin(li * dt * chunk)

    e = e_ref[...]
    pm = jnp.exp(lr * dt * e)
    pw_re, pw_im = pm * jnp.cos(li * dt * e), pm * jnp.sin(li * dt * e)
    bbt_re = jnp.concatenate([bb_re] * chunk, axis=0)
    bbt_im = jnp.concatenate([bb_im] * chunk, axis=0)
    p_ref[0] = pw_re * bbt_re - pw_im * bbt_im
    p_ref[1] = pw_re * bbt_im + pw_im * bbt_re

    lrc, lic, ldtc = lam_col_ref[0], lam_col_ref[1], lam_col_ref[2]
    dtc = jnp.exp(ldtc)
    tt = tt_ref[...]
    fm = jnp.exp(lrc * dtc * tt)
    fw_re, fw_im = fm * jnp.cos(lic * dtc * tt), fm * jnp.sin(lic * dtc * tt)
    ct_re, ct_im = ct_ref[0], ct_ref[1]
    f_re = ct_re * fw_re - ct_im * fw_im
    f_im = ct_re * fw_im + ct_im * fw_re
    g0_ref[...] = (jnp.dot(bb_re, f_re, precision=HI, preferred_element_type=F32)
                   - jnp.dot(bb_im, f_im, precision=HI, preferred_element_type=F32))
    magc = jnp.exp(lrc * dtc)
    abc_re, abc_im = magc * jnp.cos(lic * dtc), magc * jnp.sin(lic * dtc)
    q_ref[0] = (f_re * abc_re - f_im * abc_im).astype(q_ref.dtype)
    q_ref[1] = (-(f_re * abc_im + f_im * abc_re)).astype(q_ref.dtype)


def _s5_main_kernel(u_ref, w_ref, q_ref, al_ref, h0_ref, d_ref,
                    y_ref, hn_ref, bc_ref, xs_ref, *, chunk, nct, bpt):
    rows = bc_ref.shape[0]
    nw = chunk * LANES
    half = bc_ref.shape[1] // 2
    us = [u_ref[pl.ds(s, rows, stride=chunk), :] for s in range(chunk)]
    ucat = jnp.concatenate(us, axis=1).astype(BF16)
    r = jnp.dot(ucat, w_ref[...], preferred_element_type=F32)
    bc_ref[...] = r[:, nw:]
    a_re, a_im = al_ref[:, 0:half], al_ref[:, half:]

    def step(j, carry):
        nxt = []
        for bl in range(bpt):
            x_re, x_im = carry[2 * bl], carry[2 * bl + 1]
            row = pl.ds(bl * nct + j, 1)
            xs_ref[row, 0:half] = x_re
            xs_ref[row, half:] = x_im
            nxt.append(a_re * x_re - a_im * x_im + bc_ref[row, 0:half])
            nxt.append(a_re * x_im + a_im * x_re + bc_ref[row, half:])
        return tuple(nxt)

    init = tuple(h0_ref[bl:bl + 1, sl] for bl in range(bpt) for sl in (slice(0, half), slice(half, 2 * half)))
    fin = lax.fori_loop(0, nct, step, init)
    for bl in range(bpt):
        hn_ref[bl:bl + 1, 0:half] = fin[2 * bl]
        hn_ref[bl:bl + 1, half:] = fin[2 * bl + 1]
    yc = jnp.dot(xs_ref[...].astype(BF16), q_ref[...], preferred_element_type=F32)
    for t in range(chunk):
        sl = slice(t * LANES, (t + 1) * LANES)
        y_ref[pl.ds(t, rows, stride=chunk), :] = jax.nn.gelu(r[:, sl] + yc[:, sl] + d_ref[...] * us[t])


def s5_mixer(proj, bz, t, a, h0_re, h0_im, lam_re, lam_im, log_dt, b_re, b_im, c_re, c_im, d_skip):
    g, p = lam_re.shape
    gc = a // g
    chunk = S5_CHUNK if t % S5_CHUNK == 0 else t
    nct = t // chunk
    lc = chunk * gc

    lam_row = jnp.stack([lam_re, lam_im, jnp.broadcast_to(log_dt[:, None], (g, p))], axis=1)
    lam_col = jnp.broadcast_to(lam_row[:, :, :, None], (g, 3, p, lc))
    bt = jnp.stack([b_re, b_im], axis=1).transpose(0, 1, 3, 2)
    ct = jnp.tile(jnp.stack([c_re, c_im], axis=1).transpose(0, 1, 3, 2), (1, 1, 1, chunk))

    step = jnp.arange(lc, dtype=jnp.int32) // gc
    e_col = (chunk - 1 - step).astype(F32).reshape(lc, 1)
    tt_row = step.astype(F32).reshape(1, lc)

    g0, pmat, qmat, al = pl.pallas_call(
        functools.partial(_s5_prep_kernel, chunk=chunk),
        grid=(g,),
        in_specs=[pl.BlockSpec((None, 3, p), lambda i: (i, 0, 0)),
                  pl.BlockSpec((None, 3, p, lc), lambda i: (i, 0, 0, 0)),
                  pl.BlockSpec((None, 2, gc, p), lambda i: (i, 0, 0, 0)),
                  pl.BlockSpec((None, 2, p, lc), lambda i: (i, 0, 0, 0)),
                  pl.BlockSpec((lc, 1), lambda i: (0, 0)),
                  pl.BlockSpec((1, lc), lambda i: (0, 0))],
        out_specs=[pl.BlockSpec((None, gc, lc), lambda i: (i, 0, 0)),
                   pl.BlockSpec((None, 2, lc, p), lambda i: (i, 0, 0, 0)),
                   pl.BlockSpec((None, 2, p, lc), lambda i: (i, 0, 0, 0)),
                   pl.BlockSpec((None, 2, p), lambda i: (i, 0, 0))],
        out_shape=[jax.ShapeDtypeStruct((g, gc, lc), F32),
                   jax.ShapeDtypeStruct((g, 2, lc, p), F32),
                   jax.ShapeDtypeStruct((g, 2, p, lc), BF16),
                   jax.ShapeDtypeStruct((g, 2, p), F32)],
        compiler_params=_params("parallel"),
        name="s5_prep",
    )(lam_row, lam_col, bt, ct, e_col, tt_row)

    g0p = jnp.pad(g0.reshape(g, gc, chunk, gc), ((0, 0), (0, 0), (chunk, 0), (0, 0)))
    wt = jnp.stack([g0p[:, :, chunk - s:2 * chunk - s, :] for s in range(chunk)], axis=1)

    assert LANES % gc == 0 and g % (LANES // gc) == 0
    gb = LANES // gc
    nb = g // gb
    eye = jnp.eye(gb, dtype=F32)
    nw, ns = chunk * LANES, 2 * gb * p
    wbig = jnp.einsum('bgsctd,gh->bsgcthd', wt.reshape(nb, gb, chunk, gc, chunk, gc), eye).reshape(nb, nw, nw)
    pbig = jnp.einsum('bgrscn,gh->bsgcrhn', pmat.reshape(nb, gb, 2, chunk, gc, p), eye).reshape(nb, nw, ns)
    wp = jnp.concatenate([wbig, pbig], axis=2).astype(BF16)
    qbig = jnp.einsum('bgrntd,gh->brgnthd', qmat.astype(F32).reshape(nb, gb, 2, p, chunk, gc), eye)
    qbig = qbig.reshape(nb, ns, nw).astype(BF16)
    al_big = al.reshape(nb, gb, 2, p).transpose(0, 2, 1, 3).reshape(nb, 1, ns)

    bpt = max(1, min(bz, S5_ROWS // nct))
    assert bz % bpt == 0
    rt = bz // bpt
    rows = bpt * nct
    h0 = jnp.stack([h0_re, h0_im], axis=0).reshape(2, rt, bpt, nb, gb, p)
    h0 = h0.transpose(3, 1, 2, 0, 4, 5).reshape(nb, rt, bpt, ns)

    y, hn = pl.pallas_call(
        functools.partial(_s5_main_kernel, chunk=chunk, nct=nct, bpt=bpt),
        grid=(nb, rt),
        in_specs=[pl.BlockSpec((rows * chunk, LANES), lambda i, r: (r, i)),
                  pl.BlockSpec((None, nw, nw + ns), lambda i, r: (i, 0, 0)),
                  pl.BlockSpec((None, ns, nw), lambda i, r: (i, 0, 0)),
                  pl.BlockSpec((None, 1, ns), lambda i, r: (i, 0, 0)),
                  pl.BlockSpec((None, None, bpt, ns), lambda i, r: (i, r, 0, 0)),
                  pl.BlockSpec((None, 1, LANES), lambda i, r: (i, 0, 0))],
        out_specs=[pl.BlockSpec((rows * chunk, LANES), lambda i, r: (r, i)),
                   pl.BlockSpec((None, None, bpt, ns), lambda i, r: (i, r, 0, 0))],
        out_shape=[jax.ShapeDtypeStruct((bz * t, a), F32),
                   jax.ShapeDtypeStruct((nb, rt, bpt, ns), F32)],
        scratch_shapes=[pltpu.VMEM((rows, ns), F32)] * 2,
        compiler_params=_params("parallel", "arbitrary"),
        name="s5_main",
    )(proj, wp, qbig, al_big, h0, d_skip.reshape(nb, 1, LANES))

    hn = hn.reshape(nb, bz, 2, gb, p).transpose(2, 1, 0, 3, 4).reshape(2, bz, g, p)
    return y, hn[0], hn[1]


def _conv_kernel(v_ref, g_ref, buf_ref, w_ref, b_ref, lg_ref, lb_ref, y_ref, nbuf_ref, win_ref, acc_ref,
                 *, tt, width, nt):
    i = pl.program_id(1)
    keep = width - 1
    off = CONV_HALO - keep

    @pl.when(i == 0)
    def _():
        win_ref[0:off, :] = jnp.zeros((off, win_ref.shape[1]), F32)
        win_ref[off:CONV_HALO, :] = buf_ref[...]

    v = v_ref[...]
    win_ref[CONV_HALO:CONV_HALO + tt, :] = v * jax.nn.sigmoid(g_ref[...])

    bw = win_ref.shape[1]
    rb = min(tt, 64)
    for r0 in range(0, tt, rb):
        for c0 in range(0, bw, 128):
            acc = jnp.zeros((rb, 128), F32)
            for k in range(width):
                acc = acc + w_ref[k:k + 1, c0:c0 + 128] * win_ref[off + r0 + k:off + r0 + k + rb, c0:c0 + 128]
            acc_ref[r0:r0 + rb, c0:c0 + 128] = acc

    y = acc_ref[...] + b_ref[...]
    mu = jnp.mean(y, axis=-1, keepdims=True)
    var = jnp.mean(jnp.square(y - mu), axis=-1, keepdims=True)
    z = (y - mu) * lax.rsqrt(var + EPS) * lg_ref[...] + lb_ref[...]
    y_ref[...] = (z * jax.nn.sigmoid(z)).astype(y_ref.dtype)

    @pl.when(i == nt - 1)
    def _():
        nbuf_ref[...] = win_ref[tt + off:tt + CONV_HALO, :]

    tail = win_ref[tt:tt + CONV_HALO, :]
    win_ref[0:CONV_HALO, :] = tail


def conv_module(proj, buf, conv_w, conv_b, ln_g, ln_b, bz, t, a_width):
    width, bw = conv_w.shape
    assert a_width % bw == 0 and width - 1 <= CONV_HALO
    tt = _tile(t, 128)
    nt = t // tt
    vcol = a_width // bw
    proj = proj.reshape(bz, t, proj.shape[-1])
    y, nbuf = pl.pallas_call(
        functools.partial(_conv_kernel, tt=tt, width=width, nt=nt),
        grid=(bz, nt),
        in_specs=[pl.BlockSpec((None, tt, bw), lambda b, i: (b, i, vcol)),
                  pl.BlockSpec((None, tt, bw), lambda b, i: (b, i, vcol + 1)),
                  pl.BlockSpec((None, width - 1, bw), lambda b, i: (b, 0, 0)),
                  pl.BlockSpec((width, bw), lambda b, i: (0, 0)),
                  pl.BlockSpec((1, bw), lambda b, i: (0, 0)),
                  pl.BlockSpec((1, bw), lambda b, i: (0, 0)),
                  pl.BlockSpec((1, bw), lambda b, i: (0, 0))],
        out_specs=[pl.BlockSpec((None, tt, bw), lambda b, i: (b, i, 0)),
                   pl.BlockSpec((None, width - 1, bw), lambda b, i: (b, 0, 0))],
        out_shape=[jax.ShapeDtypeStruct((bz, t, bw), BF16),
                   jax.ShapeDtypeStruct((bz, width - 1, bw), F32)],
        scratch_shapes=[pltpu.VMEM((CONV_HALO + tt, bw), F32), pltpu.VMEM((tt, bw), F32)],
        compiler_params=_params("parallel", "arbitrary"),
        name="conv_module",
    )(proj, proj, buf, conv_w, conv_b.reshape(1, bw), ln_g.reshape(1, bw), ln_b.reshape(1, bw))
    return y.reshape(bz * t, bw), nbuf


def _diff_lambda(lam_ref, lam_init):
    s1 = jnp.sum(lam_ref[0:1, :] * lam_ref[1:2, :], axis=-1, keepdims=True)
    s2 = jnp.sum(lam_ref[2:3, :] * lam_ref[3:4, :], axis=-1, keepdims=True)
    return jnp.exp(s1) - jnp.exp(s2) + lam_init


def _flash_kernel(qi_ref, ki_ref, lam_ref, g_ref, q_ref, k_ref, v_ref, o_ref, qs_ref, m_ref, l_ref, acc_ref,
                  *, tq, hd, hps, lam_init):
    pair = pl.program_id(2)
    qi, ki = qi_ref[pair], ki_ref[pair]
    rc = min(tq, FLASH_ROWS)

    @pl.when(ki == 0)
    def _():
        for hh in range(hps):
            q = q_ref[:, hh * LANES:(hh + 1) * LANES]
            lane = lax.broadcasted_iota(jnp.int32, q.shape, 1)
            zero = jnp.zeros_like(q)
            qs_ref[hh, 0:tq, :] = jnp.where(lane < hd, q, zero)
            qs_ref[hh, tq:2 * tq, :] = jnp.where(lane >= hd, q, zero)
        m_ref[...] = jnp.full(m_ref.shape, NEG_INF, F32)
        l_ref[...] = jnp.zeros(l_ref.shape, F32)
        acc_ref[...] = jnp.zeros(acc_ref.shape, F32)

    def step(masked):
        items = [(hh, c) for hh in range(hps) for c in range(2 * tq // rc)]

        def scores(hh, c):
            q0 = (c * rc) % tq
            kc = q0 + rc if masked else tq
            return lax.dot_general(qs_ref[hh, c * rc:(c + 1) * rc, :],
                                   k_ref[0:kc, hh * LANES:(hh + 1) * LANES],
                                   (((1,), (1,)), ((), ())), preferred_element_type=F32)

        s_next = scores(*items[0])
        for n, (hh, c) in enumerate(items):
            rows = slice(c * rc, (c + 1) * rc)
            q0 = (c * rc) % tq
            kc = q0 + rc if masked else tq
            s = s_next
            if n + 1 < len(items):
                s_next = scores(*items[n + 1])
            if masked:
                row = lax.broadcasted_iota(jnp.int32, s.shape, 0) + q0
                col = lax.broadcasted_iota(jnp.int32, s.shape, 1)
                s = jnp.where(col <= row, s, NEG_INF)
            cols = [s[:, j:j + LANES] for j in range(0, kc, LANES)]
            m_old = m_ref[hh, rows, :]
            m_new = jnp.maximum(m_old, jnp.max(functools.reduce(jnp.maximum, cols), axis=-1, keepdims=True))
            alpha = jnp.exp(m_old - m_new)
            ps = [jnp.exp(cj - m_new) for cj in cols]
            l_ref[hh, rows, :] = alpha * l_ref[hh, rows, :] + jnp.sum(functools.reduce(jnp.add, ps), axis=-1,
                                                                       keepdims=True)
            p = jnp.concatenate(ps, axis=1) if len(ps) > 1 else ps[0]
            pv = jnp.dot(p.astype(BF16), v_ref[0:kc, hh * LANES:(hh + 1) * LANES],
                         preferred_element_type=F32)
            acc_ref[hh, rows, :] = alpha * acc_ref[hh, rows, :] + pv
            m_ref[hh, rows, :] = m_new

    @pl.when(ki < qi)
    def _():
        step(False)

    @pl.when(ki == qi)
    def _():
        step(True)
        lam = _diff_lambda(lam_ref, lam_init)
        for hh in range(hps):
            o = acc_ref[hh] / l_ref[hh]
            d = o[0:tq] - lam * o[tq:2 * tq]
            o_ref[:, hh * LANES:(hh + 1) * LANES] = (_rms_rows(d, g_ref[...])
                                                     * (1.0 - lam_init)).astype(o_ref.dtype)


def prompt_attention(q, k, v, lam_vecs, subln_g, bz, s, hd, lam_init):
    t, qw = q.shape
    vd = 2 * hd
    nh = qw // vd
    tq = _tile(s, 512)
    nq = s // tq
    assert vd == LANES and tq % min(tq, FLASH_ROWS) == 0 and min(tq, FLASH_ROWS) % LANES == 0
    hps = 2 if nh % 2 == 0 else 1
    bw = hps * vd
    pairs = [(a, c) for a in range(nq) for c in range(a + 1)]
    qi_tab = jnp.asarray([a for a, _ in pairs], jnp.int32)
    ki_tab = jnp.asarray([c for _, c in pairs], jnp.int32)
    grid_spec = pltpu.PrefetchScalarGridSpec(
        num_scalar_prefetch=2,
        grid=(bz, nh // hps, len(pairs)),
        in_specs=[pl.BlockSpec((4, hd), lambda b, h, p, qt, kt: (0, 0)),
                  pl.BlockSpec((1, vd), lambda b, h, p, qt, kt: (0, 0)),
                  pl.BlockSpec((tq, bw), lambda b, h, p, qt, kt: (b * nq + qt[p], h)),
                  pl.BlockSpec((tq, bw), lambda b, h, p, qt, kt: (b * nq + kt[p], h)),
                  pl.BlockSpec((tq, bw), lambda b, h, p, qt, kt: (b * nq + kt[p], h))],
        out_specs=pl.BlockSpec((tq, bw), lambda b, h, p, qt, kt: (b * nq + qt[p], h)),
        scratch_shapes=[pltpu.VMEM((hps, 2 * tq, vd), BF16), pltpu.VMEM((hps, 2 * tq, LANES), F32),
                        pltpu.VMEM((hps, 2 * tq, LANES), F32), pltpu.VMEM((hps, 2 * tq, vd), F32)],
    )
    return pl.pallas_call(
        functools.partial(_flash_kernel, tq=tq, hd=hd, hps=hps, lam_init=lam_init),
        grid_spec=grid_spec,
        out_shape=jax.ShapeDtypeStruct((t, nh * vd), BF16),
        compiler_params=_params("parallel", "parallel", "arbitrary"),
        name="prompt_attention",
    )(qi_tab, ki_tab, lam_vecs, subln_g.reshape(1, vd), q, k, v)


def _paged_kernel(pt_ref, lam_ref, g_ref, qidx_ref, q_ref, e_ref, msk_ref, *refs,
                  n_steps, pp, rpb, lam_init):
    k_refs, v_refs = refs[:pp], refs[pp:2 * pp]
    kn_ref, vn_ref, o_ref, m_ref, l_ref, acc_ref = refs[2 * pp:]
    p = pl.program_id(1)

    @pl.when(p == 0)
    def _():
        m_ref[...] = jnp.full(m_ref.shape, NEG_INF, F32)
        l_ref[...] = jnp.zeros(l_ref.shape, F32)
        acc_ref[...] = jnp.zeros(acc_ref.shape, F32)

    def update(kts, vfs, masked):
        q = q_ref[...]
        ss = [jnp.dot(q, kt[...].astype(BF16), preferred_element_type=F32) for kt in kts]
        if masked:
            col = lax.broadcasted_iota(jnp.int32, ss[0].shape, 1)
            ss = [jnp.where(col <= qidx_ref[...], s, NEG_INF) for s in ss]
        m_old = m_ref[...]
        m_new = jnp.maximum(m_old, jnp.max(functools.reduce(jnp.maximum, ss), axis=-1, keepdims=True))
        alpha = jnp.exp(m_old - m_new)
        prs = [jnp.exp(s - m_new) for s in ss]
        l_ref[...] = alpha * l_ref[...] + jnp.sum(functools.reduce(jnp.add, prs), axis=-1, keepdims=True)
        pv = None
        for pr, vf in zip(prs, vfs):
            pe = jnp.dot(pr.astype(BF16), e_ref[...], preferred_element_type=F32)
            pe = pe.astype(BF16) * msk_ref[...]
            part = jnp.dot(pe, vf[...].astype(BF16), preferred_element_type=F32)
            pv = part if pv is None else pv + part
        acc_ref[...] = alpha * acc_ref[...] + pv
        m_ref[...] = m_new

    @pl.when(p < n_steps)
    def _():
        for c0 in range(0, pp, PAGE_GROUP):
            update(k_refs[c0:c0 + PAGE_GROUP], v_refs[c0:c0 + PAGE_GROUP], False)

    @pl.when(p == n_steps)
    def _():
        update([kn_ref], [vn_ref], True)
        o = acc_ref[...] / l_ref[...]
        lam = _diff_lambda(lam_ref, lam_init)
        r = o.shape[0]
        d = o - lam * pltpu.roll(o, r - rpb // 2, axis=0)
        o_ref[...] = _rms_rows(d, g_ref[...]) * (1.0 - lam_init)


def sample_attention(q, k_new, v_new, cache_k, cache_v, layer, page_table, lam_vecs, subln_g,
                     bz, nq, hd, lam_init):
    vd = 2 * hd
    qw = q.shape[1]
    nh = qw // vd
    n_odd, n_pool, page = cache_k.shape[:3]
    n_pages = page_table.shape[1]
    assert nq <= page
    pp = next(c for c in (8, 4, 2, 1) if n_pages % c == 0 and n_pages // c >= min(2, n_pages))
    n_steps = n_pages // pp
    rpb = 2 * nq
    r = nh * rpb
    ckt = cache_k.transpose(0, 1, 3, 4, 2).reshape(n_odd, n_pool, qw, page)
    cvf = cache_v.reshape(n_odd, n_pool, page * nh, vd)
    q4 = q.reshape(bz, nq, nh, 2, hd).transpose(0, 2, 3, 1, 4)
    eye = jnp.eye(2 * nh, dtype=q.dtype).reshape(nh, 2, 1, 2 * nh, 1)
    qbd = (q4.reshape(bz, nh, 2, nq, 1, hd) * eye[None]).reshape(bz, r, qw)
    knt = jnp.pad(k_new.reshape(bz, nq, qw).transpose(0, 2, 1), ((0, 0), (0, 0), (0, page - nq)))
    vnf = jnp.pad(v_new.reshape(bz, nq * nh, vd), ((0, 0), (0, (page - nq) * nh), (0, 0)))
    qidx = (jnp.arange(r, dtype=jnp.int32) % nq).reshape(r, 1)
    expand = jnp.repeat(jnp.eye(page, dtype=BF16), nh, axis=1)
    own = (jnp.arange(page * nh)[None, :] % nh == jnp.arange(r)[:, None] // rpb).astype(BF16)
    pt = page_table.reshape(-1).astype(jnp.int32)

    def page_map(c):
        def index(b, p, pt):
            return (layer, pt[b * n_pages + jnp.minimum(p, n_steps - 1) * pp + c], 0, 0)
        return index

    grid_spec = pltpu.PrefetchScalarGridSpec(
        num_scalar_prefetch=1,
        grid=(bz, n_steps + 1),
        in_specs=([pl.BlockSpec((4, hd), lambda b, p, pt: (0, 0)),
                   pl.BlockSpec((1, vd), lambda b, p, pt: (0, 0)),
                   pl.BlockSpec((r, 1), lambda b, p, pt: (0, 0)),
                   pl.BlockSpec((None, r, qw), lambda b, p, pt: (b, 0, 0)),
                   pl.BlockSpec((page, page * nh), lambda b, p, pt: (0, 0)),
                   pl.BlockSpec((r, page * nh), lambda b, p, pt: (0, 0))]
                  + [pl.BlockSpec((None, None, qw, page), page_map(c)) for c in range(pp)]
                  + [pl.BlockSpec((None, None, page * nh, vd), page_map(c)) for c in range(pp)]
                  + [pl.BlockSpec((None, qw, page), lambda b, p, pt: (b, 0, 0)),
                     pl.BlockSpec((None, page * nh, vd), lambda b, p, pt: (b, 0, 0))]),
        out_specs=pl.BlockSpec((None, r, vd), lambda b, p, pt: (b, 0, 0)),
        scratch_shapes=[pltpu.VMEM((r, 1), F32), pltpu.VMEM((r, 1), F32), pltpu.VMEM((r, vd), F32)],
    )
    o = pl.pallas_call(
        functools.partial(_paged_kernel, n_steps=n_steps, pp=pp, rpb=rpb, lam_init=lam_init),
        grid_spec=grid_spec,
        out_shape=jax.ShapeDtypeStruct((bz, r, vd), F32),
        compiler_params=_params("parallel", "arbitrary"),
        name="sample_attention",
    )(pt, lam_vecs, subln_g.reshape(1, vd), qidx, qbd, expand, own,
      *([ckt] * pp), *([cvf] * pp), knt, vnf)
    o = o.reshape(bz, nh, 2, nq, vd)[:, :, 0]
    return o.transpose(0, 2, 1, 3).reshape(bz * nq, nh * vd).astype(BF16)


def kernel(x_prompt, x_sample, cache_k, cache_v, state_ssm_re, state_ssm_im, state_conv, page_table, norm_mix_pre, norm_mix_post, norm_ffn_pre, norm_ffn_post, w_in_even, ssm_lam_re, ssm_lam_im, ssm_log_dt, ssm_b_re, ssm_b_im, ssm_c_re, ssm_c_im, ssm_d, w_glu, conv_w, conv_b, conv_ln_g, conv_ln_b, w_out_even, w_qkv, lambda_q1, lambda_k1, lambda_q2, lambda_k2, subln_g, w_o, w_gate, w_up, w_down):
    depth = norm_mix_pre.shape[0]
    d_model = x_prompt.shape[-1]
    a_width = w_glu.shape[1]
    n_groups, n_state = ssm_lam_re.shape[1:]
    bw = conv_w.shape[2]
    page = cache_k.shape[2]
    head_dim = cache_k.shape[-1]
    q_width = cache_k.shape[-2] * head_dim
    past_len = page_table.shape[1] * page

    w_in_b, w_glu_b, w_out_b = w_in_even.astype(BF16), w_glu.astype(BF16), w_out_even.astype(BF16)
    w_qkv_b, w_o_b = w_qkv.astype(BF16), w_o.astype(BF16)
    w_gate_b, w_up_b, w_down_b = w_gate.astype(BF16), w_up.astype(BF16), w_down.astype(BF16)
    lam_vecs = jnp.stack([lambda_q1, lambda_k1, lambda_q2, lambda_k2], axis=1)

    def run_trunk(x, pos0, ssm_re0, ssm_im0, conv0, paged):
        bz, t, _ = x.shape
        h = x.reshape(bz * t, d_model)
        pos = jnp.tile(pos0 + jnp.arange(t, dtype=F32), bz).reshape(bz * t, 1)
        new_re, new_im, new_conv, new_k, new_v = [], [], [], [], []
        for i in range(depth):
            j = i // 2
            if i % 2 == 0:
                proj = norm_matmul(h, norm_mix_pre[i], w_in_b[j])
                y_a, s_re, s_im = s5_mixer(
                    proj, bz, t, a_width, ssm_re0[j], ssm_im0[j],
                    ssm_lam_re[j], ssm_lam_im[j], ssm_log_dt[j], ssm_b_re[j], ssm_b_im[j],
                    ssm_c_re[j], ssm_c_im[j], ssm_d[j])
                y_a = s5_glu(y_a, w_glu_b[j])
                y_b, buf = conv_module(proj, conv0[j], conv_w[j], conv_b[j], conv_ln_g[j],
                                       conv_ln_b[j], bz, t, a_width)
                new_re.append(s_re)
                new_im.append(s_im)
                new_conv.append(buf)
                mix_in, w_mix = jnp.concatenate([y_a, y_b], axis=-1), w_out_b[j]
            else:
                lam_init = 0.8 - 0.6 * math.exp(-0.3 * i)
                q, k, v, kb, vb = qkv_rope(h, norm_mix_pre[i], w_qkv_b[j], pos, q_width, head_dim)
                if paged:
                    mix_in = sample_attention(q, k, v, cache_k, cache_v, j, page_table, lam_vecs[j],
                                              subln_g[j], bz, t, head_dim, lam_init)
                else:
                    mix_in = prompt_attention(q, kb, vb, lam_vecs[j], subln_g[j], bz, t, head_dim,
                                              lam_init)
                new_k.append(k.reshape(bz, t, q_width // head_dim, head_dim))
                new_v.append(v.reshape(bz, t, -1, 2 * head_dim))
                w_mix = w_o_b[j]
            h = matmul_norm_residual(mix_in, w_mix, norm_mix_post[i], h)
            h = ffn(h, norm_ffn_pre[i], norm_ffn_post[i], w_gate_b[i], w_up_b[i], w_down_b[i])
        return (h.reshape(bz, t, d_model), jnp.stack(new_re), jnp.stack(new_im), jnp.stack(new_conv),
                jnp.stack(new_k), jnp.stack(new_v))

    n_prompt = x_prompt.shape[0]
    n_even = state_ssm_re.shape[0]
    zero_ssm = jnp.zeros((n_even, n_prompt, n_groups, n_state), F32)
    zero_conv = jnp.zeros((n_even, n_prompt, conv_w.shape[1] - 1, bw), F32)
    y_p, re_p, im_p, conv_p, k_p, v_p = run_trunk(x_prompt, 0.0, zero_ssm, zero_ssm, zero_conv, False)
    y_s, re_s, im_s, conv_s, k_s, v_s = run_trunk(x_sample, float(past_len), state_ssm_re, state_ssm_im,
                                                  state_conv, True)
    return (y_p, y_s, re_p, im_p, conv_p, k_p, v_p, re_s, im_s, conv_s, k_s, v_s)
```

```python
import functools
import math

import jax
import jax.numpy as jnp
from jax import lax
from jax.experimental import pallas as pl
from jax.experimental.pallas import tpu as pltpu

F32 = jnp.float32
BF16 = jnp.bfloat16
EPS = 1e-6
NEG_INF = -1e30
ROPE_THETA = 500000.0
S5_CHUNK = 16
S5_ROWS = 256
CONV_HALO = 32
LANES = 128
FLASH_ROWS = 256
PAGE_GROUP = 4
VMEM_LIMIT = 56 * 1024 * 1024
HI = lax.Precision.HIGHEST


def _tile(n, pref):
    if n <= pref:
        return n
    t = pref
    while t >= 8:
        if n % t == 0:
            return t
        t //= 2
    return n


def _params(*sem):
    return pltpu.CompilerParams(dimension_semantics=sem, vmem_limit_bytes=VMEM_LIMIT)


def _rms_rows(x, g, eps=EPS):
    ms = jnp.mean(x * x, axis=-1, keepdims=True)
    return x * lax.rsqrt(ms + eps) * g


def _norm_matmul_kernel(x_ref, g_ref, w_ref, o_ref, hn_ref):
    @pl.when(pl.program_id(1) == 0)
    def _():
        hn_ref[...] = _rms_rows(x_ref[...], g_ref[...]).astype(BF16)

    o_ref[...] = jnp.dot(hn_ref[...], w_ref[...], preferred_element_type=F32)


def norm_matmul(x, g, w):
    t, d = x.shape
    n = w.shape[1]
    tm, tn = _tile(t, 1024), _tile(n, 512)
    return pl.pallas_call(
        _norm_matmul_kernel,
        grid=(t // tm, n // tn),
        in_specs=[pl.BlockSpec((tm, d), lambda i, j: (i, 0)),
                  pl.BlockSpec((1, d), lambda i, j: (0, 0)),
                  pl.BlockSpec((d, tn), lambda i, j: (0, j))],
        out_specs=pl.BlockSpec((tm, tn), lambda i, j: (i, j)),
        out_shape=jax.ShapeDtypeStruct((t, n), F32),
        scratch_shapes=[pltpu.VMEM((tm, d), BF16)],
        compiler_params=_params("parallel", "arbitrary"),
        name="norm_matmul",
    )(x, g.reshape(1, d), w)


def _qkv_kernel(x_ref, g_ref, pos_ref, invf_ref, sel_ref, w_ref,
                q_ref, k_ref, v_ref, kb_ref, vb_ref, hn_ref, tab_ref, *, nq_tiles, scale):
    j = pl.program_id(1)

    @pl.when(j == 0)
    def _():
        hn_ref[...] = _rms_rows(x_ref[...], g_ref[...]).astype(BF16)
        ang = pos_ref[...] * invf_ref[...]
        c, s = jnp.cos(ang), jnp.sin(ang)
        tab_ref[0] = jnp.where(sel_ref[0:1, :] > 0.5, c, 1.0)
        tab_ref[1] = s * sel_ref[1:2, :]
        tab_ref[2] = s * sel_ref[2:3, :]

    tn = w_ref.shape[1]
    cw = min(tn, 2 * LANES)
    half = LANES // 16

    def rope(y):
        parts = []
        for c0 in range(0, y.shape[1], LANES):
            x = y[:, c0:c0 + LANES]
            parts.append(x * tab_ref[0] + pltpu.roll(x, half, axis=1) * tab_ref[1]
                         + pltpu.roll(x, LANES - half, axis=1) * tab_ref[2])
        return jnp.concatenate(parts, axis=1) if len(parts) > 1 else parts[0]

    def sweep(emit):
        hn = hn_ref[...]
        chunks = list(range(0, tn, cw))
        y_next = jnp.dot(hn, w_ref[:, 0:cw], preferred_element_type=F32)
        for n, c0 in enumerate(chunks):
            y = y_next
            if n + 1 < len(chunks):
                y_next = jnp.dot(hn, w_ref[:, c0 + cw:c0 + 2 * cw], preferred_element_type=F32)
            emit(slice(c0, c0 + cw), y)

    @pl.when(j < nq_tiles)
    def _():
        def emit(cols, y):
            q_ref[:, cols] = (rope(y) * scale).astype(BF16)
        sweep(emit)

    @pl.when(jnp.logical_and(j >= nq_tiles, j < 2 * nq_tiles))
    def _():
        def emit(cols, y):
            r = rope(y)
            k_ref[:, cols] = r
            kb_ref[:, cols] = r.astype(BF16)
        sweep(emit)

    @pl.when(j >= 2 * nq_tiles)
    def _():
        def emit(cols, y):
            v_ref[:, cols] = y
            vb_ref[:, cols] = y.astype(BF16)
        sweep(emit)


def qkv_rope(x, g, w, pos, q_width, head_dim):
    t, d = x.shape
    n = w.shape[1]
    v_width = n - 2 * q_width
    rot = head_dim // 4
    assert rot == 16 and 128 % head_dim == 0
    tm = _tile(t, 512)
    tn = _tile(math.gcd(q_width, v_width), 1024)
    nq, nv = q_width // tn, v_width // tn
    lane = jnp.arange(128) % head_dim
    inv_freq = ROPE_THETA ** (-jnp.arange(rot // 2, dtype=F32) * 2.0 / rot)
    invf = jnp.where(lane < rot, inv_freq[lane % (rot // 2)], 0.0).reshape(1, 128).astype(F32)
    sel = jnp.stack([(lane < rot).astype(F32),
                     jnp.logical_and(lane >= rot // 2, lane < rot).astype(F32),
                     -(lane < rot // 2).astype(F32)])

    def clip(j, lo, cnt):
        return jnp.clip(j - lo, 0, cnt - 1)

    outs = pl.pallas_call(
        functools.partial(_qkv_kernel, nq_tiles=nq, scale=head_dim ** -0.5),
        grid=(t // tm, 2 * nq + nv),
        in_specs=[pl.BlockSpec((tm, d), lambda i, j: (i, 0)),
                  pl.BlockSpec((1, d), lambda i, j: (0, 0)),
                  pl.BlockSpec((tm, 1), lambda i, j: (i, 0)),
                  pl.BlockSpec((1, 128), lambda i, j: (0, 0)),
                  pl.BlockSpec((3, 128), lambda i, j: (0, 0)),
                  pl.BlockSpec((d, tn), lambda i, j: (0, j))],
        out_specs=[pl.BlockSpec((tm, tn), lambda i, j: (i, clip(j, 0, nq))),
                   pl.BlockSpec((tm, tn), lambda i, j: (i, clip(j, nq, nq))),
                   pl.BlockSpec((tm, tn), lambda i, j: (i, clip(j, 2 * nq, nv))),
                   pl.BlockSpec((tm, tn), lambda i, j: (i, clip(j, nq, nq))),
                   pl.BlockSpec((tm, tn), lambda i, j: (i, clip(j, 2 * nq, nv)))],
        out_shape=[jax.ShapeDtypeStruct((t, q_width), BF16),
                   jax.ShapeDtypeStruct((t, q_width), F32),
                   jax.ShapeDtypeStruct((t, v_width), F32),
                   jax.ShapeDtypeStruct((t, q_width), BF16),
                   jax.ShapeDtypeStruct((t, v_width), BF16)],
        scratch_shapes=[pltpu.VMEM((tm, d), BF16), pltpu.VMEM((3, tm, 128), F32)],
        compiler_params=_params("parallel", "arbitrary"),
        name="qkv_rope",
    )(x, g.reshape(1, d), pos, invf, sel, w)
    return outs


def _matmul_norm_res_kernel(a_ref, w_ref, g_ref, h_ref, o_ref, acc_ref, *, nj, tn):
    j = pl.program_id(1)
    acc_ref[j] = jnp.dot(a_ref[...], w_ref[...], preferred_element_type=F32)

    @pl.when(j == nj - 1)
    def _():
        ss = jnp.zeros((acc_ref.shape[1], 1), F32)
        for jj in range(nj):
            y = acc_ref[jj]
            ss = ss + jnp.sum(y * y, axis=-1, keepdims=True)
        inv = lax.rsqrt(ss / (nj * tn) + EPS)
        for jj in range(nj):
            sl = slice(jj * tn, (jj + 1) * tn)
            o_ref[:, sl] = h_ref[:, sl] + acc_ref[jj] * inv * g_ref[:, sl]


def matmul_norm_residual(a, w, g, h):
    t, k = a.shape
    d = w.shape[1]
    tm, tn = _tile(t, 512), _tile(d, 512)
    nj = d // tn
    return pl.pallas_call(
        functools.partial(_matmul_norm_res_kernel, nj=nj, tn=tn),
        grid=(t // tm, nj),
        in_specs=[pl.BlockSpec((tm, k), lambda i, j: (i, 0)),
                  pl.BlockSpec((k, tn), lambda i, j: (0, j)),
                  pl.BlockSpec((1, d), lambda i, j: (0, 0)),
                  pl.BlockSpec((tm, d), lambda i, j: (i, 0))],
        out_specs=pl.BlockSpec((tm, d), lambda i, j: (i, 0)),
        out_shape=jax.ShapeDtypeStruct((t, d), F32),
        scratch_shapes=[pltpu.VMEM((nj, tm, tn), F32)],
        compiler_params=_params("parallel", "arbitrary"),
        name="matmul_norm_residual",
    )(a, w, g.reshape(1, d), h)


def _ffn_kernel(h_ref, gpre_ref, gpost_ref, wg_ref, wu_ref, wd_ref, o_ref, hn_ref, *, nf):
    f = pl.program_id(1)

    @pl.when(f == 0)
    def _():
        hn_ref[...] = _rms_rows(h_ref[...], gpre_ref[...]).astype(BF16)
        o_ref[...] = jnp.zeros(o_ref.shape, F32)

    hn = hn_ref[...]
    gate = jnp.dot(hn, wg_ref[...], preferred_element_type=F32)
    up = jnp.dot(hn, wu_ref[...], preferred_element_type=F32)
    act = (gate * jax.nn.sigmoid(gate) * up).astype(BF16)
    o_ref[...] += jnp.dot(act, wd_ref[...], preferred_element_type=F32)

    @pl.when(f == nf - 1)
    def _():
        o_ref[...] = h_ref[...] + _rms_rows(o_ref[...], gpost_ref[...])


def ffn(h, g_pre, g_post, wg, wu, wd):
    t, d = h.shape
    fh = wg.shape[1]
    tm = _tile(t, 1024)
    tf = next((c for c in (512, 256) if fh % c == 0), fh)
    nf = fh // tf
    return pl.pallas_call(
        functools.partial(_ffn_kernel, nf=nf),
        grid=(t // tm, nf),
        in_specs=[pl.BlockSpec((tm, d), lambda i, f: (i, 0), pipeline_mode=pl.Buffered(1)),
                  pl.BlockSpec((1, d), lambda i, f: (0, 0)),
                  pl.BlockSpec((1, d), lambda i, f: (0, 0)),
                  pl.BlockSpec((d, tf), lambda i, f: (0, f)),
                  pl.BlockSpec((d, tf), lambda i, f: (0, f)),
                  pl.BlockSpec((tf, d), lambda i, f: (f, 0))],
        out_specs=pl.BlockSpec((tm, d), lambda i, f: (i, 0)),
        out_shape=jax.ShapeDtypeStruct((t, d), F32),
        scratch_shapes=[pltpu.VMEM((tm, d), BF16)],
        compiler_params=_params("parallel", "arbitrary"),
        name="ffn",
    )(h, g_pre.reshape(1, d), g_post.reshape(1, d), wg, wu, wd)


def _glu_kernel(y_ref, w_ref, o_ref):
    y = y_ref[...]
    z = jnp.dot(y.astype(BF16), w_ref[...], preferred_element_type=F32)
    o_ref[...] = (y * jax.nn.sigmoid(z)).astype(o_ref.dtype)


def s5_glu(y, w):
    t, a = y.shape
    tm = _tile(t, 1024)
    return pl.pallas_call(
        _glu_kernel,
        grid=(t // tm,),
        in_specs=[pl.BlockSpec((tm, a), lambda i: (i, 0)),
                  pl.BlockSpec((a, a), lambda i: (0, 0))],
        out_specs=pl.BlockSpec((tm, a), lambda i: (i, 0)),
        out_shape=jax.ShapeDtypeStruct((t, a), BF16),
        compiler_params=_params("parallel"),
        name="s5_glu",
    )(y, w)


def _s5_prep_kernel(lam_row_ref, lam_col_ref, bt_ref, ct_ref, e_ref, tt_ref,
                    g0_ref, p_ref, q_ref, al_ref, *, chunk, gq):
    for gi in range(gq):
        _s5_prep_group(lam_row_ref.at[gi], lam_col_ref.at[gi], bt_ref.at[gi], ct_ref.at[gi], e_ref, tt_ref,
                       g0_ref.at[gi], p_ref.at[gi], q_ref.at[gi], al_ref.at[gi], chunk=chunk)


def _s5_prep_group(lam_row_ref, lam_col_ref, bt_ref, ct_ref, e_ref, tt_ref,
                   g0_ref, p_ref, q_ref, al_ref, *, chunk):
    lr, li, ldt = lam_row_ref[0:1, :], lam_row_ref[1:2, :], lam_row_ref[2:3, :]
    dt = jnp.exp(ldt)
    mag = jnp.exp(lr * dt)
    ab_re, ab_im = mag * jnp.cos(li * dt), mag * jnp.sin(li * dt)
    den = lr * lr + li * li
    num_re = ab_re - 1.0
    coef_re = (num_re * lr + ab_im * li) / den
    coef_im = (ab_im * lr - num_re * li) / den
    bt_re, bt_im = bt_ref[0], bt_ref[1]
    bb_re = coef_re * bt_re - coef_im * bt_im
    bb_im = coef_re * bt_im + coef_im * bt_re

    magl = jnp.exp(lr * dt * chunk)
    al_ref[0:1, :] = magl * jnp.cos(li * dt * chunk)
    al_ref[1:2, :] = magl * jnp.sin(li * dt * chunk)

    e = e_ref[...]
    pm = jnp.exp(lr * dt * e)
    pw_re, pw_im = pm * jnp.cos(li * dt * e), pm * jnp.sin(li * dt * e)
    bbt_re = jnp.concatenate([bb_re] * chunk, axis=0)
    bbt_im = jnp.concatenate([bb_im] * chunk, axis=0)
    n_state = pw_re.shape[1]
    p_ref[:, 0:n_state] = pw_re * bbt_re - pw_im * bbt_im
    p_ref[:, n_state:2 * n_state] = pw_re * bbt_im + pw_im * bbt_re

    lrc, lic, ldtc = lam_col_ref[0], lam_col_ref[1], lam_col_ref[2]
    dtc = jnp.exp(ldtc)
    tt = tt_ref[...]
    fm = jnp.exp(lrc * dtc * tt)
    fw_re, fw_im = fm * jnp.cos(lic * dtc * tt), fm * jnp.sin(lic * dtc * tt)
    ct_re, ct_im = ct_ref[0], ct_ref[1]
    f_re = ct_re * fw_re - ct_im * fw_im
    f_im = ct_re * fw_im + ct_im * fw_re
    g0_ref[...] = (jnp.dot(bb_re, f_re, precision=HI, preferred_element_type=F32)
                   - jnp.dot(bb_im, f_im, precision=HI, preferred_element_type=F32))
    magc = jnp.exp(lrc * dtc)
    abc_re, abc_im = magc * jnp.cos(lic * dtc), magc * jnp.sin(lic * dtc)
    q_ref[0] = (f_re * abc_re - f_im * abc_im).astype(q_ref.dtype)
    q_ref[1] = (-(f_re * abc_im + f_im * abc_re)).astype(q_ref.dtype)


def _s5_assemble_kernel(a_ref, p_ref, q_ref, tw_ref, tp_ref, rgw_ref, rgs_ref, cgw_ref, cgs_ref,
                        wp_ref, qb_ref, *, nw, ns):
    a = a_ref[...].astype(BF16)
    pm = p_ref[...].astype(BF16)
    q = q_ref[...]
    cs = min(nw, 512)
    for c0 in range(0, nw, cs):
        cols = slice(c0, c0 + cs)
        w = jnp.dot(a, tw_ref[:, cols], preferred_element_type=F32)
        wp_ref[:, cols] = jnp.where(rgw_ref[...] == cgw_ref[:, cols], w, 0.0).astype(BF16)
        qv = jnp.dot(q, tw_ref[:, cols], preferred_element_type=F32)
        qb_ref[:, cols] = jnp.where(rgs_ref[...] == cgw_ref[:, cols], qv, 0.0).astype(BF16)
    cs = min(ns, 512)
    for c0 in range(0, ns, cs):
        cols = slice(c0, c0 + cs)
        pv = jnp.dot(pm, tp_ref[:, cols], preferred_element_type=F32)
        wp_ref[:, nw + c0:nw + c0 + cs] = jnp.where(rgw_ref[...] == cgs_ref[:, cols], pv, 0.0).astype(BF16)


def _s5_main_kernel(u_ref, w_ref, q_ref, al_ref, h0_ref, d_ref,
                    y_ref, hn_ref, bc_ref, xs_ref, *, chunk, nct, bpt):
    rows = bc_ref.shape[0]
    nw = chunk * LANES
    half = bc_ref.shape[1] // 2
    us = [u_ref[pl.ds(s, rows, stride=chunk), :] for s in range(chunk)]
    ucat = jnp.concatenate(us, axis=1).astype(BF16)
    r = jnp.dot(ucat, w_ref[...], preferred_element_type=F32)
    bc_ref[...] = r[:, nw:]
    a_re, a_im = al_ref[:, 0:half], al_ref[:, half:]

    def step(j, carry):
        nxt = []
        for bl in range(bpt):
            x_re, x_im = carry[2 * bl], carry[2 * bl + 1]
            row = pl.ds(bl * nct + j, 1)
            xs_ref[row, 0:half] = x_re
            xs_ref[row, half:] = x_im
            nxt.append(a_re * x_re - a_im * x_im + bc_ref[row, 0:half])
            nxt.append(a_re * x_im + a_im * x_re + bc_ref[row, half:])
        return tuple(nxt)

    init = tuple(h0_ref[bl:bl + 1, sl] for bl in range(bpt) for sl in (slice(0, half), slice(half, 2 * half)))
    fin = lax.fori_loop(0, nct, step, init)
    for bl in range(bpt):
        hn_ref[bl:bl + 1, 0:half] = fin[2 * bl]
        hn_ref[bl:bl + 1, half:] = fin[2 * bl + 1]
    yc = jnp.dot(xs_ref[...].astype(BF16), q_ref[...], preferred_element_type=F32)
    for t in range(chunk):
        sl = slice(t * LANES, (t + 1) * LANES)
        y_ref[pl.ds(t, rows, stride=chunk), :] = jax.nn.gelu(r[:, sl] + yc[:, sl] + d_ref[...] * us[t])


def s5_mixer(proj, bz, t, a, h0_re, h0_im, lam_re, lam_im, log_dt, b_re, b_im, c_re, c_im, d_skip):
    g, p = lam_re.shape
    gc = a // g
    chunk = S5_CHUNK if t % S5_CHUNK == 0 else t
    nct = t // chunk
    lc = chunk * gc

    lam_row = jnp.stack([lam_re, lam_im, jnp.broadcast_to(log_dt[:, None], (g, p))], axis=1)
    lam_col = jnp.broadcast_to(lam_row[:, :, :, None], (g, 3, p, lc))
    bt = jnp.stack([b_re, b_im], axis=1).transpose(0, 1, 3, 2)
    ct = jnp.tile(jnp.stack([c_re, c_im], axis=1).transpose(0, 1, 3, 2), (1, 1, 1, chunk))

    step = jnp.arange(lc, dtype=jnp.int32) // gc
    e_col = (chunk - 1 - step).astype(F32).reshape(lc, 1)
    tt_row = step.astype(F32).reshape(1, lc)

    gq = next(c for c in (8, 4, 2, 1) if g % c == 0)
    g0, pmat, qmat, al = pl.pallas_call(
        functools.partial(_s5_prep_kernel, chunk=chunk, gq=gq),
        grid=(g // gq,),
        in_specs=[pl.BlockSpec((gq, 3, p), lambda i: (i, 0, 0)),
                  pl.BlockSpec((gq, 3, p, lc), lambda i: (i, 0, 0, 0)),
                  pl.BlockSpec((gq, 2, gc, p), lambda i: (i, 0, 0, 0)),
                  pl.BlockSpec((gq, 2, p, lc), lambda i: (i, 0, 0, 0)),
                  pl.BlockSpec((lc, 1), lambda i: (0, 0)),
                  pl.BlockSpec((1, lc), lambda i: (0, 0))],
        out_specs=[pl.BlockSpec((gq, gc, lc), lambda i: (i, 0, 0)),
                   pl.BlockSpec((gq, lc, 2 * p), lambda i: (i, 0, 0)),
                   pl.BlockSpec((gq, 2, p, lc), lambda i: (i, 0, 0, 0)),
                   pl.BlockSpec((gq, 2, p), lambda i: (i, 0, 0))],
        out_shape=[jax.ShapeDtypeStruct((g, gc, lc), F32),
                   jax.ShapeDtypeStruct((g, lc, 2 * p), F32),
                   jax.ShapeDtypeStruct((g, 2, p, lc), BF16),
                   jax.ShapeDtypeStruct((g, 2, p), F32)],
        compiler_params=_params("parallel"),
        name="s5_prep",
    )(lam_row, lam_col, bt, ct, e_col, tt_row)

    g0p = jnp.pad(g0.reshape(g, gc, chunk, gc), ((0, 0), (0, 0), (chunk, 0), (0, 0)))
    wt = jnp.stack([g0p[:, :, chunk - s:2 * chunk - s, :] for s in range(chunk)], axis=1)

    assert LANES % gc == 0 and g % (LANES // gc) == 0
    gb = LANES // gc
    nb = g // gb
    nw, ns = chunk * LANES, 2 * gb * p
    a_rows = wt.reshape(nb, gb, chunk, gc, lc).transpose(0, 2, 1, 3, 4).reshape(nb, nw, lc)
    p_rows = pmat.reshape(nb, gb, chunk, gc, 2 * p).transpose(0, 2, 1, 3, 4).reshape(nb, nw, 2 * p)
    q_rows = qmat.reshape(nb, gb, 2, p, lc).transpose(0, 2, 1, 3, 4).reshape(nb, ns, lc)
    tile_w = jnp.kron(jnp.eye(chunk, dtype=BF16), jnp.tile(jnp.eye(gc, dtype=BF16), (1, gb)))
    tile_p = jnp.kron(jnp.eye(2, dtype=BF16), jnp.tile(jnp.eye(p, dtype=BF16), (1, gb)))
    grp_w = (jnp.arange(nw, dtype=jnp.int32) // gc) % gb
    grp_s = (jnp.arange(ns, dtype=jnp.int32) // p) % gb
    wp, qbig = pl.pallas_call(
        functools.partial(_s5_assemble_kernel, nw=nw, ns=ns),
        grid=(nb,),
        in_specs=[pl.BlockSpec((None, nw, lc), lambda i: (i, 0, 0)),
                  pl.BlockSpec((None, nw, 2 * p), lambda i: (i, 0, 0)),
                  pl.BlockSpec((None, ns, lc), lambda i: (i, 0, 0)),
                  pl.BlockSpec((lc, nw), lambda i: (0, 0)),
                  pl.BlockSpec((2 * p, ns), lambda i: (0, 0)),
                  pl.BlockSpec((nw, 1), lambda i: (0, 0)),
                  pl.BlockSpec((ns, 1), lambda i: (0, 0)),
                  pl.BlockSpec((1, nw), lambda i: (0, 0)),
                  pl.BlockSpec((1, ns), lambda i: (0, 0))],
        out_specs=[pl.BlockSpec((None, nw, nw + ns), lambda i: (i, 0, 0)),
                   pl.BlockSpec((None, ns, nw), lambda i: (i, 0, 0))],
        out_shape=[jax.ShapeDtypeStruct((nb, nw, nw + ns), BF16),
                   jax.ShapeDtypeStruct((nb, ns, nw), BF16)],
        compiler_params=_params("parallel"),
        name="s5_assemble",
    )(a_rows, p_rows, q_rows, tile_w, tile_p, grp_w.reshape(nw, 1), grp_s.reshape(ns, 1),
      grp_w.reshape(1, nw), grp_s.reshape(1, ns))
    al_big = al.reshape(nb, gb, 2, p).transpose(0, 2, 1, 3).reshape(nb, 1, ns)

    bpt = max(1, min(bz, S5_ROWS // nct))
    assert bz % bpt == 0
    rt = bz // bpt
    rows = bpt * nct
    h0 = jnp.stack([h0_re, h0_im], axis=0).reshape(2, rt, bpt, nb, gb, p)
    h0 = h0.transpose(3, 1, 2, 0, 4, 5).reshape(nb, rt, bpt, ns)

    y, hn = pl.pallas_call(
        functools.partial(_s5_main_kernel, chunk=chunk, nct=nct, bpt=bpt),
        grid=(nb, rt),
        in_specs=[pl.BlockSpec((rows * chunk, LANES), lambda i, r: (r, i)),
                  pl.BlockSpec((None, nw, nw + ns), lambda i, r: (i, 0, 0)),
                  pl.BlockSpec((None, ns, nw), lambda i, r: (i, 0, 0)),
                  pl.BlockSpec((None, 1, ns), lambda i, r: (i, 0, 0)),
                  pl.BlockSpec((None, None, bpt, ns), lambda i, r: (i, r, 0, 0)),
                  pl.BlockSpec((None, 1, LANES), lambda i, r: (i, 0, 0))],
        out_specs=[pl.BlockSpec((rows * chunk, LANES), lambda i, r: (r, i)),
                   pl.BlockSpec((None, None, bpt, ns), lambda i, r: (i, r, 0, 0))],
        out_shape=[jax.ShapeDtypeStruct((bz * t, a), F32),
                   jax.ShapeDtypeStruct((nb, rt, bpt, ns), F32)],
        scratch_shapes=[pltpu.VMEM((rows, ns), F32)] * 2,
        compiler_params=_params("parallel", "arbitrary"),
        name="s5_main",
    )(proj, wp, qbig, al_big, h0, d_skip.reshape(nb, 1, LANES))

    hn = hn.reshape(nb, bz, 2, gb, p).transpose(2, 1, 0, 3, 4).reshape(2, bz, g, p)
    return y, hn[0], hn[1]


def _conv_kernel(v_ref, g_ref, buf_ref, w_ref, b_ref, lg_ref, lb_ref, y_ref, nbuf_ref, win_ref, acc_ref,
                 *, tt, width, nt):
    i = pl.program_id(1)
    keep = width - 1
    off = CONV_HALO - keep

    @pl.when(i == 0)
    def _():
        win_ref[0:off, :] = jnp.zeros((off, win_ref.shape[1]), F32)
        win_ref[off:CONV_HALO, :] = buf_ref[...]

    v = v_ref[...]
    win_ref[CONV_HALO:CONV_HALO + tt, :] = v * jax.nn.sigmoid(g_ref[...])

    bw = win_ref.shape[1]
    rb = min(tt, 64)
    for r0 in range(0, tt, rb):
        for c0 in range(0, bw, 128):
            acc = jnp.zeros((rb, 128), F32)
            for k in range(width):
                acc = acc + w_ref[k:k + 1, c0:c0 + 128] * win_ref[off + r0 + k:off + r0 + k + rb, c0:c0 + 128]
            acc_ref[r0:r0 + rb, c0:c0 + 128] = acc

    y = acc_ref[...] + b_ref[...]
    mu = jnp.mean(y, axis=-1, keepdims=True)
    var = jnp.mean(jnp.square(y - mu), axis=-1, keepdims=True)
    z = (y - mu) * lax.rsqrt(var + EPS) * lg_ref[...] + lb_ref[...]
    y_ref[...] = (z * jax.nn.sigmoid(z)).astype(y_ref.dtype)

    @pl.when(i == nt - 1)
    def _():
        nbuf_ref[...] = win_ref[tt + off:tt + CONV_HALO, :]

    tail = win_ref[tt:tt + CONV_HALO, :]
    win_ref[0:CONV_HALO, :] = tail


def conv_module(proj, buf, conv_w, conv_b, ln_g, ln_b, bz, t, a_width):
    width, bw = conv_w.shape
    assert a_width % bw == 0 and width - 1 <= CONV_HALO
    tt = _tile(t, 128)
    nt = t // tt
    vcol = a_width // bw
    proj = proj.reshape(bz, t, proj.shape[-1])
    y, nbuf = pl.pallas_call(
        functools.partial(_conv_kernel, tt=tt, width=width, nt=nt),
        grid=(bz, nt),
        in_specs=[pl.BlockSpec((None, tt, bw), lambda b, i: (b, i, vcol)),
                  pl.BlockSpec((None, tt, bw), lambda b, i: (b, i, vcol + 1)),
                  pl.BlockSpec((None, width - 1, bw), lambda b, i: (b, 0, 0)),
                  pl.BlockSpec((width, bw), lambda b, i: (0, 0)),
                  pl.BlockSpec((1, bw), lambda b, i: (0, 0)),
                  pl.BlockSpec((1, bw), lambda b, i: (0, 0)),
                  pl.BlockSpec((1, bw), lambda b, i: (0, 0))],
        out_specs=[pl.BlockSpec((None, tt, bw), lambda b, i: (b, i, 0)),
                   pl.BlockSpec((None, width - 1, bw), lambda b, i: (b, 0, 0))],
        out_shape=[jax.ShapeDtypeStruct((bz, t, bw), BF16),
                   jax.ShapeDtypeStruct((bz, width - 1, bw), F32)],
        scratch_shapes=[pltpu.VMEM((CONV_HALO + tt, bw), F32), pltpu.VMEM((tt, bw), F32)],
        compiler_params=_params("parallel", "arbitrary"),
        name="conv_module",
    )(proj, proj, buf, conv_w, conv_b.reshape(1, bw), ln_g.reshape(1, bw), ln_b.reshape(1, bw))
    return y.reshape(bz * t, bw), nbuf


def _diff_lambda(lam_ref, lam_init):
    s1 = jnp.sum(lam_ref[0:1, :] * lam_ref[1:2, :], axis=-1, keepdims=True)
    s2 = jnp.sum(lam_ref[2:3, :] * lam_ref[3:4, :], axis=-1, keepdims=True)
    return jnp.exp(s1) - jnp.exp(s2) + lam_init


def _flash_kernel(qi_ref, ki_ref, lam_ref, g_ref, q_ref, k_ref, v_ref, o_ref, qs_ref, m_ref, l_ref, acc_ref,
                  *, tq, hd, hps, lam_init):
    pair = pl.program_id(2)
    qi, ki = qi_ref[pair], ki_ref[pair]
    rc = min(tq, FLASH_ROWS)

    @pl.when(ki == 0)
    def _():
        for hh in range(hps):
            q = q_ref[:, hh * LANES:(hh + 1) * LANES]
            lane = lax.broadcasted_iota(jnp.int32, q.shape, 1)
            zero = jnp.zeros_like(q)
            qs_ref[hh, 0:tq, :] = jnp.where(lane < hd, q, zero)
            qs_ref[hh, tq:2 * tq, :] = jnp.where(lane >= hd, q, zero)
        m_ref[...] = jnp.full(m_ref.shape, NEG_INF, F32)
        l_ref[...] = jnp.zeros(l_ref.shape, F32)
        acc_ref[...] = jnp.zeros(acc_ref.shape, F32)

    def step(masked):
        items = [(hh, c) for hh in range(hps) for c in range(2 * tq // rc)]

        def scores(hh, c):
            q0 = (c * rc) % tq
            kc = q0 + rc if masked else tq
            return lax.dot_general(qs_ref[hh, c * rc:(c + 1) * rc, :],
                                   k_ref[0:kc, hh * LANES:(hh + 1) * LANES],
                                   (((1,), (1,)), ((), ())), preferred_element_type=F32)

        s_next = scores(*items[0])
        for n, (hh, c) in enumerate(items):
            rows = slice(c * rc, (c + 1) * rc)
            q0 = (c * rc) % tq
            kc = q0 + rc if masked else tq
            s = s_next
            if n + 1 < len(items):
                s_next = scores(*items[n + 1])
            if masked:
                row = lax.broadcasted_iota(jnp.int32, s.shape, 0) + q0
                col = lax.broadcasted_iota(jnp.int32, s.shape, 1)
                s = jnp.where(col <= row, s, NEG_INF)
            cols = [s[:, j:j + LANES] for j in range(0, kc, LANES)]
            m_old = m_ref[hh, rows, :]
            m_new = jnp.maximum(m_old, jnp.max(functools.reduce(jnp.maximum, cols), axis=-1, keepdims=True))
            alpha = jnp.exp(m_old - m_new)
            ps = [jnp.exp(cj - m_new) for cj in cols]
            l_ref[hh, rows, :] = alpha * l_ref[hh, rows, :] + jnp.sum(functools.reduce(jnp.add, ps), axis=-1,
                                                                       keepdims=True)
            p = jnp.concatenate(ps, axis=1) if len(ps) > 1 else ps[0]
            pv = jnp.dot(p.astype(BF16), v_ref[0:kc, hh * LANES:(hh + 1) * LANES],
                         preferred_element_type=F32)
            acc_ref[hh, rows, :] = alpha * acc_ref[hh, rows, :] + pv
            m_ref[hh, rows, :] = m_new

    @pl.when(ki < qi)
    def _():
        step(False)

    @pl.when(ki == qi)
    def _():
        step(True)
        lam = _diff_lambda(lam_ref, lam_init)
        for hh in range(hps):
            o = acc_ref[hh] / l_ref[hh]
            d = o[0:tq] - lam * o[tq:2 * tq]
            o_ref[:, hh * LANES:(hh + 1) * LANES] = (_rms_rows(d, g_ref[...])
                                                     * (1.0 - lam_init)).astype(o_ref.dtype)


def prompt_attention(q, k, v, lam_vecs, subln_g, bz, s, hd, lam_init):
    t, qw = q.shape
    vd = 2 * hd
    nh = qw // vd
    tq = _tile(s, 512)
    nq = s // tq
    assert vd == LANES and tq % min(tq, FLASH_ROWS) == 0 and min(tq, FLASH_ROWS) % LANES == 0
    hps = 2 if nh % 2 == 0 else 1
    bw = hps * vd
    pairs = [(a, c) for a in range(nq) for c in range(a + 1)]
    qi_tab = jnp.asarray([a for a, _ in pairs], jnp.int32)
    ki_tab = jnp.asarray([c for _, c in pairs], jnp.int32)
    grid_spec = pltpu.PrefetchScalarGridSpec(
        num_scalar_prefetch=2,
        grid=(bz, nh // hps, len(pairs)),
        in_specs=[pl.BlockSpec((4, hd), lambda b, h, p, qt, kt: (0, 0)),
                  pl.BlockSpec((1, vd), lambda b, h, p, qt, kt: (0, 0)),
                  pl.BlockSpec((tq, bw), lambda b, h, p, qt, kt: (b * nq + qt[p], h)),
                  pl.BlockSpec((tq, bw), lambda b, h, p, qt, kt: (b * nq + kt[p], h)),
                  pl.BlockSpec((tq, bw), lambda b, h, p, qt, kt: (b * nq + kt[p], h))],
        out_specs=pl.BlockSpec((tq, bw), lambda b, h, p, qt, kt: (b * nq + qt[p], h)),
        scratch_shapes=[pltpu.VMEM((hps, 2 * tq, vd), BF16), pltpu.VMEM((hps, 2 * tq, LANES), F32),
                        pltpu.VMEM((hps, 2 * tq, LANES), F32), pltpu.VMEM((hps, 2 * tq, vd), F32)],
    )
    return pl.pallas_call(
        functools.partial(_flash_kernel, tq=tq, hd=hd, hps=hps, lam_init=lam_init),
        grid_spec=grid_spec,
        out_shape=jax.ShapeDtypeStruct((t, nh * vd), BF16),
        compiler_params=_params("parallel", "parallel", "arbitrary"),
        name="prompt_attention",
    )(qi_tab, ki_tab, lam_vecs, subln_g.reshape(1, vd), q, k, v)


def _paged_kernel(pt_ref, lam_ref, g_ref, qidx_ref, q_ref, e_ref, msk_ref, *refs,
                  n_steps, pp, rpb, lam_init):
    k_refs, v_refs = refs[:pp], refs[pp:2 * pp]
    kn_ref, vn_ref, o_ref, m_ref, l_ref, acc_ref = refs[2 * pp:]
    p = pl.program_id(1)

    @pl.when(p == 0)
    def _():
        m_ref[...] = jnp.full(m_ref.shape, NEG_INF, F32)
        l_ref[...] = jnp.zeros(l_ref.shape, F32)
        acc_ref[...] = jnp.zeros(acc_ref.shape, F32)

    def update(kts, vfs, masked):
        q = q_ref[...]
        ss = [jnp.dot(q, kt[...].astype(BF16), preferred_element_type=F32) for kt in kts]
        if masked:
            col = lax.broadcasted_iota(jnp.int32, ss[0].shape, 1)
            ss = [jnp.where(col <= qidx_ref[...], s, NEG_INF) for s in ss]
        m_old = m_ref[...]
        m_new = jnp.maximum(m_old, jnp.max(functools.reduce(jnp.maximum, ss), axis=-1, keepdims=True))
        alpha = jnp.exp(m_old - m_new)
        prs = [jnp.exp(s - m_new) for s in ss]
        l_ref[...] = alpha * l_ref[...] + jnp.sum(functools.reduce(jnp.add, prs), axis=-1, keepdims=True)
        pv = None
        for pr, vf in zip(prs, vfs):
            pe = jnp.dot(pr.astype(BF16), e_ref[...], preferred_element_type=F32)
            pe = pe.astype(BF16) * msk_ref[...]
            part = jnp.dot(pe, vf[...].astype(BF16), preferred_element_type=F32)
            pv = part if pv is None else pv + part
        acc_ref[...] = alpha * acc_ref[...] + pv
        m_ref[...] = m_new

    @pl.when(p < n_steps)
    def _():
        for c0 in range(0, pp, PAGE_GROUP):
            update(k_refs[c0:c0 + PAGE_GROUP], v_refs[c0:c0 + PAGE_GROUP], False)

    @pl.when(p == n_steps)
    def _():
        update([kn_ref], [vn_ref], True)
        o = acc_ref[...] / l_ref[...]
        lam = _diff_lambda(lam_ref, lam_init)
        r = o.shape[0]
        d = o - lam * pltpu.roll(o, r - rpb // 2, axis=0)
        o_ref[...] = _rms_rows(d, g_ref[...]) * (1.0 - lam_init)


def sample_attention(q, k_new, v_new, cache_k, cache_v, layer, page_table, lam_vecs, subln_g,
                     bz, nq, hd, lam_init):
    vd = 2 * hd
    qw = q.shape[1]
    nh = qw // vd
    n_odd, n_pool, page = cache_k.shape[:3]
    n_pages = page_table.shape[1]
    assert nq <= page
    pp = next(c for c in (8, 4, 2, 1) if n_pages % c == 0 and n_pages // c >= min(2, n_pages))
    n_steps = n_pages // pp
    rpb = 2 * nq
    r = nh * rpb
    ckt = cache_k.transpose(0, 1, 3, 4, 2).reshape(n_odd, n_pool, qw, page)
    cvf = cache_v.reshape(n_odd, n_pool, page * nh, vd)
    q4 = q.reshape(bz, nq, nh, 2, hd).transpose(0, 2, 3, 1, 4)
    eye = jnp.eye(2 * nh, dtype=q.dtype).reshape(nh, 2, 1, 2 * nh, 1)
    qbd = (q4.reshape(bz, nh, 2, nq, 1, hd) * eye[None]).reshape(bz, r, qw)
    knt = jnp.pad(k_new.reshape(bz, nq, qw).transpose(0, 2, 1), ((0, 0), (0, 0), (0, page - nq)))
    vnf = jnp.pad(v_new.reshape(bz, nq * nh, vd), ((0, 0), (0, (page - nq) * nh), (0, 0)))
    qidx = (jnp.arange(r, dtype=jnp.int32) % nq).reshape(r, 1)
    expand = jnp.repeat(jnp.eye(page, dtype=BF16), nh, axis=1)
    own = (jnp.arange(page * nh)[None, :] % nh == jnp.arange(r)[:, None] // rpb).astype(BF16)
    pt = page_table.reshape(-1).astype(jnp.int32)

    def page_map(c):
        def index(b, p, pt):
            return (layer, pt[b * n_pages + jnp.minimum(p, n_steps - 1) * pp + c], 0, 0)
        return index

    grid_spec = pltpu.PrefetchScalarGridSpec(
        num_scalar_prefetch=1,
        grid=(bz, n_steps + 1),
        in_specs=([pl.BlockSpec((4, hd), lambda b, p, pt: (0, 0)),
                   pl.BlockSpec((1, vd), lambda b, p, pt: (0, 0)),
                   pl.BlockSpec((r, 1), lambda b, p, pt: (0, 0)),
                   pl.BlockSpec((None, r, qw), lambda b, p, pt: (b, 0, 0)),
                   pl.BlockSpec((page, page * nh), lambda b, p, pt: (0, 0)),
                   pl.BlockSpec((r, page * nh), lambda b, p, pt: (0, 0))]
                  + [pl.BlockSpec((None, None, qw, page), page_map(c)) for c in range(pp)]
                  + [pl.BlockSpec((None, None, page * nh, vd), page_map(c)) for c in range(pp)]
                  + [pl.BlockSpec((None, qw, page), lambda b, p, pt: (b, 0, 0)),
                     pl.BlockSpec((None, page * nh, vd), lambda b, p, pt: (b, 0, 0))]),
        out_specs=pl.BlockSpec((None, r, vd), lambda b, p, pt: (b, 0, 0)),
        scratch_shapes=[pltpu.VMEM((r, 1), F32), pltpu.VMEM((r, 1), F32), pltpu.VMEM((r, vd), F32)],
    )
    o = pl.pallas_call(
        functools.partial(_paged_kernel, n_steps=n_steps, pp=pp, rpb=rpb, lam_init=lam_init),
        grid_spec=grid_spec,
        out_shape=jax.ShapeDtypeStruct((bz, r, vd), F32),
        compiler_params=_params("parallel", "arbitrary"),
        name="sample_attention",
    )(pt, lam_vecs, subln_g.reshape(1, vd), qidx, qbd, expand, own,
      *([ckt] * pp), *([cvf] * pp), knt, vnf)
    o = o.reshape(bz, nh, 2, nq, vd)[:, :, 0]
    return o.transpose(0, 2, 1, 3).reshape(bz * nq, nh * vd).astype(BF16)


def kernel(x_prompt, x_sample, cache_k, cache_v, state_ssm_re, state_ssm_im, state_conv, page_table, norm_mix_pre, norm_mix_post, norm_ffn_pre, norm_ffn_post, w_in_even, ssm_lam_re, ssm_lam_im, ssm_log_dt, ssm_b_re, ssm_b_im, ssm_c_re, ssm_c_im, ssm_d, w_glu, conv_w, conv_b, conv_ln_g, conv_ln_b, w_out_even, w_qkv, lambda_q1, lambda_k1, lambda_q2, lambda_k2, subln_g, w_o, w_gate, w_up, w_down):
    depth = norm_mix_pre.shape[0]
    d_model = x_prompt.shape[-1]
    a_width = w_glu.shape[1]
    n_groups, n_state = ssm_lam_re.shape[1:]
    bw = conv_w.shape[2]
    page = cache_k.shape[2]
    head_dim = cache_k.shape[-1]
    q_width = cache_k.shape[-2] * head_dim
    past_len = page_table.shape[1] * page

    w_in_b, w_glu_b, w_out_b = w_in_even.astype(BF16), w_glu.astype(BF16), w_out_even.astype(BF16)
    w_qkv_b, w_o_b = w_qkv.astype(BF16), w_o.astype(BF16)
    w_gate_b, w_up_b, w_down_b = w_gate.astype(BF16), w_up.astype(BF16), w_down.astype(BF16)
    lam_vecs = jnp.stack([lambda_q1, lambda_k1, lambda_q2, lambda_k2], axis=1)

    def run_trunk(x, pos0, ssm_re0, ssm_im0, conv0, paged):
        bz, t, _ = x.shape
        h = x.reshape(bz * t, d_model)
        pos = jnp.tile(pos0 + jnp.arange(t, dtype=F32), bz).reshape(bz * t, 1)
        new_re, new_im, new_conv, new_k, new_v = [], [], [], [], []
        for i in range(depth):
            j = i // 2
            if i % 2 == 0:
                proj = norm_matmul(h, norm_mix_pre[i], w_in_b[j])
                y_a, s_re, s_im = s5_mixer(
                    proj, bz, t, a_width, ssm_re0[j], ssm_im0[j],
                    ssm_lam_re[j], ssm_lam_im[j], ssm_log_dt[j], ssm_b_re[j], ssm_b_im[j],
                    ssm_c_re[j], ssm_c_im[j], ssm_d[j])
                y_a = s5_glu(y_a, w_glu_b[j])
                y_b, buf = conv_module(proj, conv0[j], conv_w[j], conv_b[j], conv_ln_g[j],
                                       conv_ln_b[j], bz, t, a_width)
                new_re.append(s_re)
                new_im.append(s_im)
                new_conv.append(buf)
                mix_in, w_mix = jnp.concatenate([y_a, y_b], axis=-1), w_out_b[j]
            else:
                lam_init = 0.8 - 0.6 * math.exp(-0.3 * i)
                q, k, v, kb, vb = qkv_rope(h, norm_mix_pre[i], w_qkv_b[j], pos, q_width, head_dim)
                if paged:
                    mix_in = sample_attention(q, k, v, cache_k, cache_v, j, page_table, lam_vecs[j],
                                              subln_g[j], bz, t, head_dim, lam_init)
                else:
                    mix_in = prompt_attention(q, kb, vb, lam_vecs[j], subln_g[j], bz, t, head_dim,
                                              lam_init)
                new_k.append(k.reshape(bz, t, q_width // head_dim, head_dim))
                new_v.append(v.reshape(bz, t, -1, 2 * head_dim))
                w_mix = w_o_b[j]
            h = matmul_norm_residual(mix_in, w_mix, norm_mix_post[i], h)
            h = ffn(h, norm_ffn_pre[i], norm_ffn_post[i], w_gate_b[i], w_up_b[i], w_down_b[i])
        return (h.reshape(bz, t, d_model), jnp.stack(new_re), jnp.stack(new_im), jnp.stack(new_conv),
                jnp.stack(new_k), jnp.stack(new_v))

    n_prompt = x_prompt.shape[0]
    n_even = state_ssm_re.shape[0]
    zero_ssm = jnp.zeros((n_even, n_prompt, n_groups, n_state), F32)
    zero_conv = jnp.zeros((n_even, n_prompt, conv_w.shape[1] - 1, bw), F32)
    y_p, re_p, im_p, conv_p, k_p, v_p = run_trunk(x_prompt, 0.0, zero_ssm, zero_ssm, zero_conv, False)
    y_s, re_s, im_s, conv_s, k_s, v_s = run_trunk(x_sample, float(past_len), state_ssm_re, state_ssm_im,
                                                  state_conv, True)
    return (y_p, y_s, re_p, im_p, conv_p, k_p, v_p, re_s, im_s, conv_s, k_s, v_s)
```

```python
import functools
import math

import jax
import jax.numpy as jnp
from jax import lax
from jax.experimental import pallas as pl
from jax.experimental.pallas import tpu as pltpu

F32 = jnp.float32
BF16 = jnp.bfloat16
EPS = 1e-6
NEG_INF = -1e30
ROPE_THETA = 500000.0
S5_CHUNK = 16
S5_ROWS = 256
CONV_HALO = 32
LANES = 128
FLASH_ROWS = 256
PAGE_GROUP = 4
VMEM_LIMIT = 56 * 1024 * 1024
HI = lax.Precision.HIGHEST


def _tile(n, pref):
    if n <= pref:
        return n
    t = pref
    while t >= 8:
        if n % t == 0:
            return t
        t //= 2
    return n


def _params(*sem):
    return pltpu.CompilerParams(dimension_semantics=sem, vmem_limit_bytes=VMEM_LIMIT)


def _rms_rows(x, g, eps=EPS):
    ms = jnp.mean(x * x, axis=-1, keepdims=True)
    return x * lax.rsqrt(ms + eps) * g


def _norm_matmul_kernel(x_ref, g_ref, w_ref, o_ref, hn_ref):
    @pl.when(pl.program_id(1) == 0)
    def _():
        hn_ref[...] = _rms_rows(x_ref[...], g_ref[...]).astype(BF16)

    o_ref[...] = jnp.dot(hn_ref[...], w_ref[...], preferred_element_type=F32)


def norm_matmul(x, g, w):
    t, d = x.shape
    n = w.shape[1]
    tm, tn = _tile(t, 1024), _tile(n, 512)
    return pl.pallas_call(
        _norm_matmul_kernel,
        grid=(t // tm, n // tn),
        in_specs=[pl.BlockSpec((tm, d), lambda i, j: (i, 0)),
                  pl.BlockSpec((1, d), lambda i, j: (0, 0)),
                  pl.BlockSpec((d, tn), lambda i, j: (0, j))],
        out_specs=pl.BlockSpec((tm, tn), lambda i, j: (i, j)),
        out_shape=jax.ShapeDtypeStruct((t, n), F32),
        scratch_shapes=[pltpu.VMEM((tm, d), BF16)],
        compiler_params=_params("parallel", "arbitrary"),
        name="norm_matmul",
    )(x, g.reshape(1, d), w)


def _qkv_kernel(x_ref, g_ref, pos_ref, invf_ref, sel_ref, w_ref, *rest, nq_tiles, scale, n_prev, k_major):
    q_ref, k_ref, v_ref, kb_ref, vb_ref, hn_ref, tab_ref = rest[n_prev:]
    j = pl.program_id(1)

    @pl.when(j == 0)
    def _():
        hn_ref[...] = _rms_rows(x_ref[...], g_ref[...]).astype(BF16)
        ang = pos_ref[...] * invf_ref[...]
        c, s = jnp.cos(ang), jnp.sin(ang)
        tab_ref[0] = jnp.where(sel_ref[0:1, :] > 0.5, c, 1.0)
        tab_ref[1] = s * sel_ref[1:2, :]
        tab_ref[2] = s * sel_ref[2:3, :]

    tn = w_ref.shape[1]
    cw = min(tn, 2 * LANES)
    half = LANES // 16

    def rope(y):
        parts = []
        for c0 in range(0, y.shape[1], LANES):
            x = y[:, c0:c0 + LANES]
            parts.append(x * tab_ref[0] + pltpu.roll(x, half, axis=1) * tab_ref[1]
                         + pltpu.roll(x, LANES - half, axis=1) * tab_ref[2])
        return jnp.concatenate(parts, axis=1) if len(parts) > 1 else parts[0]

    def sweep(emit):
        hn = hn_ref[...]
        chunks = list(range(0, tn, cw))
        y_next = jnp.dot(hn, w_ref[:, 0:cw], preferred_element_type=F32)
        for n, c0 in enumerate(chunks):
            y = y_next
            if n + 1 < len(chunks):
                y_next = jnp.dot(hn, w_ref[:, c0 + cw:c0 + 2 * cw], preferred_element_type=F32)
            emit(slice(c0, c0 + cw), y)

    @pl.when(j < nq_tiles)
    def _():
        def emit(cols, y):
            q_ref[:, cols] = (rope(y) * scale).astype(BF16)
        sweep(emit)

    @pl.when(jnp.logical_and(j >= nq_tiles, j < 2 * nq_tiles))
    def _():
        def emit(cols, y):
            r = rope(y)
            if k_major:
                k_ref[cols, :] = r.T
            else:
                k_ref[:, cols] = r
            kb_ref[:, cols] = r.astype(BF16)
        sweep(emit)

    @pl.when(j >= 2 * nq_tiles)
    def _():
        def emit(cols, y):
            v_ref[:, cols] = y
            vb_ref[:, cols] = y.astype(BF16)
        sweep(emit)


def qkv_rope(x, g, w, pos, q_width, head_dim, layer, n_layers, seq, k_all=None, v_all=None):
    t, d = x.shape
    n = w.shape[1]
    v_width = n - 2 * q_width
    rot = head_dim // 4
    assert rot == 16 and 128 % head_dim == 0
    tm = _tile(t, 512)
    tn = _tile(math.gcd(q_width, v_width), 1024)
    nq, nv = q_width // tn, v_width // tn
    k_major = seq % tm == 0 and tm % LANES == 0
    tps = seq // tm if k_major else 1
    prev = [] if k_all is None else [k_all, v_all]
    lane = jnp.arange(128) % head_dim
    inv_freq = ROPE_THETA ** (-jnp.arange(rot // 2, dtype=F32) * 2.0 / rot)
    invf = jnp.where(lane < rot, inv_freq[lane % (rot // 2)], 0.0).reshape(1, 128).astype(F32)
    sel = jnp.stack([(lane < rot).astype(F32),
                     jnp.logical_and(lane >= rot // 2, lane < rot).astype(F32),
                     -(lane < rot // 2).astype(F32)])

    def clip(j, lo, cnt):
        return jnp.clip(j - lo, 0, cnt - 1)

    if k_major:
        k_spec = pl.BlockSpec((None, None, tn, tm), lambda i, j: (layer, i // tps, clip(j, nq, nq), i % tps))
        k_shape = jax.ShapeDtypeStruct((n_layers, t // seq, q_width, seq), F32)
    else:
        k_spec = pl.BlockSpec((None, tm, tn), lambda i, j: (layer, i, clip(j, nq, nq)))
        k_shape = jax.ShapeDtypeStruct((n_layers, t, q_width), F32)
    outs = pl.pallas_call(
        functools.partial(_qkv_kernel, nq_tiles=nq, scale=head_dim ** -0.5, n_prev=len(prev), k_major=k_major),
        grid=(t // tm, 2 * nq + nv),
        in_specs=[pl.BlockSpec((tm, d), lambda i, j: (i, 0)),
                  pl.BlockSpec((1, d), lambda i, j: (0, 0)),
                  pl.BlockSpec((tm, 1), lambda i, j: (i, 0)),
                  pl.BlockSpec((1, 128), lambda i, j: (0, 0)),
                  pl.BlockSpec((3, 128), lambda i, j: (0, 0)),
                  pl.BlockSpec((d, tn), lambda i, j: (0, j))]
                 + [pl.BlockSpec(memory_space=pl.ANY)] * len(prev),
        out_specs=[pl.BlockSpec((tm, tn), lambda i, j: (i, clip(j, 0, nq))),
                   k_spec,
                   pl.BlockSpec((None, tm, tn), lambda i, j: (layer, i, clip(j, 2 * nq, nv))),
                   pl.BlockSpec((tm, tn), lambda i, j: (i, clip(j, nq, nq))),
                   pl.BlockSpec((tm, tn), lambda i, j: (i, clip(j, 2 * nq, nv)))],
        out_shape=[jax.ShapeDtypeStruct((t, q_width), BF16),
                   k_shape,
                   jax.ShapeDtypeStruct((n_layers, t, v_width), F32),
                   jax.ShapeDtypeStruct((t, q_width), BF16),
                   jax.ShapeDtypeStruct((t, v_width), BF16)],
        scratch_shapes=[pltpu.VMEM((tm, d), BF16), pltpu.VMEM((3, tm, 128), F32)],
        input_output_aliases={6 + n: 1 + n for n in range(len(prev))},
        compiler_params=_params("parallel", "arbitrary"),
        name="qkv_rope",
    )(x, g.reshape(1, d), pos, invf, sel, w, *prev)
    return outs


def _matmul_norm_res_kernel(a_ref, w_ref, g_ref, h_ref, o_ref, acc_ref, *, nj, tn):
    j = pl.program_id(1)
    acc_ref[j] = jnp.dot(a_ref[...], w_ref[...], preferred_element_type=F32)

    @pl.when(j == nj - 1)
    def _():
        ss = jnp.zeros((acc_ref.shape[1], 1), F32)
        for jj in range(nj):
            y = acc_ref[jj]
            ss = ss + jnp.sum(y * y, axis=-1, keepdims=True)
        inv = lax.rsqrt(ss / (nj * tn) + EPS)
        for jj in range(nj):
            sl = slice(jj * tn, (jj + 1) * tn)
            o_ref[:, sl] = h_ref[:, sl] + acc_ref[jj] * inv * g_ref[:, sl]


def matmul_norm_residual(a, w, g, h):
    t, k = a.shape
    d = w.shape[1]
    tm, tn = _tile(t, 512), _tile(d, 512)
    nj = d // tn
    return pl.pallas_call(
        functools.partial(_matmul_norm_res_kernel, nj=nj, tn=tn),
        grid=(t // tm, nj),
        in_specs=[pl.BlockSpec((tm, k), lambda i, j: (i, 0)),
                  pl.BlockSpec((k, tn), lambda i, j: (0, j)),
                  pl.BlockSpec((1, d), lambda i, j: (0, 0)),
                  pl.BlockSpec((tm, d), lambda i, j: (i, 0))],
        out_specs=pl.BlockSpec((tm, d), lambda i, j: (i, 0)),
        out_shape=jax.ShapeDtypeStruct((t, d), F32),
        scratch_shapes=[pltpu.VMEM((nj, tm, tn), F32)],
        compiler_params=_params("parallel", "arbitrary"),
        name="matmul_norm_residual",
    )(a, w, g.reshape(1, d), h)


def _ffn_kernel(h_ref, gpre_ref, gpost_ref, wg_ref, wu_ref, wd_ref, o_ref, hn_ref, *, nf):
    f = pl.program_id(1)

    @pl.when(f == 0)
    def _():
        hn_ref[...] = _rms_rows(h_ref[...], gpre_ref[...]).astype(BF16)
        o_ref[...] = jnp.zeros(o_ref.shape, F32)

    hn = hn_ref[...]
    gate = jnp.dot(hn, wg_ref[...], preferred_element_type=F32)
    up = jnp.dot(hn, wu_ref[...], preferred_element_type=F32)
    act = (gate * jax.nn.sigmoid(gate) * up).astype(BF16)
    o_ref[...] += jnp.dot(act, wd_ref[...], preferred_element_type=F32)

    @pl.when(f == nf - 1)
    def _():
        o_ref[...] = h_ref[...] + _rms_rows(o_ref[...], gpost_ref[...])


def ffn(h, g_pre, g_post, wg, wu, wd):
    t, d = h.shape
    fh = wg.shape[1]
    tm = _tile(t, 1024)
    tf = next((c for c in (512, 256) if fh % c == 0), fh)
    nf = fh // tf
    return pl.pallas_call(
        functools.partial(_ffn_kernel, nf=nf),
        grid=(t // tm, nf),
        in_specs=[pl.BlockSpec((tm, d), lambda i, f: (i, 0), pipeline_mode=pl.Buffered(1)),
                  pl.BlockSpec((1, d), lambda i, f: (0, 0)),
                  pl.BlockSpec((1, d), lambda i, f: (0, 0)),
                  pl.BlockSpec((d, tf), lambda i, f: (0, f)),
                  pl.BlockSpec((d, tf), lambda i, f: (0, f)),
                  pl.BlockSpec((tf, d), lambda i, f: (f, 0))],
        out_specs=pl.BlockSpec((tm, d), lambda i, f: (i, 0)),
        out_shape=jax.ShapeDtypeStruct((t, d), F32),
        scratch_shapes=[pltpu.VMEM((tm, d), BF16)],
        compiler_params=_params("parallel", "arbitrary"),
        name="ffn",
    )(h, g_pre.reshape(1, d), g_post.reshape(1, d), wg, wu, wd)


def _glu_kernel(y_ref, w_ref, o_ref):
    y = y_ref[...]
    z = jnp.dot(y.astype(BF16), w_ref[...], preferred_element_type=F32)
    o_ref[...] = (y * jax.nn.sigmoid(z)).astype(o_ref.dtype)


def s5_glu(y, w):
    t, a = y.shape
    tm = _tile(t, 1024)
    return pl.pallas_call(
        _glu_kernel,
        grid=(t // tm,),
        in_specs=[pl.BlockSpec((tm, a), lambda i: (i, 0)),
                  pl.BlockSpec((a, a), lambda i: (0, 0))],
        out_specs=pl.BlockSpec((tm, a), lambda i: (i, 0)),
        out_shape=jax.ShapeDtypeStruct((t, a), BF16),
        compiler_params=_params("parallel"),
        name="s5_glu",
    )(y, w)


def _s5_prep_kernel(lam_row_ref, lam_col_ref, bt_ref, ct_ref, e_ref, tt_ref,
                    g0_ref, p_ref, q_ref, al_ref, *, chunk, gq):
    for gi in range(gq):
        _s5_prep_group(lam_row_ref.at[gi], lam_col_ref.at[gi], bt_ref.at[gi], ct_ref.at[gi], e_ref, tt_ref,
                       g0_ref.at[gi], p_ref.at[gi], q_ref.at[gi], al_ref.at[gi], chunk=chunk)


def _s5_prep_group(lam_row_ref, lam_col_ref, bt_ref, ct_ref, e_ref, tt_ref,
                   g0_ref, p_ref, q_ref, al_ref, *, chunk):
    lr, li, ldt = lam_row_ref[0:1, :], lam_row_ref[1:2, :], lam_row_ref[2:3, :]
    dt = jnp.exp(ldt)
    mag = jnp.exp(lr * dt)
    ab_re, ab_im = mag * jnp.cos(li * dt), mag * jnp.sin(li * dt)
    den = lr * lr + li * li
    num_re = ab_re - 1.0
    coef_re = (num_re * lr + ab_im * li) / den
    coef_im = (ab_im * lr - num_re * li) / den
    bt_re, bt_im = bt_ref[0], bt_ref[1]
    bb_re = coef_re * bt_re - coef_im * bt_im
    bb_im = coef_re * bt_im + coef_im * bt_re

    magl = jnp.exp(lr * dt * chunk)
    al_ref[0:1, :] = magl * jnp.cos(li * dt * chunk)
    al_ref[1:2, :] = magl * jnp.sin(li * dt * chunk)

    e = e_ref[...]
    pm = jnp.exp(lr * dt * e)
    pw_re, pw_im = pm * jnp.cos(li * dt * e), pm * jnp.sin(li * dt * e)
    bbt_re = jnp.concatenate([bb_re] * chunk, axis=0)
    bbt_im = jnp.concatenate([bb_im] * chunk, axis=0)
    n_state = pw_re.shape[1]
    p_ref[:, 0:n_state] = pw_re * bbt_re - pw_im * bbt_im
    p_ref[:, n_state:2 * n_state] = pw_re * bbt_im + pw_im * bbt_re

    lrc, lic, ldtc = lam_col_ref[0], lam_col_ref[1], lam_col_ref[2]
    dtc = jnp.exp(ldtc)
    tt = tt_ref[...]
    fm = jnp.exp(lrc * dtc * tt)
    fw_re, fw_im = fm * jnp.cos(lic * dtc * tt), fm * jnp.sin(lic * dtc * tt)
    ct_re, ct_im = ct_ref[0], ct_ref[1]
    f_re = ct_re * fw_re - ct_im * fw_im
    f_im = ct_re * fw_im + ct_im * fw_re
    g0_ref[...] = (jnp.dot(bb_re, f_re, precision=HI, preferred_element_type=F32)
                   - jnp.dot(bb_im, f_im, precision=HI, preferred_element_type=F32))
    magc = jnp.exp(lrc * dtc)
    abc_re, abc_im = magc * jnp.cos(lic * dtc), magc * jnp.sin(lic * dtc)
    q_ref[0] = (f_re * abc_re - f_im * abc_im).astype(q_ref.dtype)
    q_ref[1] = (-(f_re * abc_im + f_im * abc_re)).astype(q_ref.dtype)


def _s5_assemble_kernel(a_ref, p_ref, q_ref, tw_ref, tp_ref, rgw_ref, rgs_ref, cgw_ref, cgs_ref,
                        wp_ref, qb_ref, *, nw, ns):
    a = a_ref[...].astype(BF16)
    pm = p_ref[...].astype(BF16)
    q = q_ref[...]
    cs = min(nw, 512)
    for c0 in range(0, nw, cs):
        cols = slice(c0, c0 + cs)
        w = jnp.dot(a, tw_ref[:, cols], preferred_element_type=F32)
        wp_ref[:, cols] = jnp.where(rgw_ref[...] == cgw_ref[:, cols], w, 0.0).astype(BF16)
        qv = jnp.dot(q, tw_ref[:, cols], preferred_element_type=F32)
        qb_ref[:, cols] = jnp.where(rgs_ref[...] == cgw_ref[:, cols], qv, 0.0).astype(BF16)
    cs = min(ns, 512)
    for c0 in range(0, ns, cs):
        cols = slice(c0, c0 + cs)
        pv = jnp.dot(pm, tp_ref[:, cols], preferred_element_type=F32)
        wp_ref[:, nw + c0:nw + c0 + cs] = jnp.where(rgw_ref[...] == cgs_ref[:, cols], pv, 0.0).astype(BF16)


def _s5_main_kernel(u_ref, w_ref, q_ref, al_ref, h0_ref, d_ref,
                    y_ref, hn_ref, bc_ref, xs_ref, *, chunk, nct, bpt):
    rows = bc_ref.shape[0]
    nw = chunk * LANES
    half = bc_ref.shape[1] // 2
    us = [u_ref[pl.ds(s, rows, stride=chunk), :] for s in range(chunk)]
    ucat = jnp.concatenate(us, axis=1).astype(BF16)
    r = jnp.dot(ucat, w_ref[...], preferred_element_type=F32)
    bc_ref[...] = r[:, nw:]
    a_re, a_im = al_ref[:, 0:half], al_ref[:, half:]

    def step(j, carry):
        nxt = []
        for bl in range(bpt):
            x_re, x_im = carry[2 * bl], carry[2 * bl + 1]
            row = pl.ds(bl * nct + j, 1)
            xs_ref[row, 0:half] = x_re
            xs_ref[row, half:] = x_im
            nxt.append(a_re * x_re - a_im * x_im + bc_ref[row, 0:half])
            nxt.append(a_re * x_im + a_im * x_re + bc_ref[row, half:])
        return tuple(nxt)

    init = tuple(h0_ref[bl:bl + 1, sl] for bl in range(bpt) for sl in (slice(0, half), slice(half, 2 * half)))
    fin = lax.fori_loop(0, nct, step, init)
    for bl in range(bpt):
        hn_ref[bl:bl + 1, 0:half] = fin[2 * bl]
        hn_ref[bl:bl + 1, half:] = fin[2 * bl + 1]
    yc = jnp.dot(xs_ref[...].astype(BF16), q_ref[...], preferred_element_type=F32)
    for t in range(chunk):
        sl = slice(t * LANES, (t + 1) * LANES)
        y_ref[pl.ds(t, rows, stride=chunk), :] = jax.nn.gelu(r[:, sl] + yc[:, sl] + d_ref[...] * us[t])


def s5_mixer(proj, bz, t, a, h0_re, h0_im, lam_re, lam_im, log_dt, b_re, b_im, c_re, c_im, d_skip):
    g, p = lam_re.shape
    gc = a // g
    chunk = S5_CHUNK if t % S5_CHUNK == 0 else t
    nct = t // chunk
    lc = chunk * gc

    lam_row = jnp.stack([lam_re, lam_im, jnp.broadcast_to(log_dt[:, None], (g, p))], axis=1)
    lam_col = jnp.broadcast_to(lam_row[:, :, :, None], (g, 3, p, lc))
    bt = jnp.stack([b_re, b_im], axis=1).transpose(0, 1, 3, 2)
    ct = jnp.tile(jnp.stack([c_re, c_im], axis=1).transpose(0, 1, 3, 2), (1, 1, 1, chunk))

    step = jnp.arange(lc, dtype=jnp.int32) // gc
    e_col = (chunk - 1 - step).astype(F32).reshape(lc, 1)
    tt_row = step.astype(F32).reshape(1, lc)

    gq = next(c for c in (8, 4, 2, 1) if g % c == 0)
    g0, pmat, qmat, al = pl.pallas_call(
        functools.partial(_s5_prep_kernel, chunk=chunk, gq=gq),
        grid=(g // gq,),
        in_specs=[pl.BlockSpec((gq, 3, p), lambda i: (i, 0, 0)),
                  pl.BlockSpec((gq, 3, p, lc), lambda i: (i, 0, 0, 0)),
                  pl.BlockSpec((gq, 2, gc, p), lambda i: (i, 0, 0, 0)),
                  pl.BlockSpec((gq, 2, p, lc), lambda i: (i, 0, 0, 0)),
                  pl.BlockSpec((lc, 1), lambda i: (0, 0)),
                  pl.BlockSpec((1, lc), lambda i: (0, 0))],
        out_specs=[pl.BlockSpec((gq, gc, lc), lambda i: (i, 0, 0)),
                   pl.BlockSpec((gq, lc, 2 * p), lambda i: (i, 0, 0)),
                   pl.BlockSpec((gq, 2, p, lc), lambda i: (i, 0, 0, 0)),
                   pl.BlockSpec((gq, 2, p), lambda i: (i, 0, 0))],
        out_shape=[jax.ShapeDtypeStruct((g, gc, lc), F32),
                   jax.ShapeDtypeStruct((g, lc, 2 * p), F32),
                   jax.ShapeDtypeStruct((g, 2, p, lc), BF16),
                   jax.ShapeDtypeStruct((g, 2, p), F32)],
        compiler_params=_params("parallel"),
        name="s5_prep",
    )(lam_row, lam_col, bt, ct, e_col, tt_row)

    g0p = jnp.pad(g0.reshape(g, gc, chunk, gc), ((0, 0), (0, 0), (chunk, 0), (0, 0)))
    wt = jnp.stack([g0p[:, :, chunk - s:2 * chunk - s, :] for s in range(chunk)], axis=1)

    assert LANES % gc == 0 and g % (LANES // gc) == 0
    gb = LANES // gc
    nb = g // gb
    nw, ns = chunk * LANES, 2 * gb * p
    a_rows = wt.reshape(nb, gb, chunk, gc, lc).transpose(0, 2, 1, 3, 4).reshape(nb, nw, lc)
    p_rows = pmat.reshape(nb, gb, chunk, gc, 2 * p).transpose(0, 2, 1, 3, 4).reshape(nb, nw, 2 * p)
    q_rows = qmat.reshape(nb, gb, 2, p, lc).transpose(0, 2, 1, 3, 4).reshape(nb, ns, lc)
    tile_w = jnp.kron(jnp.eye(chunk, dtype=BF16), jnp.tile(jnp.eye(gc, dtype=BF16), (1, gb)))
    tile_p = jnp.kron(jnp.eye(2, dtype=BF16), jnp.tile(jnp.eye(p, dtype=BF16), (1, gb)))
    grp_w = (jnp.arange(nw, dtype=jnp.int32) // gc) % gb
    grp_s = (jnp.arange(ns, dtype=jnp.int32) // p) % gb
    wp, qbig = pl.pallas_call(
        functools.partial(_s5_assemble_kernel, nw=nw, ns=ns),
        grid=(nb,),
        in_specs=[pl.BlockSpec((None, nw, lc), lambda i: (i, 0, 0)),
                  pl.BlockSpec((None, nw, 2 * p), lambda i: (i, 0, 0)),
                  pl.BlockSpec((None, ns, lc), lambda i: (i, 0, 0)),
                  pl.BlockSpec((lc, nw), lambda i: (0, 0)),
                  pl.BlockSpec((2 * p, ns), lambda i: (0, 0)),
                  pl.BlockSpec((nw, 1), lambda i: (0, 0)),
                  pl.BlockSpec((ns, 1), lambda i: (0, 0)),
                  pl.BlockSpec((1, nw), lambda i: (0, 0)),
                  pl.BlockSpec((1, ns), lambda i: (0, 0))],
        out_specs=[pl.BlockSpec((None, nw, nw + ns), lambda i: (i, 0, 0)),
                   pl.BlockSpec((None, ns, nw), lambda i: (i, 0, 0))],
        out_shape=[jax.ShapeDtypeStruct((nb, nw, nw + ns), BF16),
                   jax.ShapeDtypeStruct((nb, ns, nw), BF16)],
        compiler_params=_params("parallel"),
        name="s5_assemble",
    )(a_rows, p_rows, q_rows, tile_w, tile_p, grp_w.reshape(nw, 1), grp_s.reshape(ns, 1),
      grp_w.reshape(1, nw), grp_s.reshape(1, ns))
    al_big = al.reshape(nb, gb, 2, p).transpose(0, 2, 1, 3).reshape(nb, 1, ns)

    bpt = max(1, min(bz, S5_ROWS // nct))
    assert bz % bpt == 0
    rt = bz // bpt
    rows = bpt * nct
    h0 = jnp.stack([h0_re, h0_im], axis=0).reshape(2, rt, bpt, nb, gb, p)
    h0 = h0.transpose(3, 1, 2, 0, 4, 5).reshape(nb, rt, bpt, ns)

    y, hn = pl.pallas_call(
        functools.partial(_s5_main_kernel, chunk=chunk, nct=nct, bpt=bpt),
        grid=(nb, rt),
        in_specs=[pl.BlockSpec((rows * chunk, LANES), lambda i, r: (r, i)),
                  pl.BlockSpec((None, nw, nw + ns), lambda i, r: (i, 0, 0)),
                  pl.BlockSpec((None, ns, nw), lambda i, r: (i, 0, 0)),
                  pl.BlockSpec((None, 1, ns), lambda i, r: (i, 0, 0)),
                  pl.BlockSpec((None, None, bpt, ns), lambda i, r: (i, r, 0, 0)),
                  pl.BlockSpec((None, 1, LANES), lambda i, r: (i, 0, 0))],
        out_specs=[pl.BlockSpec((rows * chunk, LANES), lambda i, r: (r, i)),
                   pl.BlockSpec((None, None, bpt, ns), lambda i, r: (i, r, 0, 0))],
        out_shape=[jax.ShapeDtypeStruct((bz * t, a), F32),
                   jax.ShapeDtypeStruct((nb, rt, bpt, ns), F32)],
        scratch_shapes=[pltpu.VMEM((rows, ns), F32)] * 2,
        compiler_params=_params("parallel", "arbitrary"),
        name="s5_main",
    )(proj, wp, qbig, al_big, h0, d_skip.reshape(nb, 1, LANES))

    hn = hn.reshape(nb, bz, 2, gb, p).transpose(2, 1, 0, 3, 4).reshape(2, bz, g, p)
    return y, hn[0], hn[1]


def _conv_kernel(v_ref, g_ref, buf_ref, w_ref, b_ref, lg_ref, lb_ref, y_ref, nbuf_ref, win_ref, acc_ref,
                 *, tt, width, nt):
    i = pl.program_id(1)
    keep = width - 1
    off = CONV_HALO - keep

    @pl.when(i == 0)
    def _():
        win_ref[0:off, :] = jnp.zeros((off, win_ref.shape[1]), F32)
        win_ref[off:CONV_HALO, :] = buf_ref[...]

    v = v_ref[...]
    win_ref[CONV_HALO:CONV_HALO + tt, :] = v * jax.nn.sigmoid(g_ref[...])

    bw = win_ref.shape[1]
    rb = min(tt, 64)
    for r0 in range(0, tt, rb):
        for c0 in range(0, bw, 128):
            acc = jnp.zeros((rb, 128), F32)
            for k in range(width):
                acc = acc + w_ref[k:k + 1, c0:c0 + 128] * win_ref[off + r0 + k:off + r0 + k + rb, c0:c0 + 128]
            acc_ref[r0:r0 + rb, c0:c0 + 128] = acc

    y = acc_ref[...] + b_ref[...]
    mu = jnp.mean(y, axis=-1, keepdims=True)
    var = jnp.mean(jnp.square(y - mu), axis=-1, keepdims=True)
    z = (y - mu) * lax.rsqrt(var + EPS) * lg_ref[...] + lb_ref[...]
    y_ref[...] = (z * jax.nn.sigmoid(z)).astype(y_ref.dtype)

    @pl.when(i == nt - 1)
    def _():
        nbuf_ref[...] = win_ref[tt + off:tt + CONV_HALO, :]

    tail = win_ref[tt:tt + CONV_HALO, :]
    win_ref[0:CONV_HALO, :] = tail


def conv_module(proj, buf, conv_w, conv_b, ln_g, ln_b, bz, t, a_width):
    width, bw = conv_w.shape
    assert a_width % bw == 0 and width - 1 <= CONV_HALO
    tt = _tile(t, 128)
    nt = t // tt
    vcol = a_width // bw
    proj = proj.reshape(bz, t, proj.shape[-1])
    y, nbuf = pl.pallas_call(
        functools.partial(_conv_kernel, tt=tt, width=width, nt=nt),
        grid=(bz, nt),
        in_specs=[pl.BlockSpec((None, tt, bw), lambda b, i: (b, i, vcol)),
                  pl.BlockSpec((None, tt, bw), lambda b, i: (b, i, vcol + 1)),
                  pl.BlockSpec((None, width - 1, bw), lambda b, i: (b, 0, 0)),
                  pl.BlockSpec((width, bw), lambda b, i: (0, 0)),
                  pl.BlockSpec((1, bw), lambda b, i: (0, 0)),
                  pl.BlockSpec((1, bw), lambda b, i: (0, 0)),
                  pl.BlockSpec((1, bw), lambda b, i: (0, 0))],
        out_specs=[pl.BlockSpec((None, tt, bw), lambda b, i: (b, i, 0)),
                   pl.BlockSpec((None, width - 1, bw), lambda b, i: (b, 0, 0))],
        out_shape=[jax.ShapeDtypeStruct((bz, t, bw), BF16),
                   jax.ShapeDtypeStruct((bz, width - 1, bw), F32)],
        scratch_shapes=[pltpu.VMEM((CONV_HALO + tt, bw), F32), pltpu.VMEM((tt, bw), F32)],
        compiler_params=_params("parallel", "arbitrary"),
        name="conv_module",
    )(proj, proj, buf, conv_w, conv_b.reshape(1, bw), ln_g.reshape(1, bw), ln_b.reshape(1, bw))
    return y.reshape(bz * t, bw), nbuf


def _diff_lambda(lam_ref, lam_init):
    s1 = jnp.sum(lam_ref[0:1, :] * lam_ref[1:2, :], axis=-1, keepdims=True)
    s2 = jnp.sum(lam_ref[2:3, :] * lam_ref[3:4, :], axis=-1, keepdims=True)
    return jnp.exp(s1) - jnp.exp(s2) + lam_init


def _flash_kernel(qi_ref, ki_ref, lam_ref, g_ref, q_ref, k_ref, v_ref, o_ref, qs_ref, m_ref, l_ref, acc_ref,
                  *, tq, hd, hps, lam_init):
    pair = pl.program_id(2)
    qi, ki = qi_ref[pair], ki_ref[pair]
    rc = min(tq, FLASH_ROWS)

    @pl.when(ki == 0)
    def _():
        for hh in range(hps):
            q = q_ref[:, hh * LANES:(hh + 1) * LANES]
            lane = lax.broadcasted_iota(jnp.int32, q.shape, 1)
            zero = jnp.zeros_like(q)
            qs_ref[hh, 0:tq, :] = jnp.where(lane < hd, q, zero)
            qs_ref[hh, tq:2 * tq, :] = jnp.where(lane >= hd, q, zero)
        m_ref[...] = jnp.full(m_ref.shape, NEG_INF, F32)
        l_ref[...] = jnp.zeros(l_ref.shape, F32)
        acc_ref[...] = jnp.zeros(acc_ref.shape, F32)

    def step(masked):
        items = [(hh, c) for hh in range(hps) for c in range(2 * tq // rc)]

        def scores(hh, c):
            q0 = (c * rc) % tq
            kc = q0 + rc if masked else tq
            return lax.dot_general(qs_ref[hh, c * rc:(c + 1) * rc, :],
                                   k_ref[0:kc, hh * LANES:(hh + 1) * LANES],
                                   (((1,), (1,)), ((), ())), preferred_element_type=F32)

        s_next = scores(*items[0])
        for n, (hh, c) in enumerate(items):
            rows = slice(c * rc, (c + 1) * rc)
            q0 = (c * rc) % tq
            kc = q0 + rc if masked else tq
            s = s_next
            if n + 1 < len(items):
                s_next = scores(*items[n + 1])
            if masked:
                row = lax.broadcasted_iota(jnp.int32, s.shape, 0) + q0
                col = lax.broadcasted_iota(jnp.int32, s.shape, 1)
                s = jnp.where(col <= row, s, NEG_INF)
            cols = [s[:, j:j + LANES] for j in range(0, kc, LANES)]
            m_old = m_ref[hh, rows, :]
            m_new = jnp.maximum(m_old, jnp.max(functools.reduce(jnp.maximum, cols), axis=-1, keepdims=True))
            alpha = jnp.exp(m_old - m_new)
            ps = [jnp.exp(cj - m_new) for cj in cols]
            l_ref[hh, rows, :] = alpha * l_ref[hh, rows, :] + jnp.sum(functools.reduce(jnp.add, ps), axis=-1,
                                                                       keepdims=True)
            p = jnp.concatenate(ps, axis=1) if len(ps) > 1 else ps[0]
            pv = jnp.dot(p.astype(BF16), v_ref[0:kc, hh * LANES:(hh + 1) * LANES],
                         preferred_element_type=F32)
            acc_ref[hh, rows, :] = alpha * acc_ref[hh, rows, :] + pv
            m_ref[hh, rows, :] = m_new

    @pl.when(ki < qi)
    def _():
        step(False)

    @pl.when(ki == qi)
    def _():
        step(True)
        lam = _diff_lambda(lam_ref, lam_init)
        for hh in range(hps):
            o = acc_ref[hh] / l_ref[hh]
            d = o[0:tq] - lam * o[tq:2 * tq]
            o_ref[:, hh * LANES:(hh + 1) * LANES] = (_rms_rows(d, g_ref[...])
                                                     * (1.0 - lam_init)).astype(o_ref.dtype)


def prompt_attention(q, k, v, lam_vecs, subln_g, bz, s, hd, lam_init):
    t, qw = q.shape
    vd = 2 * hd
    nh = qw // vd
    tq = _tile(s, 512)
    nq = s // tq
    assert vd == LANES and tq % min(tq, FLASH_ROWS) == 0 and min(tq, FLASH_ROWS) % LANES == 0
    hps = 2 if nh % 2 == 0 else 1
    bw = hps * vd
    pairs = [(a, c) for a in range(nq) for c in range(a + 1)]
    qi_tab = jnp.asarray([a for a, _ in pairs], jnp.int32)
    ki_tab = jnp.asarray([c for _, c in pairs], jnp.int32)
    grid_spec = pltpu.PrefetchScalarGridSpec(
        num_scalar_prefetch=2,
        grid=(bz, nh // hps, len(pairs)),
        in_specs=[pl.BlockSpec((4, hd), lambda b, h, p, qt, kt: (0, 0)),
                  pl.BlockSpec((1, vd), lambda b, h, p, qt, kt: (0, 0)),
                  pl.BlockSpec((tq, bw), lambda b, h, p, qt, kt: (b * nq + qt[p], h)),
                  pl.BlockSpec((tq, bw), lambda b, h, p, qt, kt: (b * nq + kt[p], h)),
                  pl.BlockSpec((tq, bw), lambda b, h, p, qt, kt: (b * nq + kt[p], h))],
        out_specs=pl.BlockSpec((tq, bw), lambda b, h, p, qt, kt: (b * nq + qt[p], h)),
        scratch_shapes=[pltpu.VMEM((hps, 2 * tq, vd), BF16), pltpu.VMEM((hps, 2 * tq, LANES), F32),
                        pltpu.VMEM((hps, 2 * tq, LANES), F32), pltpu.VMEM((hps, 2 * tq, vd), F32)],
    )
    return pl.pallas_call(
        functools.partial(_flash_kernel, tq=tq, hd=hd, hps=hps, lam_init=lam_init),
        grid_spec=grid_spec,
        out_shape=jax.ShapeDtypeStruct((t, nh * vd), BF16),
        compiler_params=_params("parallel", "parallel", "arbitrary"),
        name="prompt_attention",
    )(qi_tab, ki_tab, lam_vecs, subln_g.reshape(1, vd), q, k, v)


def _paged_kernel(pt_ref, lam_ref, g_ref, qidx_ref, q_ref, e_ref, msk_ref, *refs,
                  n_steps, pp, rpb, lam_init):
    k_refs, v_refs = refs[:pp], refs[pp:2 * pp]
    kn_ref, vn_ref, o_ref, m_ref, l_ref, acc_ref = refs[2 * pp:]
    p = pl.program_id(1)

    @pl.when(p == 0)
    def _():
        m_ref[...] = jnp.full(m_ref.shape, NEG_INF, F32)
        l_ref[...] = jnp.zeros(l_ref.shape, F32)
        acc_ref[...] = jnp.zeros(acc_ref.shape, F32)

    def update(kts, vfs, masked):
        q = q_ref[...]
        ss = [jnp.dot(q, kt[...].astype(BF16), preferred_element_type=F32) for kt in kts]
        if masked:
            col = lax.broadcasted_iota(jnp.int32, ss[0].shape, 1)
            ss = [jnp.where(col <= qidx_ref[...], s, NEG_INF) for s in ss]
        m_old = m_ref[...]
        m_new = jnp.maximum(m_old, jnp.max(functools.reduce(jnp.maximum, ss), axis=-1, keepdims=True))
        alpha = jnp.exp(m_old - m_new)
        prs = [jnp.exp(s - m_new) for s in ss]
        l_ref[...] = alpha * l_ref[...] + jnp.sum(functools.reduce(jnp.add, prs), axis=-1, keepdims=True)
        pv = None
        for pr, vf in zip(prs, vfs):
            pe = jnp.dot(pr.astype(BF16), e_ref[...], preferred_element_type=F32)
            pe = pe.astype(BF16) * msk_ref[...]
            part = jnp.dot(pe, vf[...].astype(BF16), preferred_element_type=F32)
            pv = part if pv is None else pv + part
        acc_ref[...] = alpha * acc_ref[...] + pv
        m_ref[...] = m_new

    @pl.when(p < n_steps)
    def _():
        for c0 in range(0, pp, PAGE_GROUP):
            update(k_refs[c0:c0 + PAGE_GROUP], v_refs[c0:c0 + PAGE_GROUP], False)

    @pl.when(p == n_steps)
    def _():
        update([kn_ref], [vn_ref], True)
        o = acc_ref[...] / l_ref[...]
        lam = _diff_lambda(lam_ref, lam_init)
        r = o.shape[0]
        d = o - lam * pltpu.roll(o, r - rpb // 2, axis=0)
        o_ref[...] = _rms_rows(d, g_ref[...]) * (1.0 - lam_init)


def sample_attention(q, k_new, v_new, cache_k, cache_v, layer, page_table, lam_vecs, subln_g,
                     bz, nq, hd, lam_init):
    vd = 2 * hd
    qw = q.shape[1]
    nh = qw // vd
    n_odd, n_pool, page = cache_k.shape[:3]
    n_pages = page_table.shape[1]
    assert nq <= page
    pp = next(c for c in (8, 4, 2, 1) if n_pages % c == 0 and n_pages // c >= min(2, n_pages))
    n_steps = n_pages // pp
    rpb = 2 * nq
    r = nh * rpb
    ckt = cache_k.transpose(0, 1, 3, 4, 2).reshape(n_odd, n_pool, qw, page)
    cvf = cache_v.reshape(n_odd, n_pool, page * nh, vd)
    q4 = q.reshape(bz, nq, nh, 2, hd).transpose(0, 2, 3, 1, 4)
    eye = jnp.eye(2 * nh, dtype=q.dtype).reshape(nh, 2, 1, 2 * nh, 1)
    qbd = (q4.reshape(bz, nh, 2, nq, 1, hd) * eye[None]).reshape(bz, r, qw)
    knt = jnp.pad(k_new.reshape(bz, nq, qw).transpose(0, 2, 1), ((0, 0), (0, 0), (0, page - nq)))
    vnf = jnp.pad(v_new.reshape(bz, nq * nh, vd), ((0, 0), (0, (page - nq) * nh), (0, 0)))
    qidx = (jnp.arange(r, dtype=jnp.int32) % nq).reshape(r, 1)
    expand = jnp.repeat(jnp.eye(page, dtype=BF16), nh, axis=1)
    own = (jnp.arange(page * nh)[None, :] % nh == jnp.arange(r)[:, None] // rpb).astype(BF16)
    pt = page_table.reshape(-1).astype(jnp.int32)

    def page_map(c):
        def index(b, p, pt):
            return (layer, pt[b * n_pages + jnp.minimum(p, n_steps - 1) * pp + c], 0, 0)
        return index

    grid_spec = pltpu.PrefetchScalarGridSpec(
        num_scalar_prefetch=1,
        grid=(bz, n_steps + 1),
        in_specs=([pl.BlockSpec((4, hd), lambda b, p, pt: (0, 0)),
                   pl.BlockSpec((1, vd), lambda b, p, pt: (0, 0)),
                   pl.BlockSpec((r, 1), lambda b, p, pt: (0, 0)),
                   pl.BlockSpec((None, r, qw), lambda b, p, pt: (b, 0, 0)),
                   pl.BlockSpec((page, page * nh), lambda b, p, pt: (0, 0)),
                   pl.BlockSpec((r, page * nh), lambda b, p, pt: (0, 0))]
                  + [pl.BlockSpec((None, None, qw, page), page_map(c)) for c in range(pp)]
                  + [pl.BlockSpec((None, None, page * nh, vd), page_map(c)) for c in range(pp)]
                  + [pl.BlockSpec((None, qw, page), lambda b, p, pt: (b, 0, 0)),
                     pl.BlockSpec((None, page * nh, vd), lambda b, p, pt: (b, 0, 0))]),
        out_specs=pl.BlockSpec((None, r, vd), lambda b, p, pt: (b, 0, 0)),
        scratch_shapes=[pltpu.VMEM((r, 1), F32), pltpu.VMEM((r, 1), F32), pltpu.VMEM((r, vd), F32)],
    )
    o = pl.pallas_call(
        functools.partial(_paged_kernel, n_steps=n_steps, pp=pp, rpb=rpb, lam_init=lam_init),
        grid_spec=grid_spec,
        out_shape=jax.ShapeDtypeStruct((bz, r, vd), F32),
        compiler_params=_params("parallel", "arbitrary"),
        name="sample_attention",
    )(pt, lam_vecs, subln_g.reshape(1, vd), qidx, qbd, expand, own,
      *([ckt] * pp), *([cvf] * pp), knt, vnf)
    o = o.reshape(bz, nh, 2, nq, vd)[:, :, 0]
    return o.transpose(0, 2, 1, 3).reshape(bz * nq, nh * vd).astype(BF16)


def kernel(x_prompt, x_sample, cache_k, cache_v, state_ssm_re, state_ssm_im, state_conv, page_table, norm_mix_pre, norm_mix_post, norm_ffn_pre, norm_ffn_post, w_in_even, ssm_lam_re, ssm_lam_im, ssm_log_dt, ssm_b_re, ssm_b_im, ssm_c_re, ssm_c_im, ssm_d, w_glu, conv_w, conv_b, conv_ln_g, conv_ln_b, w_out_even, w_qkv, lambda_q1, lambda_k1, lambda_q2, lambda_k2, subln_g, w_o, w_gate, w_up, w_down):
    depth = norm_mix_pre.shape[0]
    d_model = x_prompt.shape[-1]
    a_width = w_glu.shape[1]
    n_groups, n_state = ssm_lam_re.shape[1:]
    bw = conv_w.shape[2]
    page = cache_k.shape[2]
    head_dim = cache_k.shape[-1]
    q_width = cache_k.shape[-2] * head_dim
    past_len = page_table.shape[1] * page

    def per_layer_bf16(w):
        return [w[n].astype(BF16) for n in range(w.shape[0])]

    w_in_b, w_glu_b, w_out_b = per_layer_bf16(w_in_even), per_layer_bf16(w_glu), per_layer_bf16(w_out_even)
    w_qkv_b, w_o_b = per_layer_bf16(w_qkv), per_layer_bf16(w_o)
    w_gate_b, w_up_b, w_down_b = per_layer_bf16(w_gate), per_layer_bf16(w_up), per_layer_bf16(w_down)
    n_odd = w_qkv.shape[0]
    lam_vecs = jnp.stack([lambda_q1, lambda_k1, lambda_q2, lambda_k2], axis=1)

    def run_trunk(x, pos0, ssm_re0, ssm_im0, conv0, paged):
        bz, t, _ = x.shape
        h = x.reshape(bz * t, d_model)
        pos = jnp.tile(pos0 + jnp.arange(t, dtype=F32), bz).reshape(bz * t, 1)
        new_re, new_im, new_conv = [], [], []
        k_all = v_all = None
        for i in range(depth):
            j = i // 2
            if i % 2 == 0:
                proj = norm_matmul(h, norm_mix_pre[i], w_in_b[j])
                y_a, s_re, s_im = s5_mixer(
                    proj, bz, t, a_width, ssm_re0[j], ssm_im0[j],
                    ssm_lam_re[j], ssm_lam_im[j], ssm_log_dt[j], ssm_b_re[j], ssm_b_im[j],
                    ssm_c_re[j], ssm_c_im[j], ssm_d[j])
                y_a = s5_glu(y_a, w_glu_b[j])
                y_b, buf = conv_module(proj, conv0[j], conv_w[j], conv_b[j], conv_ln_g[j],
                                       conv_ln_b[j], bz, t, a_width)
                new_re.append(s_re)
                new_im.append(s_im)
                new_conv.append(buf)
                mix_in, w_mix = jnp.concatenate([y_a, y_b], axis=-1), w_out_b[j]
            else:
                lam_init = 0.8 - 0.6 * math.exp(-0.3 * i)
                q, k_all, v_all, kb, vb = qkv_rope(h, norm_mix_pre[i], w_qkv_b[j], pos, q_width, head_dim,
                                                   j, n_odd, t, k_all, v_all)
                if paged:
                    assert k_all.ndim == 3
                    mix_in = sample_attention(q, k_all[j], v_all[j], cache_k, cache_v, j, page_table,
                                              lam_vecs[j], subln_g[j], bz, t, head_dim, lam_init)
                else:
                    mix_in = prompt_attention(q, kb, vb, lam_vecs[j], subln_g[j], bz, t, head_dim,
                                              lam_init)
                w_mix = w_o_b[j]
            h = matmul_norm_residual(mix_in, w_mix, norm_mix_post[i], h)
            h = ffn(h, norm_ffn_pre[i], norm_ffn_post[i], w_gate_b[i], w_up_b[i], w_down_b[i])
        n_kh = q_width // head_dim
        if k_all.ndim == 4:
            new_k = k_all.reshape(n_odd, bz, n_kh, head_dim, t).transpose(0, 1, 4, 2, 3)
        else:
            new_k = k_all.reshape(n_odd, bz, t, n_kh, head_dim)
        new_v = v_all.reshape(n_odd, bz, t, -1, 2 * head_dim)
        return (h.reshape(bz, t, d_model), jnp.stack(new_re), jnp.stack(new_im), jnp.stack(new_conv),
                new_k, new_v)

    n_prompt = x_prompt.shape[0]
    n_even = state_ssm_re.shape[0]
    zero_ssm = jnp.zeros((n_even, n_prompt, n_groups, n_state), F32)
    zero_conv = jnp.zeros((n_even, n_prompt, conv_w.shape[1] - 1, bw), F32)
    y_p, re_p, im_p, conv_p, k_p, v_p = run_trunk(x_prompt, 0.0, zero_ssm, zero_ssm, zero_conv, False)
    y_s, re_s, im_s, conv_s, k_s, v_s = run_trunk(x_sample, float(past_len), state_ssm_re, state_ssm_im,
                                                  state_conv, True)
    return (y_p, y_s, re_p, im_p, conv_p, k_p, v_p, re_s, im_s, conv_s, k_s, v_s)
```

```python
import functools
import math

import jax
import jax.numpy as jnp
from jax import lax
from jax.experimental import pallas as pl
from jax.experimental.pallas import tpu as pltpu

F32 = jnp.float32
BF16 = jnp.bfloat16
EPS = 1e-6
NEG_INF = -1e30
ROPE_THETA = 500000.0
S5_CHUNK = 16
S5_ROWS = 256
CONV_HALO = 32
LANES = 128
FLASH_ROWS = 256
PAGE_GROUP = 4
VMEM_LIMIT = 56 * 1024 * 1024
HI = lax.Precision.HIGHEST


def _tile(n, pref):
    if n <= pref:
        return n
    t = pref
    while t >= 8:
        if n % t == 0:
            return t
        t //= 2
    return n


def _params(*sem):
    return pltpu.CompilerParams(dimension_semantics=sem, vmem_limit_bytes=VMEM_LIMIT)


def _rms_rows(x, g, eps=EPS):
    ms = jnp.mean(x * x, axis=-1, keepdims=True)
    return x * lax.rsqrt(ms + eps) * g


def _norm_matmul_kernel(x_ref, g_ref, w_ref, o_ref, hn_ref):
    @pl.when(pl.program_id(1) == 0)
    def _():
        hn_ref[...] = _rms_rows(x_ref[...], g_ref[...]).astype(BF16)

    o_ref[...] = jnp.dot(hn_ref[...], w_ref[...], preferred_element_type=F32)


def norm_matmul(x, g, w, layer):
    t, d = x.shape
    n = w.shape[2]
    tm, tn = _tile(t, 1024), _tile(n, 512)
    return pl.pallas_call(
        _norm_matmul_kernel,
        grid=(t // tm, n // tn),
        in_specs=[pl.BlockSpec((tm, d), lambda i, j: (i, 0)),
                  pl.BlockSpec((1, d), lambda i, j: (0, 0)),
                  pl.BlockSpec((None, d, tn), lambda i, j: (layer, 0, j))],
        out_specs=pl.BlockSpec((tm, tn), lambda i, j: (i, j)),
        out_shape=jax.ShapeDtypeStruct((t, n), F32),
        scratch_shapes=[pltpu.VMEM((tm, d), BF16)],
        compiler_params=_params("parallel", "arbitrary"),
        name="norm_matmul",
    )(x, g.reshape(1, d), w)


def _qkv_kernel(x_ref, g_ref, pos_ref, invf_ref, sel_ref, w_ref, *rest, nq_tiles, scale, n_prev, k_major):
    q_ref, k_ref, v_ref, kb_ref, vb_ref, hn_ref, tab_ref = rest[n_prev:]
    j = pl.program_id(1)

    @pl.when(j == 0)
    def _():
        hn_ref[...] = _rms_rows(x_ref[...], g_ref[...]).astype(BF16)
        ang = pos_ref[...] * invf_ref[...]
        c, s = jnp.cos(ang), jnp.sin(ang)
        tab_ref[0] = jnp.where(sel_ref[0:1, :] > 0.5, c, 1.0)
        tab_ref[1] = s * sel_ref[1:2, :]
        tab_ref[2] = s * sel_ref[2:3, :]

    tn = w_ref.shape[1]
    cw = min(tn, 2 * LANES)
    half = LANES // 16

    def rope(y):
        parts = []
        for c0 in range(0, y.shape[1], LANES):
            x = y[:, c0:c0 + LANES]
            parts.append(x * tab_ref[0] + pltpu.roll(x, half, axis=1) * tab_ref[1]
                         + pltpu.roll(x, LANES - half, axis=1) * tab_ref[2])
        return jnp.concatenate(parts, axis=1) if len(parts) > 1 else parts[0]

    def sweep(emit):
        hn = hn_ref[...]
        chunks = list(range(0, tn, cw))
        y_next = jnp.dot(hn, w_ref[:, 0:cw], preferred_element_type=F32)
        for n, c0 in enumerate(chunks):
            y = y_next
            if n + 1 < len(chunks):
                y_next = jnp.dot(hn, w_ref[:, c0 + cw:c0 + 2 * cw], preferred_element_type=F32)
            emit(slice(c0, c0 + cw), y)

    @pl.when(j < nq_tiles)
    def _():
        def emit(cols, y):
            q_ref[:, cols] = (rope(y) * scale).astype(BF16)
        sweep(emit)

    @pl.when(jnp.logical_and(j >= nq_tiles, j < 2 * nq_tiles))
    def _():
        def emit(cols, y):
            r = rope(y)
            if k_major:
                k_ref[cols, :] = r.T
            else:
                k_ref[:, cols] = r
            kb_ref[:, cols] = r.astype(BF16)
        sweep(emit)

    @pl.when(j >= 2 * nq_tiles)
    def _():
        def emit(cols, y):
            v_ref[:, cols] = y
            vb_ref[:, cols] = y.astype(BF16)
        sweep(emit)


def qkv_rope(x, g, w, pos, q_width, head_dim, layer, n_layers, seq, k_all=None, v_all=None):
    t, d = x.shape
    n = w.shape[2]
    v_width = n - 2 * q_width
    rot = head_dim // 4
    assert rot == 16 and 128 % head_dim == 0
    tm = _tile(t, 512)
    tn = _tile(math.gcd(q_width, v_width), 1024)
    nq, nv = q_width // tn, v_width // tn
    k_major = seq % tm == 0 and tm % LANES == 0
    tps = seq // tm if k_major else 1
    prev = [] if k_all is None else [k_all, v_all]
    lane = jnp.arange(128) % head_dim
    inv_freq = ROPE_THETA ** (-jnp.arange(rot // 2, dtype=F32) * 2.0 / rot)
    invf = jnp.where(lane < rot, inv_freq[lane % (rot // 2)], 0.0).reshape(1, 128).astype(F32)
    sel = jnp.stack([(lane < rot).astype(F32),
                     jnp.logical_and(lane >= rot // 2, lane < rot).astype(F32),
                     -(lane < rot // 2).astype(F32)])

    def clip(j, lo, cnt):
        return jnp.clip(j - lo, 0, cnt - 1)

    if k_major:
        k_spec = pl.BlockSpec((None, None, tn, tm), lambda i, j: (layer, i // tps, clip(j, nq, nq), i % tps))
        k_shape = jax.ShapeDtypeStruct((n_layers, t // seq, q_width, seq), F32)
    else:
        k_spec = pl.BlockSpec((None, tm, tn), lambda i, j: (layer, i, clip(j, nq, nq)))
        k_shape = jax.ShapeDtypeStruct((n_layers, t, q_width), F32)
    outs = pl.pallas_call(
        functools.partial(_qkv_kernel, nq_tiles=nq, scale=head_dim ** -0.5, n_prev=len(prev), k_major=k_major),
        grid=(t // tm, 2 * nq + nv),
        in_specs=[pl.BlockSpec((tm, d), lambda i, j: (i, 0)),
                  pl.BlockSpec((1, d), lambda i, j: (0, 0)),
                  pl.BlockSpec((tm, 1), lambda i, j: (i, 0)),
                  pl.BlockSpec((1, 128), lambda i, j: (0, 0)),
                  pl.BlockSpec((3, 128), lambda i, j: (0, 0)),
                  pl.BlockSpec((None, d, tn), lambda i, j: (layer, 0, j))]
                 + [pl.BlockSpec(memory_space=pl.ANY)] * len(prev),
        out_specs=[pl.BlockSpec((tm, tn), lambda i, j: (i, clip(j, 0, nq))),
                   k_spec,
                   pl.BlockSpec((None, tm, tn), lambda i, j: (layer, i, clip(j, 2 * nq, nv))),
                   pl.BlockSpec((tm, tn), lambda i, j: (i, clip(j, nq, nq))),
                   pl.BlockSpec((tm, tn), lambda i, j: (i, clip(j, 2 * nq, nv)))],
        out_shape=[jax.ShapeDtypeStruct((t, q_width), BF16),
                   k_shape,
                   jax.ShapeDtypeStruct((n_layers, t, v_width), F32),
                   jax.ShapeDtypeStruct((t, q_width), BF16),
                   jax.ShapeDtypeStruct((t, v_width), BF16)],
        scratch_shapes=[pltpu.VMEM((tm, d), BF16), pltpu.VMEM((3, tm, 128), F32)],
        input_output_aliases={6 + n: 1 + n for n in range(len(prev))},
        compiler_params=_params("parallel", "arbitrary"),
        name="qkv_rope",
    )(x, g.reshape(1, d), pos, invf, sel, w, *prev)
    return outs


def _matmul_norm_res_kernel(a_ref, w_ref, g_ref, h_ref, o_ref, acc_ref, *, nj, tn):
    j = pl.program_id(1)
    acc_ref[j] = jnp.dot(a_ref[...], w_ref[...], preferred_element_type=F32)

    @pl.when(j == nj - 1)
    def _():
        ss = jnp.zeros((acc_ref.shape[1], 1), F32)
        for jj in range(nj):
            y = acc_ref[jj]
            ss = ss + jnp.sum(y * y, axis=-1, keepdims=True)
        inv = lax.rsqrt(ss / (nj * tn) + EPS)
        for jj in range(nj):
            sl = slice(jj * tn, (jj + 1) * tn)
            o_ref[:, sl] = h_ref[:, sl] + acc_ref[jj] * inv * g_ref[:, sl]


def matmul_norm_residual(a, w, layer, g, h):
    t, k = a.shape
    d = w.shape[2]
    tm, tn = _tile(t, 512), _tile(d, 512)
    nj = d // tn
    return pl.pallas_call(
        functools.partial(_matmul_norm_res_kernel, nj=nj, tn=tn),
        grid=(t // tm, nj),
        in_specs=[pl.BlockSpec((tm, k), lambda i, j: (i, 0)),
                  pl.BlockSpec((None, k, tn), lambda i, j: (layer, 0, j)),
                  pl.BlockSpec((1, d), lambda i, j: (0, 0)),
                  pl.BlockSpec((tm, d), lambda i, j: (i, 0))],
        out_specs=pl.BlockSpec((tm, d), lambda i, j: (i, 0)),
        out_shape=jax.ShapeDtypeStruct((t, d), F32),
        scratch_shapes=[pltpu.VMEM((nj, tm, tn), F32)],
        compiler_params=_params("parallel", "arbitrary"),
        name="matmul_norm_residual",
    )(a, w, g.reshape(1, d), h)


def _ffn_kernel(h_ref, gpre_ref, gpost_ref, wg_ref, wu_ref, wd_ref, o_ref, hn_ref, *, nf):
    f = pl.program_id(1)

    @pl.when(f == 0)
    def _():
        hn_ref[...] = _rms_rows(h_ref[...], gpre_ref[...]).astype(BF16)
        o_ref[...] = jnp.zeros(o_ref.shape, F32)

    hn = hn_ref[...]
    gate = jnp.dot(hn, wg_ref[...], preferred_element_type=F32)
    up = jnp.dot(hn, wu_ref[...], preferred_element_type=F32)
    act = (gate * jax.nn.sigmoid(gate) * up).astype(BF16)
    o_ref[...] += jnp.dot(act, wd_ref[...], preferred_element_type=F32)

    @pl.when(f == nf - 1)
    def _():
        o_ref[...] = h_ref[...] + _rms_rows(o_ref[...], gpost_ref[...])


def ffn(h, g_pre, g_post, wg, wu, wd, layer):
    t, d = h.shape
    fh = wg.shape[2]
    tm = _tile(t, 1024)
    tf = next((c for c in (512, 256) if fh % c == 0), fh)
    nf = fh // tf
    return pl.pallas_call(
        functools.partial(_ffn_kernel, nf=nf),
        grid=(t // tm, nf),
        in_specs=[pl.BlockSpec((tm, d), lambda i, f: (i, 0), pipeline_mode=pl.Buffered(1)),
                  pl.BlockSpec((1, d), lambda i, f: (0, 0)),
                  pl.BlockSpec((1, d), lambda i, f: (0, 0)),
                  pl.BlockSpec((None, d, tf), lambda i, f: (layer, 0, f)),
                  pl.BlockSpec((None, d, tf), lambda i, f: (layer, 0, f)),
                  pl.BlockSpec((None, tf, d), lambda i, f: (layer, f, 0))],
        out_specs=pl.BlockSpec((tm, d), lambda i, f: (i, 0)),
        out_shape=jax.ShapeDtypeStruct((t, d), F32),
        scratch_shapes=[pltpu.VMEM((tm, d), BF16)],
        compiler_params=_params("parallel", "arbitrary"),
        name="ffn",
    )(h, g_pre.reshape(1, d), g_post.reshape(1, d), wg, wu, wd)


def _glu_kernel(y_ref, w_ref, o_ref):
    y = y_ref[...]
    z = jnp.dot(y.astype(BF16), w_ref[...], preferred_element_type=F32)
    o_ref[...] = (y * jax.nn.sigmoid(z)).astype(o_ref.dtype)


def s5_glu(y, w, layer):
    t, a = y.shape
    tm = _tile(t, 1024)
    return pl.pallas_call(
        _glu_kernel,
        grid=(t // tm,),
        in_specs=[pl.BlockSpec((tm, a), lambda i: (i, 0)),
                  pl.BlockSpec((None, a, a), lambda i: (layer, 0, 0))],
        out_specs=pl.BlockSpec((tm, a), lambda i: (i, 0)),
        out_shape=jax.ShapeDtypeStruct((t, a), BF16),
        compiler_params=_params("parallel"),
        name="s5_glu",
    )(y, w)


def _s5_prep_kernel(lam_row_ref, lam_col_ref, bt_ref, ct_ref, e_ref, tt_ref,
                    g0_ref, p_ref, q_ref, al_ref, *, chunk, gq):
    for gi in range(gq):
        _s5_prep_group(lam_row_ref.at[gi], lam_col_ref.at[gi], bt_ref.at[gi], ct_ref.at[gi], e_ref, tt_ref,
                       g0_ref.at[gi], p_ref.at[gi], q_ref.at[gi], al_ref.at[gi], chunk=chunk)


def _s5_prep_group(lam_row_ref, lam_col_ref, bt_ref, ct_ref, e_ref, tt_ref,
                   g0_ref, p_ref, q_ref, al_ref, *, chunk):
    lr, li, ldt = lam_row_ref[0:1, :], lam_row_ref[1:2, :], lam_row_ref[2:3, :]
    dt = jnp.exp(ldt)
    mag = jnp.exp(lr * dt)
    ab_re, ab_im = mag * jnp.cos(li * dt), mag * jnp.sin(li * dt)
    den = lr * lr + li * li
    num_re = ab_re - 1.0
    coef_re = (num_re * lr + ab_im * li) / den
    coef_im = (ab_im * lr - num_re * li) / den
    bt_re, bt_im = bt_ref[0], bt_ref[1]
    bb_re = coef_re * bt_re - coef_im * bt_im
    bb_im = coef_re * bt_im + coef_im * bt_re

    magl = jnp.exp(lr * dt * chunk)
    al_ref[0:1, :] = magl * jnp.cos(li * dt * chunk)
    al_ref[1:2, :] = magl * jnp.sin(li * dt * chunk)

    e = e_ref[...]
    pm = jnp.exp(lr * dt * e)
    pw_re, pw_im = pm * jnp.cos(li * dt * e), pm * jnp.sin(li * dt * e)
    bbt_re = jnp.concatenate([bb_re] * chunk, axis=0)
    bbt_im = jnp.concatenate([bb_im] * chunk, axis=0)
    n_state = pw_re.shape[1]
    p_ref[:, 0:n_state] = pw_re * bbt_re - pw_im * bbt_im
    p_ref[:, n_state:2 * n_state] = pw_re * bbt_im + pw_im * bbt_re

    lrc, lic, ldtc = lam_col_ref[0], lam_col_ref[1], lam_col_ref[2]
    dtc = jnp.exp(ldtc)
    tt = tt_ref[...]
    fm = jnp.exp(lrc * dtc * tt)
    fw_re, fw_im = fm * jnp.cos(lic * dtc * tt), fm * jnp.sin(lic * dtc * tt)
    ct_re, ct_im = ct_ref[0], ct_ref[1]
    f_re = ct_re * fw_re - ct_im * fw_im
    f_im = ct_re * fw_im + ct_im * fw_re
    g0_ref[...] = (jnp.dot(bb_re, f_re, precision=HI, preferred_element_type=F32)
                   - jnp.dot(bb_im, f_im, precision=HI, preferred_element_type=F32))
    magc = jnp.exp(lrc * dtc)
    abc_re, abc_im = magc * jnp.cos(lic * dtc), magc * jnp.sin(lic * dtc)
    q_ref[0] = (f_re * abc_re - f_im * abc_im).astype(q_ref.dtype)
    q_ref[1] = (-(f_re * abc_im + f_im * abc_re)).astype(q_ref.dtype)


def _s5_assemble_kernel(a_ref, p_ref, q_ref, tw_ref, tp_ref, rgw_ref, rgs_ref, cgw_ref, cgs_ref,
                        wp_ref, qb_ref, *, nw, ns):
    a = a_ref[...].astype(BF16)
    pm = p_ref[...].astype(BF16)
    q = q_ref[...]
    cs = min(nw, 512)
    for c0 in range(0, nw, cs):
        cols = slice(c0, c0 + cs)
        w = jnp.dot(a, tw_ref[:, cols], preferred_element_type=F32)
        wp_ref[:, cols] = jnp.where(rgw_ref[...] == cgw_ref[:, cols], w, 0.0).astype(BF16)
        qv = jnp.dot(q, tw_ref[:, cols], preferred_element_type=F32)
        qb_ref[:, cols] = jnp.where(rgs_ref[...] == cgw_ref[:, cols], qv, 0.0).astype(BF16)
    cs = min(ns, 512)
    for c0 in range(0, ns, cs):
        cols = slice(c0, c0 + cs)
        pv = jnp.dot(pm, tp_ref[:, cols], preferred_element_type=F32)
        wp_ref[:, nw + c0:nw + c0 + cs] = jnp.where(rgw_ref[...] == cgs_ref[:, cols], pv, 0.0).astype(BF16)


def _s5_main_kernel(u_ref, w_ref, q_ref, al_ref, h0_ref, d_ref,
                    y_ref, hn_ref, bc_ref, xs_ref, *, chunk, nct, bpt):
    rows = bc_ref.shape[0]
    nw = chunk * LANES
    half = bc_ref.shape[1] // 2
    us = [u_ref[pl.ds(s, rows, stride=chunk), :] for s in range(chunk)]
    ucat = jnp.concatenate(us, axis=1).astype(BF16)
    r = jnp.dot(ucat, w_ref[...], preferred_element_type=F32)
    bc_ref[...] = r[:, nw:]
    a_re, a_im = al_ref[:, 0:half], al_ref[:, half:]

    def step(j, carry):
        nxt = []
        for bl in range(bpt):
            x_re, x_im = carry[2 * bl], carry[2 * bl + 1]
            row = pl.ds(bl * nct + j, 1)
            xs_ref[row, 0:half] = x_re
            xs_ref[row, half:] = x_im
            nxt.append(a_re * x_re - a_im * x_im + bc_ref[row, 0:half])
            nxt.append(a_re * x_im + a_im * x_re + bc_ref[row, half:])
        return tuple(nxt)

    init = tuple(h0_ref[bl:bl + 1, sl] for bl in range(bpt) for sl in (slice(0, half), slice(half, 2 * half)))
    fin = lax.fori_loop(0, nct, step, init)
    for bl in range(bpt):
        hn_ref[bl:bl + 1, 0:half] = fin[2 * bl]
        hn_ref[bl:bl + 1, half:] = fin[2 * bl + 1]
    yc = jnp.dot(xs_ref[...].astype(BF16), q_ref[...], preferred_element_type=F32)
    for t in range(chunk):
        sl = slice(t * LANES, (t + 1) * LANES)
        y_ref[pl.ds(t, rows, stride=chunk), :] = jax.nn.gelu(r[:, sl] + yc[:, sl] + d_ref[...] * us[t])


def s5_mixer(proj, bz, t, a, h0_re, h0_im, lam_re, lam_im, log_dt, b_re, b_im, c_re, c_im, d_skip):
    g, p = lam_re.shape
    gc = a // g
    chunk = S5_CHUNK if t % S5_CHUNK == 0 else t
    nct = t // chunk
    lc = chunk * gc

    lam_row = jnp.stack([lam_re, lam_im, jnp.broadcast_to(log_dt[:, None], (g, p))], axis=1)
    lam_col = jnp.broadcast_to(lam_row[:, :, :, None], (g, 3, p, lc))
    bt = jnp.stack([b_re, b_im], axis=1).transpose(0, 1, 3, 2)
    ct = jnp.tile(jnp.stack([c_re, c_im], axis=1).transpose(0, 1, 3, 2), (1, 1, 1, chunk))

    step = jnp.arange(lc, dtype=jnp.int32) // gc
    e_col = (chunk - 1 - step).astype(F32).reshape(lc, 1)
    tt_row = step.astype(F32).reshape(1, lc)

    gq = next(c for c in (8, 4, 2, 1) if g % c == 0)
    g0, pmat, qmat, al = pl.pallas_call(
        functools.partial(_s5_prep_kernel, chunk=chunk, gq=gq),
        grid=(g // gq,),
        in_specs=[pl.BlockSpec((gq, 3, p), lambda i: (i, 0, 0)),
                  pl.BlockSpec((gq, 3, p, lc), lambda i: (i, 0, 0, 0)),
                  pl.BlockSpec((gq, 2, gc, p), lambda i: (i, 0, 0, 0)),
                  pl.BlockSpec((gq, 2, p, lc), lambda i: (i, 0, 0, 0)),
                  pl.BlockSpec((lc, 1), lambda i: (0, 0)),
                  pl.BlockSpec((1, lc), lambda i: (0, 0))],
        out_specs=[pl.BlockSpec((gq, gc, lc), lambda i: (i, 0, 0)),
                   pl.BlockSpec((gq, lc, 2 * p), lambda i: (i, 0, 0)),
                   pl.BlockSpec((gq, 2, p, lc), lambda i: (i, 0, 0, 0)),
                   pl.BlockSpec((gq, 2, p), lambda i: (i, 0, 0))],
        out_shape=[jax.ShapeDtypeStruct((g, gc, lc), F32),
                   jax.ShapeDtypeStruct((g, lc, 2 * p), F32),
                   jax.ShapeDtypeStruct((g, 2, p, lc), BF16),
                   jax.ShapeDtypeStruct((g, 2, p), F32)],
        compiler_params=_params("parallel"),
        name="s5_prep",
    )(lam_row, lam_col, bt, ct, e_col, tt_row)

    g0p = jnp.pad(g0.reshape(g, gc, chunk, gc), ((0, 0), (0, 0), (chunk, 0), (0, 0)))
    wt = jnp.stack([g0p[:, :, chunk - s:2 * chunk - s, :] for s in range(chunk)], axis=1)

    assert LANES % gc == 0 and g % (LANES // gc) == 0
    gb = LANES // gc
    nb = g // gb
    nw, ns = chunk * LANES, 2 * gb * p
    a_rows = wt.reshape(nb, gb, chunk, gc, lc).transpose(0, 2, 1, 3, 4).reshape(nb, nw, lc)
    p_rows = pmat.reshape(nb, gb, chunk, gc, 2 * p).transpose(0, 2, 1, 3, 4).reshape(nb, nw, 2 * p)
    q_rows = qmat.reshape(nb, gb, 2, p, lc).transpose(0, 2, 1, 3, 4).reshape(nb, ns, lc)
    tile_w = jnp.kron(jnp.eye(chunk, dtype=BF16), jnp.tile(jnp.eye(gc, dtype=BF16), (1, gb)))
    tile_p = jnp.kron(jnp.eye(2, dtype=BF16), jnp.tile(jnp.eye(p, dtype=BF16), (1, gb)))
    grp_w = (jnp.arange(nw, dtype=jnp.int32) // gc) % gb
    grp_s = (jnp.arange(ns, dtype=jnp.int32) // p) % gb
    wp, qbig = pl.pallas_call(
        functools.partial(_s5_assemble_kernel, nw=nw, ns=ns),
        grid=(nb,),
        in_specs=[pl.BlockSpec((None, nw, lc), lambda i: (i, 0, 0)),
                  pl.BlockSpec((None, nw, 2 * p), lambda i: (i, 0, 0)),
                  pl.BlockSpec((None, ns, lc), lambda i: (i, 0, 0)),
                  pl.BlockSpec((lc, nw), lambda i: (0, 0)),
                  pl.BlockSpec((2 * p, ns), lambda i: (0, 0)),
                  pl.BlockSpec((nw, 1), lambda i: (0, 0)),
                  pl.BlockSpec((ns, 1), lambda i: (0, 0)),
                  pl.BlockSpec((1, nw), lambda i: (0, 0)),
                  pl.BlockSpec((1, ns), lambda i: (0, 0))],
        out_specs=[pl.BlockSpec((None, nw, nw + ns), lambda i: (i, 0, 0)),
                   pl.BlockSpec((None, ns, nw), lambda i: (i, 0, 0))],
        out_shape=[jax.ShapeDtypeStruct((nb, nw, nw + ns), BF16),
                   jax.ShapeDtypeStruct((nb, ns, nw), BF16)],
        compiler_params=_params("parallel"),
        name="s5_assemble",
    )(a_rows, p_rows, q_rows, tile_w, tile_p, grp_w.reshape(nw, 1), grp_s.reshape(ns, 1),
      grp_w.reshape(1, nw), grp_s.reshape(1, ns))
    al_big = al.reshape(nb, gb, 2, p).transpose(0, 2, 1, 3).reshape(nb, 1, ns)

    bpt = max(1, min(bz, S5_ROWS // nct))
    assert bz % bpt == 0
    rt = bz // bpt
    rows = bpt * nct
    h0 = jnp.stack([h0_re, h0_im], axis=0).reshape(2, rt, bpt, nb, gb, p)
    h0 = h0.transpose(3, 1, 2, 0, 4, 5).reshape(nb, rt, bpt, ns)

    y, hn = pl.pallas_call(
        functools.partial(_s5_main_kernel, chunk=chunk, nct=nct, bpt=bpt),
        grid=(nb, rt),
        in_specs=[pl.BlockSpec((rows * chunk, LANES), lambda i, r: (r, i)),
                  pl.BlockSpec((None, nw, nw + ns), lambda i, r: (i, 0, 0)),
                  pl.BlockSpec((None, ns, nw), lambda i, r: (i, 0, 0)),
                  pl.BlockSpec((None, 1, ns), lambda i, r: (i, 0, 0)),
                  pl.BlockSpec((None, None, bpt, ns), lambda i, r: (i, r, 0, 0)),
                  pl.BlockSpec((None, 1, LANES), lambda i, r: (i, 0, 0))],
        out_specs=[pl.BlockSpec((rows * chunk, LANES), lambda i, r: (r, i)),
                   pl.BlockSpec((None, None, bpt, ns), lambda i, r: (i, r, 0, 0))],
        out_shape=[jax.ShapeDtypeStruct((bz * t, a), F32),
                   jax.ShapeDtypeStruct((nb, rt, bpt, ns), F32)],
        scratch_shapes=[pltpu.VMEM((rows, ns), F32)] * 2,
        compiler_params=_params("parallel", "arbitrary"),
        name="s5_main",
    )(proj, wp, qbig, al_big, h0, d_skip.reshape(nb, 1, LANES))

    hn = hn.reshape(nb, bz, 2, gb, p).transpose(2, 1, 0, 3, 4).reshape(2, bz, g, p)
    return y, hn[0], hn[1]


def _conv_kernel(v_ref, g_ref, buf_ref, w_ref, b_ref, lg_ref, lb_ref, y_ref, nbuf_ref, win_ref, acc_ref,
                 *, tt, width, nt):
    i = pl.program_id(1)
    keep = width - 1
    off = CONV_HALO - keep

    @pl.when(i == 0)
    def _():
        win_ref[0:off, :] = jnp.zeros((off, win_ref.shape[1]), F32)
        win_ref[off:CONV_HALO, :] = buf_ref[...]

    v = v_ref[...]
    win_ref[CONV_HALO:CONV_HALO + tt, :] = v * jax.nn.sigmoid(g_ref[...])

    bw = win_ref.shape[1]
    rb = min(tt, 64)
    for r0 in range(0, tt, rb):
        for c0 in range(0, bw, 128):
            acc = jnp.zeros((rb, 128), F32)
            for k in range(width):
                acc = acc + w_ref[k:k + 1, c0:c0 + 128] * win_ref[off + r0 + k:off + r0 + k + rb, c0:c0 + 128]
            acc_ref[r0:r0 + rb, c0:c0 + 128] = acc

    y = acc_ref[...] + b_ref[...]
    mu = jnp.mean(y, axis=-1, keepdims=True)
    var = jnp.mean(jnp.square(y - mu), axis=-1, keepdims=True)
    z = (y - mu) * lax.rsqrt(var + EPS) * lg_ref[...] + lb_ref[...]
    y_ref[...] = (z * jax.nn.sigmoid(z)).astype(y_ref.dtype)

    @pl.when(i == nt - 1)
    def _():
        nbuf_ref[...] = win_ref[tt + off:tt + CONV_HALO, :]

    tail = win_ref[tt:tt + CONV_HALO, :]
    win_ref[0:CONV_HALO, :] = tail


def conv_module(proj, buf, conv_w, conv_b, ln_g, ln_b, bz, t, a_width):
    width, bw = conv_w.shape
    assert a_width % bw == 0 and width - 1 <= CONV_HALO
    tt = _tile(t, 128)
    nt = t // tt
    vcol = a_width // bw
    proj = proj.reshape(bz, t, proj.shape[-1])
    y, nbuf = pl.pallas_call(
        functools.partial(_conv_kernel, tt=tt, width=width, nt=nt),
        grid=(bz, nt),
        in_specs=[pl.BlockSpec((None, tt, bw), lambda b, i: (b, i, vcol)),
                  pl.BlockSpec((None, tt, bw), lambda b, i: (b, i, vcol + 1)),
                  pl.BlockSpec((None, width - 1, bw), lambda b, i: (b, 0, 0)),
                  pl.BlockSpec((width, bw), lambda b, i: (0, 0)),
                  pl.BlockSpec((1, bw), lambda b, i: (0, 0)),
                  pl.BlockSpec((1, bw), lambda b, i: (0, 0)),
                  pl.BlockSpec((1, bw), lambda b, i: (0, 0))],
        out_specs=[pl.BlockSpec((None, tt, bw), lambda b, i: (b, i, 0)),
                   pl.BlockSpec((None, width - 1, bw), lambda b, i: (b, 0, 0))],
        out_shape=[jax.ShapeDtypeStruct((bz, t, bw), BF16),
                   jax.ShapeDtypeStruct((bz, width - 1, bw), F32)],
        scratch_shapes=[pltpu.VMEM((CONV_HALO + tt, bw), F32), pltpu.VMEM((tt, bw), F32)],
        compiler_params=_params("parallel", "arbitrary"),
        name="conv_module",
    )(proj, proj, buf, conv_w, conv_b.reshape(1, bw), ln_g.reshape(1, bw), ln_b.reshape(1, bw))
    return y.reshape(bz * t, bw), nbuf


def _diff_lambda(lam_ref, lam_init):
    s1 = jnp.sum(lam_ref[0:1, :] * lam_ref[1:2, :], axis=-1, keepdims=True)
    s2 = jnp.sum(lam_ref[2:3, :] * lam_ref[3:4, :], axis=-1, keepdims=True)
    return jnp.exp(s1) - jnp.exp(s2) + lam_init


def _flash_kernel(qi_ref, ki_ref, lam_ref, g_ref, q_ref, k_ref, v_ref, o_ref, qs_ref, m_ref, l_ref, acc_ref,
                  *, tq, hd, hps, lam_init):
    pair = pl.program_id(2)
    qi, ki = qi_ref[pair], ki_ref[pair]
    rc = min(tq, FLASH_ROWS)

    @pl.when(ki == 0)
    def _():
        for hh in range(hps):
            q = q_ref[:, hh * LANES:(hh + 1) * LANES]
            lane = lax.broadcasted_iota(jnp.int32, q.shape, 1)
            zero = jnp.zeros_like(q)
            qs_ref[hh, 0:tq, :] = jnp.where(lane < hd, q, zero)
            qs_ref[hh, tq:2 * tq, :] = jnp.where(lane >= hd, q, zero)
        m_ref[...] = jnp.full(m_ref.shape, NEG_INF, F32)
        l_ref[...] = jnp.zeros(l_ref.shape, F32)
        acc_ref[...] = jnp.zeros(acc_ref.shape, F32)

    def step(masked):
        items = [(hh, c) for hh in range(hps) for c in range(2 * tq // rc)]

        def scores(hh, c):
            q0 = (c * rc) % tq
            kc = q0 + rc if masked else tq
            return lax.dot_general(qs_ref[hh, c * rc:(c + 1) * rc, :],
                                   k_ref[0:kc, hh * LANES:(hh + 1) * LANES],
                                   (((1,), (1,)), ((), ())), preferred_element_type=F32)

        s_next = scores(*items[0])
        for n, (hh, c) in enumerate(items):
            rows = slice(c * rc, (c + 1) * rc)
            q0 = (c * rc) % tq
            kc = q0 + rc if masked else tq
            s = s_next
            if n + 1 < len(items):
                s_next = scores(*items[n + 1])
            if masked:
                row = lax.broadcasted_iota(jnp.int32, s.shape, 0) + q0
                col = lax.broadcasted_iota(jnp.int32, s.shape, 1)
                s = jnp.where(col <= row, s, NEG_INF)
            cols = [s[:, j:j + LANES] for j in range(0, kc, LANES)]
            m_old = m_ref[hh, rows, :]
            m_new = jnp.maximum(m_old, jnp.max(functools.reduce(jnp.maximum, cols), axis=-1, keepdims=True))
            alpha = jnp.exp(m_old - m_new)
            ps = [jnp.exp(cj - m_new) for cj in cols]
            l_ref[hh, rows, :] = alpha * l_ref[hh, rows, :] + jnp.sum(functools.reduce(jnp.add, ps), axis=-1,
                                                                       keepdims=True)
            p = jnp.concatenate(ps, axis=1) if len(ps) > 1 else ps[0]
            pv = jnp.dot(p.astype(BF16), v_ref[0:kc, hh * LANES:(hh + 1) * LANES],
                         preferred_element_type=F32)
            acc_ref[hh, rows, :] = alpha * acc_ref[hh, rows, :] + pv
            m_ref[hh, rows, :] = m_new

    @pl.when(ki < qi)
    def _():
        step(False)

    @pl.when(ki == qi)
    def _():
        step(True)
        lam = _diff_lambda(lam_ref, lam_init)
        for hh in range(hps):
            o = acc_ref[hh] / l_ref[hh]
            d = o[0:tq] - lam * o[tq:2 * tq]
            o_ref[:, hh * LANES:(hh + 1) * LANES] = (_rms_rows(d, g_ref[...])
                                                     * (1.0 - lam_init)).astype(o_ref.dtype)


def prompt_attention(q, k, v, lam_vecs, subln_g, bz, s, hd, lam_init):
    t, qw = q.shape
    vd = 2 * hd
    nh = qw // vd
    tq = _tile(s, 512)
    nq = s // tq
    assert vd == LANES and tq % min(tq, FLASH_ROWS) == 0 and min(tq, FLASH_ROWS) % LANES == 0
    hps = 2 if nh % 2 == 0 else 1
    bw = hps * vd
    pairs = [(a, c) for a in range(nq) for c in range(a + 1)]
    qi_tab = jnp.asarray([a for a, _ in pairs], jnp.int32)
    ki_tab = jnp.asarray([c for _, c in pairs], jnp.int32)
    grid_spec = pltpu.PrefetchScalarGridSpec(
        num_scalar_prefetch=2,
        grid=(bz, nh // hps, len(pairs)),
        in_specs=[pl.BlockSpec((4, hd), lambda b, h, p, qt, kt: (0, 0)),
                  pl.BlockSpec((1, vd), lambda b, h, p, qt, kt: (0, 0)),
                  pl.BlockSpec((tq, bw), lambda b, h, p, qt, kt: (b * nq + qt[p], h)),
                  pl.BlockSpec((tq, bw), lambda b, h, p, qt, kt: (b * nq + kt[p], h)),
                  pl.BlockSpec((tq, bw), lambda b, h, p, qt, kt: (b * nq + kt[p], h))],
        out_specs=pl.BlockSpec((tq, bw), lambda b, h, p, qt, kt: (b * nq + qt[p], h)),
        scratch_shapes=[pltpu.VMEM((hps, 2 * tq, vd), BF16), pltpu.VMEM((hps, 2 * tq, LANES), F32),
                        pltpu.VMEM((hps, 2 * tq, LANES), F32), pltpu.VMEM((hps, 2 * tq, vd), F32)],
    )
    return pl.pallas_call(
        functools.partial(_flash_kernel, tq=tq, hd=hd, hps=hps, lam_init=lam_init),
        grid_spec=grid_spec,
        out_shape=jax.ShapeDtypeStruct((t, nh * vd), BF16),
        compiler_params=_params("parallel", "parallel", "arbitrary"),
        name="prompt_attention",
    )(qi_tab, ki_tab, lam_vecs, subln_g.reshape(1, vd), q, k, v)


def _paged_kernel(pt_ref, lam_ref, g_ref, qidx_ref, q_ref, e_ref, msk_ref, *refs,
                  n_steps, pp, rpb, lam_init):
    k_refs, v_refs = refs[:pp], refs[pp:2 * pp]
    kn_ref, vn_ref, o_ref, m_ref, l_ref, acc_ref = refs[2 * pp:]
    p = pl.program_id(1)

    @pl.when(p == 0)
    def _():
        m_ref[...] = jnp.full(m_ref.shape, NEG_INF, F32)
        l_ref[...] = jnp.zeros(l_ref.shape, F32)
        acc_ref[...] = jnp.zeros(acc_ref.shape, F32)

    def update(kts, vfs, masked):
        q = q_ref[...]
        ss = [jnp.dot(q, kt[...].astype(BF16), preferred_element_type=F32) for kt in kts]
        if masked:
            col = lax.broadcasted_iota(jnp.int32, ss[0].shape, 1)
            ss = [jnp.where(col <= qidx_ref[...], s, NEG_INF) for s in ss]
        m_old = m_ref[...]
        m_new = jnp.maximum(m_old, jnp.max(functools.reduce(jnp.maximum, ss), axis=-1, keepdims=True))
        alpha = jnp.exp(m_old - m_new)
        prs = [jnp.exp(s - m_new) for s in ss]
        l_ref[...] = alpha * l_ref[...] + jnp.sum(functools.reduce(jnp.add, prs), axis=-1, keepdims=True)
        pv = None
        for pr, vf in zip(prs, vfs):
            pe = jnp.dot(pr.astype(BF16), e_ref[...], preferred_element_type=F32)
            pe = pe.astype(BF16) * msk_ref[...]
            part = jnp.dot(pe, vf[...].astype(BF16), preferred_element_type=F32)
            pv = part if pv is None else pv + part
        acc_ref[...] = alpha * acc_ref[...] + pv
        m_ref[...] = m_new

    @pl.when(p < n_steps)
    def _():
        for c0 in range(0, pp, PAGE_GROUP):
            update(k_refs[c0:c0 + PAGE_GROUP], v_refs[c0:c0 + PAGE_GROUP], False)

    @pl.when(p == n_steps)
    def _():
        update([kn_ref], [vn_ref], True)
        o = acc_ref[...] / l_ref[...]
        lam = _diff_lambda(lam_ref, lam_init)
        r = o.shape[0]
        d = o - lam * pltpu.roll(o, r - rpb // 2, axis=0)
        o_ref[...] = _rms_rows(d, g_ref[...]) * (1.0 - lam_init)


def sample_attention(q, k_new, v_new, cache_k, cache_v, layer, page_table, lam_vecs, subln_g,
                     bz, nq, hd, lam_init):
    vd = 2 * hd
    qw = q.shape[1]
    nh = qw // vd
    n_odd, n_pool, page = cache_k.shape[:3]
    n_pages = page_table.shape[1]
    assert nq <= page
    pp = next(c for c in (8, 4, 2, 1) if n_pages % c == 0 and n_pages // c >= min(2, n_pages))
    n_steps = n_pages // pp
    rpb = 2 * nq
    r = nh * rpb
    ckt = cache_k.transpose(0, 1, 3, 4, 2).reshape(n_odd, n_pool, qw, page)
    cvf = cache_v.reshape(n_odd, n_pool, page * nh, vd)
    q4 = q.reshape(bz, nq, nh, 2, hd).transpose(0, 2, 3, 1, 4)
    eye = jnp.eye(2 * nh, dtype=q.dtype).reshape(nh, 2, 1, 2 * nh, 1)
    qbd = (q4.reshape(bz, nh, 2, nq, 1, hd) * eye[None]).reshape(bz, r, qw)
    knt = jnp.pad(k_new.reshape(bz, nq, qw).transpose(0, 2, 1), ((0, 0), (0, 0), (0, page - nq)))
    vnf = jnp.pad(v_new.reshape(bz, nq * nh, vd), ((0, 0), (0, (page - nq) * nh), (0, 0)))
    qidx = (jnp.arange(r, dtype=jnp.int32) % nq).reshape(r, 1)
    expand = jnp.repeat(jnp.eye(page, dtype=BF16), nh, axis=1)
    own = (jnp.arange(page * nh)[None, :] % nh == jnp.arange(r)[:, None] // rpb).astype(BF16)
    pt = page_table.reshape(-1).astype(jnp.int32)

    def page_map(c):
        def index(b, p, pt):
            return (layer, pt[b * n_pages + jnp.minimum(p, n_steps - 1) * pp + c], 0, 0)
        return index

    grid_spec = pltpu.PrefetchScalarGridSpec(
        num_scalar_prefetch=1,
        grid=(bz, n_steps + 1),
        in_specs=([pl.BlockSpec((4, hd), lambda b, p, pt: (0, 0)),
                   pl.BlockSpec((1, vd), lambda b, p, pt: (0, 0)),
                   pl.BlockSpec((r, 1), lambda b, p, pt: (0, 0)),
                   pl.BlockSpec((None, r, qw), lambda b, p, pt: (b, 0, 0)),
                   pl.BlockSpec((page, page * nh), lambda b, p, pt: (0, 0)),
                   pl.BlockSpec((r, page * nh), lambda b, p, pt: (0, 0))]
                  + [pl.BlockSpec((None, None, qw, page), page_map(c)) for c in range(pp)]
                  + [pl.BlockSpec((None, None, page * nh, vd), page_map(c)) for c in range(pp)]
                  + [pl.BlockSpec((None, qw, page), lambda b, p, pt: (b, 0, 0)),
                     pl.BlockSpec((None, page * nh, vd), lambda b, p, pt: (b, 0, 0))]),
        out_specs=pl.BlockSpec((None, r, vd), lambda b, p, pt: (b, 0, 0)),
        scratch_shapes=[pltpu.VMEM((r, 1), F32), pltpu.VMEM((r, 1), F32), pltpu.VMEM((r, vd), F32)],
    )
    o = pl.pallas_call(
        functools.partial(_paged_kernel, n_steps=n_steps, pp=pp, rpb=rpb, lam_init=lam_init),
        grid_spec=grid_spec,
        out_shape=jax.ShapeDtypeStruct((bz, r, vd), F32),
        compiler_params=_params("parallel", "arbitrary"),
        name="sample_attention",
    )(pt, lam_vecs, subln_g.reshape(1, vd), qidx, qbd, expand, own,
      *([ckt] * pp), *([cvf] * pp), knt, vnf)
    o = o.reshape(bz, nh, 2, nq, vd)[:, :, 0]
    return o.transpose(0, 2, 1, 3).reshape(bz * nq, nh * vd).astype(BF16)


def kernel(x_prompt, x_sample, cache_k, cache_v, state_ssm_re, state_ssm_im, state_conv, page_table, norm_mix_pre, norm_mix_post, norm_ffn_pre, norm_ffn_post, w_in_even, ssm_lam_re, ssm_lam_im, ssm_log_dt, ssm_b_re, ssm_b_im, ssm_c_re, ssm_c_im, ssm_d, w_glu, conv_w, conv_b, conv_ln_g, conv_ln_b, w_out_even, w_qkv, lambda_q1, lambda_k1, lambda_q2, lambda_k2, subln_g, w_o, w_gate, w_up, w_down):
    depth = norm_mix_pre.shape[0]
    d_model = x_prompt.shape[-1]
    a_width = w_glu.shape[1]
    n_groups, n_state = ssm_lam_re.shape[1:]
    bw = conv_w.shape[2]
    page = cache_k.shape[2]
    head_dim = cache_k.shape[-1]
    q_width = cache_k.shape[-2] * head_dim
    past_len = page_table.shape[1] * page

    w_in_b, w_glu_b, w_out_b = w_in_even.astype(BF16), w_glu.astype(BF16), w_out_even.astype(BF16)
    w_qkv_b, w_o_b = w_qkv.astype(BF16), w_o.astype(BF16)
    w_gate_b, w_up_b, w_down_b = w_gate.astype(BF16), w_up.astype(BF16), w_down.astype(BF16)
    n_odd = w_qkv.shape[0]
    lam_vecs = jnp.stack([lambda_q1, lambda_k1, lambda_q2, lambda_k2], axis=1)

    def run_trunk(x, pos0, ssm_re0, ssm_im0, conv0, paged):
        bz, t, _ = x.shape
        h = x.reshape(bz * t, d_model)
        pos = jnp.tile(pos0 + jnp.arange(t, dtype=F32), bz).reshape(bz * t, 1)
        new_re, new_im, new_conv = [], [], []
        k_all = v_all = None
        for i in range(depth):
            j = i // 2
            if i % 2 == 0:
                proj = norm_matmul(h, norm_mix_pre[i], w_in_b, j)
                y_a, s_re, s_im = s5_mixer(
                    proj, bz, t, a_width, ssm_re0[j], ssm_im0[j],
                    ssm_lam_re[j], ssm_lam_im[j], ssm_log_dt[j], ssm_b_re[j], ssm_b_im[j],
                    ssm_c_re[j], ssm_c_im[j], ssm_d[j])
                y_a = s5_glu(y_a, w_glu_b, j)
                y_b, buf = conv_module(proj, conv0[j], conv_w[j], conv_b[j], conv_ln_g[j],
                                       conv_ln_b[j], bz, t, a_width)
                new_re.append(s_re)
                new_im.append(s_im)
                new_conv.append(buf)
                mix_in, w_mix = jnp.concatenate([y_a, y_b], axis=-1), w_out_b
            else:
                lam_init = 0.8 - 0.6 * math.exp(-0.3 * i)
                q, k_all, v_all, kb, vb = qkv_rope(h, norm_mix_pre[i], w_qkv_b, pos, q_width, head_dim,
                                                   j, n_odd, t, k_all, v_all)
                if paged:
                    assert k_all.ndim == 3
                    mix_in = sample_attention(q, k_all[j], v_all[j], cache_k, cache_v, j, page_table,
                                              lam_vecs[j], subln_g[j], bz, t, head_dim, lam_init)
                else:
                    mix_in = prompt_attention(q, kb, vb, lam_vecs[j], subln_g[j], bz, t, head_dim,
                                              lam_init)
                w_mix = w_o_b
            h = matmul_norm_residual(mix_in, w_mix, j, norm_mix_post[i], h)
            h = ffn(h, norm_ffn_pre[i], norm_ffn_post[i], w_gate_b, w_up_b, w_down_b, i)
        n_kh = q_width // head_dim
        if k_all.ndim == 4:
            new_k = k_all.reshape(n_odd, bz, n_kh, head_dim, t).transpose(0, 1, 4, 2, 3)
        else:
            new_k = k_all.reshape(n_odd, bz, t, n_kh, head_dim)
        new_v = v_all.reshape(n_odd, bz, t, -1, 2 * head_dim)
        return (h.reshape(bz, t, d_model), jnp.stack(new_re), jnp.stack(new_im), jnp.stack(new_conv),
                new_k, new_v)

    n_prompt = x_prompt.shape[0]
    n_even = state_ssm_re.shape[0]
    zero_ssm = jnp.zeros((n_even, n_prompt, n_groups, n_state), F32)
    zero_conv = jnp.zeros((n_even, n_prompt, conv_w.shape[1] - 1, bw), F32)
    y_p, re_p, im_p, conv_p, k_p, v_p = run_trunk(x_prompt, 0.0, zero_ssm, zero_ssm, zero_conv, False)
    y_s, re_s, im_s, conv_s, k_s, v_s = run_trunk(x_sample, float(past_len), state_ssm_re, state_ssm_im,
                                                  state_conv, True)
    return (y_p, y_s, re_p, im_p, conv_p, k_p, v_p, re_s, im_s, conv_s, k_s, v_s)
```

```python
import functools
import math

import jax
import jax.numpy as jnp
from jax import lax
from jax.experimental import pallas as pl
from jax.experimental.pallas import tpu as pltpu

F32 = jnp.float32
BF16 = jnp.bfloat16
EPS = 1e-6
NEG_INF = -1e30
ROPE_THETA = 500000.0
S5_CHUNK = 16
S5_ROWS = 256
CONV_HALO = 32
LANES = 128
FLASH_ROWS = 256
PAGE_GROUP = 4
VMEM_LIMIT = 56 * 1024 * 1024
HI = lax.Precision.HIGHEST


def _tile(n, pref):
    if n <= pref:
        return n
    t = pref
    while t >= 8:
        if n % t == 0:
            return t
        t //= 2
    return n


def _params(*sem):
    return pltpu.CompilerParams(dimension_semantics=sem, vmem_limit_bytes=VMEM_LIMIT)


def _rms_rows(x, g, eps=EPS):
    ms = jnp.mean(x * x, axis=-1, keepdims=True)
    return x * lax.rsqrt(ms + eps) * g


def _norm_matmul_kernel(x_ref, g_ref, w_ref, o_ref, hn_ref):
    @pl.when(pl.program_id(1) == 0)
    def _():
        hn_ref[...] = _rms_rows(x_ref[...], g_ref[...]).astype(BF16)

    o_ref[...] = jnp.dot(hn_ref[...], w_ref[...], preferred_element_type=F32)


def norm_matmul(x, g, w, layer):
    t, d = x.shape
    n = w.shape[2]
    tm, tn = _tile(t, 1024), _tile(n, 1024)
    return pl.pallas_call(
        _norm_matmul_kernel,
        grid=(t // tm, n // tn),
        in_specs=[pl.BlockSpec((tm, d), lambda i, j: (i, 0)),
                  pl.BlockSpec((1, d), lambda i, j: (0, 0)),
                  pl.BlockSpec((None, d, tn), lambda i, j: (layer, 0, j))],
        out_specs=pl.BlockSpec((tm, tn), lambda i, j: (i, j)),
        out_shape=jax.ShapeDtypeStruct((t, n), F32),
        scratch_shapes=[pltpu.VMEM((tm, d), BF16)],
        compiler_params=_params("parallel", "arbitrary"),
        name="norm_matmul",
    )(x, g.reshape(1, d), w)


def _qkv_kernel(x_ref, g_ref, pos_ref, invf_ref, sel_ref, w_ref, *rest, nq_tiles, scale, n_prev, k_major):
    q_ref, k_ref, v_ref, kb_ref, vb_ref, hn_ref, tab_ref = rest[n_prev:]
    j = pl.program_id(1)

    @pl.when(j == 0)
    def _():
        hn_ref[...] = _rms_rows(x_ref[...], g_ref[...]).astype(BF16)
        ang = pos_ref[...] * invf_ref[...]
        c, s = jnp.cos(ang), jnp.sin(ang)
        tab_ref[0] = jnp.where(sel_ref[0:1, :] > 0.5, c, 1.0)
        tab_ref[1] = s * sel_ref[1:2, :]
        tab_ref[2] = s * sel_ref[2:3, :]

    tn = w_ref.shape[1]
    cw = min(tn, 2 * LANES)
    half = LANES // 16

    def rope(y):
        parts = []
        for c0 in range(0, y.shape[1], LANES):
            x = y[:, c0:c0 + LANES]
            parts.append(x * tab_ref[0] + pltpu.roll(x, half, axis=1) * tab_ref[1]
                         + pltpu.roll(x, LANES - half, axis=1) * tab_ref[2])
        return jnp.concatenate(parts, axis=1) if len(parts) > 1 else parts[0]

    def sweep(emit):
        hn = hn_ref[...]
        chunks = list(range(0, tn, cw))
        y_next = jnp.dot(hn, w_ref[:, 0:cw], preferred_element_type=F32)
        for n, c0 in enumerate(chunks):
            y = y_next
            if n + 1 < len(chunks):
                y_next = jnp.dot(hn, w_ref[:, c0 + cw:c0 + 2 * cw], preferred_element_type=F32)
            emit(slice(c0, c0 + cw), y)

    @pl.when(j < nq_tiles)
    def _():
        def emit(cols, y):
            q_ref[:, cols] = (rope(y) * scale).astype(BF16)
        sweep(emit)

    @pl.when(jnp.logical_and(j >= nq_tiles, j < 2 * nq_tiles))
    def _():
        def emit(cols, y):
            r = rope(y)
            if k_major:
                k_ref[cols, :] = r.T
            else:
                k_ref[:, cols] = r
            kb_ref[:, cols] = r.astype(BF16)
        sweep(emit)

    @pl.when(j >= 2 * nq_tiles)
    def _():
        def emit(cols, y):
            v_ref[:, cols] = y
            vb_ref[:, cols] = y.astype(BF16)
        sweep(emit)


def qkv_rope(x, g, w, pos, q_width, head_dim, layer, n_layers, seq, k_all=None, v_all=None):
    t, d = x.shape
    n = w.shape[2]
    v_width = n - 2 * q_width
    rot = head_dim // 4
    assert rot == 16 and 128 % head_dim == 0
    tm = _tile(t, 512)
    tn = _tile(math.gcd(q_width, v_width), 1024)
    nq, nv = q_width // tn, v_width // tn
    k_major = seq % tm == 0 and tm % LANES == 0
    tps = seq // tm if k_major else 1
    prev = [] if k_all is None else [k_all, v_all]
    lane = jnp.arange(128) % head_dim
    inv_freq = ROPE_THETA ** (-jnp.arange(rot // 2, dtype=F32) * 2.0 / rot)
    invf = jnp.where(lane < rot, inv_freq[lane % (rot // 2)], 0.0).reshape(1, 128).astype(F32)
    sel = jnp.stack([(lane < rot).astype(F32),
                     jnp.logical_and(lane >= rot // 2, lane < rot).astype(F32),
                     -(lane < rot // 2).astype(F32)])

    def clip(j, lo, cnt):
        return jnp.clip(j - lo, 0, cnt - 1)

    if k_major:
        k_spec = pl.BlockSpec((None, None, tn, tm), lambda i, j: (layer, i // tps, clip(j, nq, nq), i % tps))
        k_shape = jax.ShapeDtypeStruct((n_layers, t // seq, q_width, seq), F32)
    else:
        k_spec = pl.BlockSpec((None, tm, tn), lambda i, j: (layer, i, clip(j, nq, nq)))
        k_shape = jax.ShapeDtypeStruct((n_layers, t, q_width), F32)
    outs = pl.pallas_call(
        functools.partial(_qkv_kernel, nq_tiles=nq, scale=head_dim ** -0.5, n_prev=len(prev), k_major=k_major),
        grid=(t // tm, 2 * nq + nv),
        in_specs=[pl.BlockSpec((tm, d), lambda i, j: (i, 0)),
                  pl.BlockSpec((1, d), lambda i, j: (0, 0)),
                  pl.BlockSpec((tm, 1), lambda i, j: (i, 0)),
                  pl.BlockSpec((1, 128), lambda i, j: (0, 0)),
                  pl.BlockSpec((3, 128), lambda i, j: (0, 0)),
                  pl.BlockSpec((None, d, tn), lambda i, j: (layer, 0, j))]
                 + [pl.BlockSpec(memory_space=pl.ANY)] * len(prev),
        out_specs=[pl.BlockSpec((tm, tn), lambda i, j: (i, clip(j, 0, nq))),
                   k_spec,
                   pl.BlockSpec((None, tm, tn), lambda i, j: (layer, i, clip(j, 2 * nq, nv))),
                   pl.BlockSpec((tm, tn), lambda i, j: (i, clip(j, nq, nq))),
                   pl.BlockSpec((tm, tn), lambda i, j: (i, clip(j, 2 * nq, nv)))],
        out_shape=[jax.ShapeDtypeStruct((t, q_width), BF16),
                   k_shape,
                   jax.ShapeDtypeStruct((n_layers, t, v_width), F32),
                   jax.ShapeDtypeStruct((t, q_width), BF16),
                   jax.ShapeDtypeStruct((t, v_width), BF16)],
        scratch_shapes=[pltpu.VMEM((tm, d), BF16), pltpu.VMEM((3, tm, 128), F32)],
        input_output_aliases={6 + n: 1 + n for n in range(len(prev))},
        compiler_params=_params("parallel", "arbitrary"),
        name="qkv_rope",
    )(x, g.reshape(1, d), pos, invf, sel, w, *prev)
    return outs


def _matmul_norm_res_kernel(a_ref, w_ref, g_ref, h_ref, o_ref, acc_ref, *, nj, tn):
    j = pl.program_id(1)
    acc_ref[j] = jnp.dot(a_ref[...], w_ref[...], preferred_element_type=F32)

    @pl.when(j == nj - 1)
    def _():
        ss = jnp.zeros((acc_ref.shape[1], 1), F32)
        for jj in range(nj):
            y = acc_ref[jj]
            ss = ss + jnp.sum(y * y, axis=-1, keepdims=True)
        inv = lax.rsqrt(ss / (nj * tn) + EPS)
        for jj in range(nj):
            sl = slice(jj * tn, (jj + 1) * tn)
            o_ref[:, sl] = h_ref[:, sl] + acc_ref[jj] * inv * g_ref[:, sl]


def matmul_norm_residual(a, w, layer, g, h):
    t, k = a.shape
    d = w.shape[2]
    tm, tn = _tile(t, 512), _tile(d, 1024)
    nj = d // tn
    return pl.pallas_call(
        functools.partial(_matmul_norm_res_kernel, nj=nj, tn=tn),
        grid=(t // tm, nj),
        in_specs=[pl.BlockSpec((tm, k), lambda i, j: (i, 0)),
                  pl.BlockSpec((None, k, tn), lambda i, j: (layer, 0, j)),
                  pl.BlockSpec((1, d), lambda i, j: (0, 0)),
                  pl.BlockSpec((tm, d), lambda i, j: (i, 0))],
        out_specs=pl.BlockSpec((tm, d), lambda i, j: (i, 0)),
        out_shape=jax.ShapeDtypeStruct((t, d), F32),
        scratch_shapes=[pltpu.VMEM((nj, tm, tn), F32)],
        compiler_params=_params("parallel", "arbitrary"),
        name="matmul_norm_residual",
    )(a, w, g.reshape(1, d), h)


def _ffn_kernel(h_ref, gpre_ref, gpost_ref, wg_ref, wu_ref, wd_ref, o_ref, hn_ref, *, nf):
    f = pl.program_id(1)

    @pl.when(f == 0)
    def _():
        hn_ref[...] = _rms_rows(h_ref[...], gpre_ref[...]).astype(BF16)
        o_ref[...] = jnp.zeros(o_ref.shape, F32)

    hn = hn_ref[...]
    gate = jnp.dot(hn, wg_ref[...], preferred_element_type=F32)
    up = jnp.dot(hn, wu_ref[...], preferred_element_type=F32)
    act = (gate * jax.nn.sigmoid(gate) * up).astype(BF16)
    o_ref[...] += jnp.dot(act, wd_ref[...], preferred_element_type=F32)

    @pl.when(f == nf - 1)
    def _():
        o_ref[...] = h_ref[...] + _rms_rows(o_ref[...], gpost_ref[...])


def ffn(h, g_pre, g_post, wg, wu, wd, layer):
    t, d = h.shape
    fh = wg.shape[2]
    tm = _tile(t, 1024)
    tf = next((c for c in (512, 256) if fh % c == 0), fh)
    nf = fh // tf
    return pl.pallas_call(
        functools.partial(_ffn_kernel, nf=nf),
        grid=(t // tm, nf),
        in_specs=[pl.BlockSpec((tm, d), lambda i, f: (i, 0), pipeline_mode=pl.Buffered(1)),
                  pl.BlockSpec((1, d), lambda i, f: (0, 0)),
                  pl.BlockSpec((1, d), lambda i, f: (0, 0)),
                  pl.BlockSpec((None, d, tf), lambda i, f: (layer, 0, f)),
                  pl.BlockSpec((None, d, tf), lambda i, f: (layer, 0, f)),
                  pl.BlockSpec((None, tf, d), lambda i, f: (layer, f, 0))],
        out_specs=pl.BlockSpec((tm, d), lambda i, f: (i, 0)),
        out_shape=jax.ShapeDtypeStruct((t, d), F32),
        scratch_shapes=[pltpu.VMEM((tm, d), BF16)],
        compiler_params=_params("parallel", "arbitrary"),
        name="ffn",
    )(h, g_pre.reshape(1, d), g_post.reshape(1, d), wg, wu, wd)


def _glu_kernel(y_ref, w_ref, o_ref):
    y = y_ref[...]
    z = jnp.dot(y.astype(BF16), w_ref[...], preferred_element_type=F32)
    o_ref[...] = (y * jax.nn.sigmoid(z)).astype(o_ref.dtype)


def s5_glu(y, w, layer):
    t, a = y.shape
    tm = _tile(t, 1024)
    return pl.pallas_call(
        _glu_kernel,
        grid=(t // tm,),
        in_specs=[pl.BlockSpec((tm, a), lambda i: (i, 0)),
                  pl.BlockSpec((None, a, a), lambda i: (layer, 0, 0))],
        out_specs=pl.BlockSpec((tm, a), lambda i: (i, 0)),
        out_shape=jax.ShapeDtypeStruct((t, a), BF16),
        compiler_params=_params("parallel"),
        name="s5_glu",
    )(y, w)


def _s5_prep_kernel(lam_row_ref, lam_col_ref, bt_ref, ct_ref, e_ref, tt_ref,
                    g0_ref, p_ref, q_ref, al_ref, *, chunk, gq):
    for gi in range(gq):
        _s5_prep_group(lam_row_ref.at[gi], lam_col_ref.at[gi], bt_ref.at[gi], ct_ref.at[gi], e_ref, tt_ref,
                       g0_ref.at[gi], p_ref.at[gi], q_ref.at[gi], al_ref.at[gi], chunk=chunk)


def _s5_prep_group(lam_row_ref, lam_col_ref, bt_ref, ct_ref, e_ref, tt_ref,
                   g0_ref, p_ref, q_ref, al_ref, *, chunk):
    lr, li, ldt = lam_row_ref[0:1, :], lam_row_ref[1:2, :], lam_row_ref[2:3, :]
    dt = jnp.exp(ldt)
    mag = jnp.exp(lr * dt)
    ab_re, ab_im = mag * jnp.cos(li * dt), mag * jnp.sin(li * dt)
    den = lr * lr + li * li
    num_re = ab_re - 1.0
    coef_re = (num_re * lr + ab_im * li) / den
    coef_im = (ab_im * lr - num_re * li) / den
    bt_re, bt_im = bt_ref[0], bt_ref[1]
    bb_re = coef_re * bt_re - coef_im * bt_im
    bb_im = coef_re * bt_im + coef_im * bt_re

    magl = jnp.exp(lr * dt * chunk)
    al_ref[0:1, :] = magl * jnp.cos(li * dt * chunk)
    al_ref[1:2, :] = magl * jnp.sin(li * dt * chunk)

    e = e_ref[...]
    pm = jnp.exp(lr * dt * e)
    pw_re, pw_im = pm * jnp.cos(li * dt * e), pm * jnp.sin(li * dt * e)
    bbt_re = jnp.concatenate([bb_re] * chunk, axis=0)
    bbt_im = jnp.concatenate([bb_im] * chunk, axis=0)
    n_state = pw_re.shape[1]
    p_ref[:, 0:n_state] = pw_re * bbt_re - pw_im * bbt_im
    p_ref[:, n_state:2 * n_state] = pw_re * bbt_im + pw_im * bbt_re

    lrc, lic, ldtc = lam_col_ref[0], lam_col_ref[1], lam_col_ref[2]
    dtc = jnp.exp(ldtc)
    tt = tt_ref[...]
    fm = jnp.exp(lrc * dtc * tt)
    fw_re, fw_im = fm * jnp.cos(lic * dtc * tt), fm * jnp.sin(lic * dtc * tt)
    ct_re, ct_im = ct_ref[0], ct_ref[1]
    f_re = ct_re * fw_re - ct_im * fw_im
    f_im = ct_re * fw_im + ct_im * fw_re
    g0_ref[...] = (jnp.dot(bb_re, f_re, precision=HI, preferred_element_type=F32)
                   - jnp.dot(bb_im, f_im, precision=HI, preferred_element_type=F32))
    magc = jnp.exp(lrc * dtc)
    abc_re, abc_im = magc * jnp.cos(lic * dtc), magc * jnp.sin(lic * dtc)
    q_ref[0] = (f_re * abc_re - f_im * abc_im).astype(q_ref.dtype)
    q_ref[1] = (-(f_re * abc_im + f_im * abc_re)).astype(q_ref.dtype)


def _s5_assemble_kernel(a_ref, p_ref, q_ref, tw_ref, tp_ref, rgw_ref, rgs_ref, cgw_ref, cgs_ref,
                        wp_ref, qb_ref, *, nw, ns):
    a = a_ref[...].astype(BF16)
    pm = p_ref[...].astype(BF16)
    q = q_ref[...]
    cs = min(nw, 512)
    for c0 in range(0, nw, cs):
        cols = slice(c0, c0 + cs)
        w = jnp.dot(a, tw_ref[:, cols], preferred_element_type=F32)
        wp_ref[:, cols] = jnp.where(rgw_ref[...] == cgw_ref[:, cols], w, 0.0).astype(BF16)
        qv = jnp.dot(q, tw_ref[:, cols], preferred_element_type=F32)
        qb_ref[:, cols] = jnp.where(rgs_ref[...] == cgw_ref[:, cols], qv, 0.0).astype(BF16)
    cs = min(ns, 512)
    for c0 in range(0, ns, cs):
        cols = slice(c0, c0 + cs)
        pv = jnp.dot(pm, tp_ref[:, cols], preferred_element_type=F32)
        wp_ref[:, nw + c0:nw + c0 + cs] = jnp.where(rgw_ref[...] == cgs_ref[:, cols], pv, 0.0).astype(BF16)


def _s5_main_kernel(u_ref, w_ref, q_ref, al_ref, h0_ref, d_ref,
                    y_ref, hn_ref, bc_ref, xs_ref, *, chunk, nct, bpt):
    rows = bc_ref.shape[0]
    nw = chunk * LANES
    half = bc_ref.shape[1] // 2
    us = [u_ref[pl.ds(s, rows, stride=chunk), :] for s in range(chunk)]
    ucat = jnp.concatenate(us, axis=1).astype(BF16)
    r = jnp.dot(ucat, w_ref[...], preferred_element_type=F32)
    bc_ref[...] = r[:, nw:]
    a_re, a_im = al_ref[:, 0:half], al_ref[:, half:]

    def step(j, carry):
        nxt = []
        for bl in range(bpt):
            x_re, x_im = carry[2 * bl], carry[2 * bl + 1]
            row = pl.ds(bl * nct + j, 1)
            xs_ref[row, 0:half] = x_re
            xs_ref[row, half:] = x_im
            nxt.append(a_re * x_re - a_im * x_im + bc_ref[row, 0:half])
            nxt.append(a_re * x_im + a_im * x_re + bc_ref[row, half:])
        return tuple(nxt)

    init = tuple(h0_ref[bl:bl + 1, sl] for bl in range(bpt) for sl in (slice(0, half), slice(half, 2 * half)))
    fin = lax.fori_loop(0, nct, step, init)
    for bl in range(bpt):
        hn_ref[bl:bl + 1, 0:half] = fin[2 * bl]
        hn_ref[bl:bl + 1, half:] = fin[2 * bl + 1]
    yc = jnp.dot(xs_ref[...].astype(BF16), q_ref[...], preferred_element_type=F32)
    for t in range(chunk):
        sl = slice(t * LANES, (t + 1) * LANES)
        y_ref[pl.ds(t, rows, stride=chunk), :] = jax.nn.gelu(r[:, sl] + yc[:, sl] + d_ref[...] * us[t])


def s5_mixer(proj, bz, t, a, h0_re, h0_im, lam_re, lam_im, log_dt, b_re, b_im, c_re, c_im, d_skip):
    g, p = lam_re.shape
    gc = a // g
    chunk = S5_CHUNK if t % S5_CHUNK == 0 else t
    nct = t // chunk
    lc = chunk * gc

    lam_row = jnp.stack([lam_re, lam_im, jnp.broadcast_to(log_dt[:, None], (g, p))], axis=1)
    lam_col = jnp.broadcast_to(lam_row[:, :, :, None], (g, 3, p, lc))
    bt = jnp.stack([b_re, b_im], axis=1).transpose(0, 1, 3, 2)
    ct = jnp.tile(jnp.stack([c_re, c_im], axis=1).transpose(0, 1, 3, 2), (1, 1, 1, chunk))

    step = jnp.arange(lc, dtype=jnp.int32) // gc
    e_col = (chunk - 1 - step).astype(F32).reshape(lc, 1)
    tt_row = step.astype(F32).reshape(1, lc)

    gq = next(c for c in (8, 4, 2, 1) if g % c == 0)
    g0, pmat, qmat, al = pl.pallas_call(
        functools.partial(_s5_prep_kernel, chunk=chunk, gq=gq),
        grid=(g // gq,),
        in_specs=[pl.BlockSpec((gq, 3, p), lambda i: (i, 0, 0)),
                  pl.BlockSpec((gq, 3, p, lc), lambda i: (i, 0, 0, 0)),
                  pl.BlockSpec((gq, 2, gc, p), lambda i: (i, 0, 0, 0)),
                  pl.BlockSpec((gq, 2, p, lc), lambda i: (i, 0, 0, 0)),
                  pl.BlockSpec((lc, 1), lambda i: (0, 0)),
                  pl.BlockSpec((1, lc), lambda i: (0, 0))],
        out_specs=[pl.BlockSpec((gq, gc, lc), lambda i: (i, 0, 0)),
                   pl.BlockSpec((gq, lc, 2 * p), lambda i: (i, 0, 0)),
                   pl.BlockSpec((gq, 2, p, lc), lambda i: (i, 0, 0, 0)),
                   pl.BlockSpec((gq, 2, p), lambda i: (i, 0, 0))],
        out_shape=[jax.ShapeDtypeStruct((g, gc, lc), F32),
                   jax.ShapeDtypeStruct((g, lc, 2 * p), F32),
                   jax.ShapeDtypeStruct((g, 2, p, lc), BF16),
                   jax.ShapeDtypeStruct((g, 2, p), F32)],
        compiler_params=_params("parallel"),
        name="s5_prep",
    )(lam_row, lam_col, bt, ct, e_col, tt_row)

    g0p = jnp.pad(g0.reshape(g, gc, chunk, gc), ((0, 0), (0, 0), (chunk, 0), (0, 0)))
    wt = jnp.stack([g0p[:, :, chunk - s:2 * chunk - s, :] for s in range(chunk)], axis=1)

    assert LANES % gc == 0 and g % (LANES // gc) == 0
    gb = LANES // gc
    nb = g // gb
    nw, ns = chunk * LANES, 2 * gb * p
    a_rows = wt.reshape(nb, gb, chunk, gc, lc).transpose(0, 2, 1, 3, 4).reshape(nb, nw, lc)
    p_rows = pmat.reshape(nb, gb, chunk, gc, 2 * p).transpose(0, 2, 1, 3, 4).reshape(nb, nw, 2 * p)
    q_rows = qmat.reshape(nb, gb, 2, p, lc).transpose(0, 2, 1, 3, 4).reshape(nb, ns, lc)
    tile_w = jnp.kron(jnp.eye(chunk, dtype=BF16), jnp.tile(jnp.eye(gc, dtype=BF16), (1, gb)))
    tile_p = jnp.kron(jnp.eye(2, dtype=BF16), jnp.tile(jnp.eye(p, dtype=BF16), (1, gb)))
    grp_w = (jnp.arange(nw, dtype=jnp.int32) // gc) % gb
    grp_s = (jnp.arange(ns, dtype=jnp.int32) // p) % gb
    wp, qbig = pl.pallas_call(
        functools.partial(_s5_assemble_kernel, nw=nw, ns=ns),
        grid=(nb,),
        in_specs=[pl.BlockSpec((None, nw, lc), lambda i: (i, 0, 0)),
                  pl.BlockSpec((None, nw, 2 * p), lambda i: (i, 0, 0)),
                  pl.BlockSpec((None, ns, lc), lambda i: (i, 0, 0)),
                  pl.BlockSpec((lc, nw), lambda i: (0, 0)),
                  pl.BlockSpec((2 * p, ns), lambda i: (0, 0)),
                  pl.BlockSpec((nw, 1), lambda i: (0, 0)),
                  pl.BlockSpec((ns, 1), lambda i: (0, 0)),
                  pl.BlockSpec((1, nw), lambda i: (0, 0)),
                  pl.BlockSpec((1, ns), lambda i: (0, 0))],
        out_specs=[pl.BlockSpec((None, nw, nw + ns), lambda i: (i, 0, 0)),
                   pl.BlockSpec((None, ns, nw), lambda i: (i, 0, 0))],
        out_shape=[jax.ShapeDtypeStruct((nb, nw, nw + ns), BF16),
                   jax.ShapeDtypeStruct((nb, ns, nw), BF16)],
        compiler_params=_params("parallel"),
        name="s5_assemble",
    )(a_rows, p_rows, q_rows, tile_w, tile_p, grp_w.reshape(nw, 1), grp_s.reshape(ns, 1),
      grp_w.reshape(1, nw), grp_s.reshape(1, ns))
    al_big = al.reshape(nb, gb, 2, p).transpose(0, 2, 1, 3).reshape(nb, 1, ns)

    bpt = max(1, min(bz, S5_ROWS // nct))
    assert bz % bpt == 0
    rt = bz // bpt
    rows = bpt * nct
    h0 = jnp.stack([h0_re, h0_im], axis=0).reshape(2, rt, bpt, nb, gb, p)
    h0 = h0.transpose(3, 1, 2, 0, 4, 5).reshape(nb, rt, bpt, ns)

    y, hn = pl.pallas_call(
        functools.partial(_s5_main_kernel, chunk=chunk, nct=nct, bpt=bpt),
        grid=(nb, rt),
        in_specs=[pl.BlockSpec((rows * chunk, LANES), lambda i, r: (r, i)),
                  pl.BlockSpec((None, nw, nw + ns), lambda i, r: (i, 0, 0)),
                  pl.BlockSpec((None, ns, nw), lambda i, r: (i, 0, 0)),
                  pl.BlockSpec((None, 1, ns), lambda i, r: (i, 0, 0)),
                  pl.BlockSpec((None, None, bpt, ns), lambda i, r: (i, r, 0, 0)),
                  pl.BlockSpec((None, 1, LANES), lambda i, r: (i, 0, 0))],
        out_specs=[pl.BlockSpec((rows * chunk, LANES), lambda i, r: (r, i)),
                   pl.BlockSpec((None, None, bpt, ns), lambda i, r: (i, r, 0, 0))],
        out_shape=[jax.ShapeDtypeStruct((bz * t, a), F32),
                   jax.ShapeDtypeStruct((nb, rt, bpt, ns), F32)],
        scratch_shapes=[pltpu.VMEM((rows, ns), F32)] * 2,
        compiler_params=_params("parallel", "arbitrary"),
        name="s5_main",
    )(proj, wp, qbig, al_big, h0, d_skip.reshape(nb, 1, LANES))

    hn = hn.reshape(nb, bz, 2, gb, p).transpose(2, 1, 0, 3, 4).reshape(2, bz, g, p)
    return y, hn[0], hn[1]


def _conv_kernel(v_ref, g_ref, buf_ref, w_ref, b_ref, lg_ref, lb_ref, y_ref, nbuf_ref, win_ref, acc_ref,
                 *, tt, width, nt):
    i = pl.program_id(1)
    keep = width - 1
    off = CONV_HALO - keep

    @pl.when(i == 0)
    def _():
        win_ref[0:off, :] = jnp.zeros((off, win_ref.shape[1]), F32)
        win_ref[off:CONV_HALO, :] = buf_ref[...]

    v = v_ref[...]
    win_ref[CONV_HALO:CONV_HALO + tt, :] = v * jax.nn.sigmoid(g_ref[...])

    bw = win_ref.shape[1]
    rb = min(tt, 64)
    for r0 in range(0, tt, rb):
        for c0 in range(0, bw, 128):
            acc = jnp.zeros((rb, 128), F32)
            for k in range(width):
                acc = acc + w_ref[k:k + 1, c0:c0 + 128] * win_ref[off + r0 + k:off + r0 + k + rb, c0:c0 + 128]
            acc_ref[r0:r0 + rb, c0:c0 + 128] = acc

    y = acc_ref[...] + b_ref[...]
    mu = jnp.mean(y, axis=-1, keepdims=True)
    var = jnp.mean(jnp.square(y - mu), axis=-1, keepdims=True)
    z = (y - mu) * lax.rsqrt(var + EPS) * lg_ref[...] + lb_ref[...]
    y_ref[...] = (z * jax.nn.sigmoid(z)).astype(y_ref.dtype)

    @pl.when(i == nt - 1)
    def _():
        nbuf_ref[...] = win_ref[tt + off:tt + CONV_HALO, :]

    tail = win_ref[tt:tt + CONV_HALO, :]
    win_ref[0:CONV_HALO, :] = tail


def conv_module(proj, buf, conv_w, conv_b, ln_g, ln_b, bz, t, a_width):
    width, bw = conv_w.shape
    assert a_width % bw == 0 and width - 1 <= CONV_HALO
    tt = _tile(t, 128)
    nt = t // tt
    vcol = a_width // bw
    proj = proj.reshape(bz, t, proj.shape[-1])
    y, nbuf = pl.pallas_call(
        functools.partial(_conv_kernel, tt=tt, width=width, nt=nt),
        grid=(bz, nt),
        in_specs=[pl.BlockSpec((None, tt, bw), lambda b, i: (b, i, vcol)),
                  pl.BlockSpec((None, tt, bw), lambda b, i: (b, i, vcol + 1)),
                  pl.BlockSpec((None, width - 1, bw), lambda b, i: (b, 0, 0)),
                  pl.BlockSpec((width, bw), lambda b, i: (0, 0)),
                  pl.BlockSpec((1, bw), lambda b, i: (0, 0)),
                  pl.BlockSpec((1, bw), lambda b, i: (0, 0)),
                  pl.BlockSpec((1, bw), lambda b, i: (0, 0))],
        out_specs=[pl.BlockSpec((None, tt, bw), lambda b, i: (b, i, 0)),
                   pl.BlockSpec((None, width - 1, bw), lambda b, i: (b, 0, 0))],
        out_shape=[jax.ShapeDtypeStruct((bz, t, bw), BF16),
                   jax.ShapeDtypeStruct((bz, width - 1, bw), F32)],
        scratch_shapes=[pltpu.VMEM((CONV_HALO + tt, bw), F32), pltpu.VMEM((tt, bw), F32)],
        compiler_params=_params("parallel", "arbitrary"),
        name="conv_module",
    )(proj, proj, buf, conv_w, conv_b.reshape(1, bw), ln_g.reshape(1, bw), ln_b.reshape(1, bw))
    return y.reshape(bz * t, bw), nbuf


def _diff_lambda(lam_ref, lam_init):
    s1 = jnp.sum(lam_ref[0:1, :] * lam_ref[1:2, :], axis=-1, keepdims=True)
    s2 = jnp.sum(lam_ref[2:3, :] * lam_ref[3:4, :], axis=-1, keepdims=True)
    return jnp.exp(s1) - jnp.exp(s2) + lam_init


def _flash_kernel(qi_ref, ki_ref, lam_ref, g_ref, q_ref, k_ref, v_ref, o_ref, qs_ref, m_ref, l_ref, acc_ref,
                  *, tq, hd, hps, lam_init):
    pair = pl.program_id(2)
    qi, ki = qi_ref[pair], ki_ref[pair]
    rc = min(tq, FLASH_ROWS)

    @pl.when(ki == 0)
    def _():
        for hh in range(hps):
            q = q_ref[:, hh * LANES:(hh + 1) * LANES]
            lane = lax.broadcasted_iota(jnp.int32, q.shape, 1)
            zero = jnp.zeros_like(q)
            qs_ref[hh, 0:tq, :] = jnp.where(lane < hd, q, zero)
            qs_ref[hh, tq:2 * tq, :] = jnp.where(lane >= hd, q, zero)
        m_ref[...] = jnp.full(m_ref.shape, NEG_INF, F32)
        l_ref[...] = jnp.zeros(l_ref.shape, F32)
        acc_ref[...] = jnp.zeros(acc_ref.shape, F32)

    def step(masked):
        items = [(hh, c) for hh in range(hps) for c in range(2 * tq // rc)]

        def scores(hh, c):
            q0 = (c * rc) % tq
            kc = q0 + rc if masked else tq
            return lax.dot_general(qs_ref[hh, c * rc:(c + 1) * rc, :],
                                   k_ref[0:kc, hh * LANES:(hh + 1) * LANES],
                                   (((1,), (1,)), ((), ())), preferred_element_type=F32)

        s_next = scores(*items[0])
        for n, (hh, c) in enumerate(items):
            rows = slice(c * rc, (c + 1) * rc)
            q0 = (c * rc) % tq
            kc = q0 + rc if masked else tq
            s = s_next
            if n + 1 < len(items):
                s_next = scores(*items[n + 1])
            if masked:
                row = lax.broadcasted_iota(jnp.int32, s.shape, 0) + q0
                col = lax.broadcasted_iota(jnp.int32, s.shape, 1)
                s = jnp.where(col <= row, s, NEG_INF)
            cols = [s[:, j:j + LANES] for j in range(0, kc, LANES)]
            m_old = m_ref[hh, rows, :]
            m_new = jnp.maximum(m_old, jnp.max(functools.reduce(jnp.maximum, cols), axis=-1, keepdims=True))
            alpha = jnp.exp(m_old - m_new)
            ps = [jnp.exp(cj - m_new) for cj in cols]
            l_ref[hh, rows, :] = alpha * l_ref[hh, rows, :] + jnp.sum(functools.reduce(jnp.add, ps), axis=-1,
                                                                       keepdims=True)
            p = jnp.concatenate(ps, axis=1) if len(ps) > 1 else ps[0]
            pv = jnp.dot(p.astype(BF16), v_ref[0:kc, hh * LANES:(hh + 1) * LANES],
                         preferred_element_type=F32)
            acc_ref[hh, rows, :] = alpha * acc_ref[hh, rows, :] + pv
            m_ref[hh, rows, :] = m_new

    @pl.when(ki < qi)
    def _():
        step(False)

    @pl.when(ki == qi)
    def _():
        step(True)
        lam = _diff_lambda(lam_ref, lam_init)
        for hh in range(hps):
            o = acc_ref[hh] / l_ref[hh]
            d = o[0:tq] - lam * o[tq:2 * tq]
            o_ref[:, hh * LANES:(hh + 1) * LANES] = (_rms_rows(d, g_ref[...])
                                                     * (1.0 - lam_init)).astype(o_ref.dtype)


def prompt_attention(q, k, v, lam_vecs, subln_g, bz, s, hd, lam_init):
    t, qw = q.shape
    vd = 2 * hd
    nh = qw // vd
    tq = _tile(s, 512)
    nq = s // tq
    assert vd == LANES and tq % min(tq, FLASH_ROWS) == 0 and min(tq, FLASH_ROWS) % LANES == 0
    hps = 2 if nh % 2 == 0 else 1
    bw = hps * vd
    pairs = [(a, c) for a in range(nq) for c in range(a + 1)]
    qi_tab = jnp.asarray([a for a, _ in pairs], jnp.int32)
    ki_tab = jnp.asarray([c for _, c in pairs], jnp.int32)
    grid_spec = pltpu.PrefetchScalarGridSpec(
        num_scalar_prefetch=2,
        grid=(bz, nh // hps, len(pairs)),
        in_specs=[pl.BlockSpec((4, hd), lambda b, h, p, qt, kt: (0, 0)),
                  pl.BlockSpec((1, vd), lambda b, h, p, qt, kt: (0, 0)),
                  pl.BlockSpec((tq, bw), lambda b, h, p, qt, kt: (b * nq + qt[p], h)),
                  pl.BlockSpec((tq, bw), lambda b, h, p, qt, kt: (b * nq + kt[p], h)),
                  pl.BlockSpec((tq, bw), lambda b, h, p, qt, kt: (b * nq + kt[p], h))],
        out_specs=pl.BlockSpec((tq, bw), lambda b, h, p, qt, kt: (b * nq + qt[p], h)),
        scratch_shapes=[pltpu.VMEM((hps, 2 * tq, vd), BF16), pltpu.VMEM((hps, 2 * tq, LANES), F32),
                        pltpu.VMEM((hps, 2 * tq, LANES), F32), pltpu.VMEM((hps, 2 * tq, vd), F32)],
    )
    return pl.pallas_call(
        functools.partial(_flash_kernel, tq=tq, hd=hd, hps=hps, lam_init=lam_init),
        grid_spec=grid_spec,
        out_shape=jax.ShapeDtypeStruct((t, nh * vd), BF16),
        compiler_params=_params("parallel", "parallel", "arbitrary"),
        name="prompt_attention",
    )(qi_tab, ki_tab, lam_vecs, subln_g.reshape(1, vd), q, k, v)


def _paged_kernel(pt_ref, lam_ref, g_ref, qidx_ref, q_ref, e_ref, msk_ref, *refs,
                  n_steps, pp, rpb, lam_init):
    k_refs, v_refs = refs[:pp], refs[pp:2 * pp]
    kn_ref, vn_ref, o_ref, m_ref, l_ref, acc_ref = refs[2 * pp:]
    p = pl.program_id(1)

    @pl.when(p == 0)
    def _():
        m_ref[...] = jnp.full(m_ref.shape, NEG_INF, F32)
        l_ref[...] = jnp.zeros(l_ref.shape, F32)
        acc_ref[...] = jnp.zeros(acc_ref.shape, F32)

    def update(kts, vfs, masked):
        q = q_ref[...]
        ss = [jnp.dot(q, kt[...].astype(BF16), preferred_element_type=F32) for kt in kts]
        if masked:
            col = lax.broadcasted_iota(jnp.int32, ss[0].shape, 1)
            ss = [jnp.where(col <= qidx_ref[...], s, NEG_INF) for s in ss]
        m_old = m_ref[...]
        m_new = jnp.maximum(m_old, jnp.max(functools.reduce(jnp.maximum, ss), axis=-1, keepdims=True))
        alpha = jnp.exp(m_old - m_new)
        prs = [jnp.exp(s - m_new) for s in ss]
        l_ref[...] = alpha * l_ref[...] + jnp.sum(functools.reduce(jnp.add, prs), axis=-1, keepdims=True)
        pv = None
        for pr, vf in zip(prs, vfs):
            pe = jnp.dot(pr.astype(BF16), e_ref[...], preferred_element_type=F32)
            pe = pe.astype(BF16) * msk_ref[...]
            part = jnp.dot(pe, vf[...].astype(BF16), preferred_element_type=F32)
            pv = part if pv is None else pv + part
        acc_ref[...] = alpha * acc_ref[...] + pv
        m_ref[...] = m_new

    @pl.when(p < n_steps)
    def _():
        for c0 in range(0, pp, PAGE_GROUP):
            update(k_refs[c0:c0 + PAGE_GROUP], v_refs[c0:c0 + PAGE_GROUP], False)

    @pl.when(p == n_steps)
    def _():
        update([kn_ref], [vn_ref], True)
        o = acc_ref[...] / l_ref[...]
        lam = _diff_lambda(lam_ref, lam_init)
        r = o.shape[0]
        d = o - lam * pltpu.roll(o, r - rpb // 2, axis=0)
        o_ref[...] = _rms_rows(d, g_ref[...]) * (1.0 - lam_init)


def sample_attention(q, k_new, v_new, cache_k, cache_v, layer, page_table, lam_vecs, subln_g,
                     bz, nq, hd, lam_init):
    vd = 2 * hd
    qw = q.shape[1]
    nh = qw // vd
    n_odd, n_pool, page = cache_k.shape[:3]
    n_pages = page_table.shape[1]
    assert nq <= page
    pp = next(c for c in (8, 4, 2, 1) if n_pages % c == 0 and n_pages // c >= min(2, n_pages))
    n_steps = n_pages // pp
    rpb = 2 * nq
    r = nh * rpb
    ckt = cache_k.transpose(0, 1, 3, 4, 2).reshape(n_odd, n_pool, qw, page)
    cvf = cache_v.reshape(n_odd, n_pool, page * nh, vd)
    q4 = q.reshape(bz, nq, nh, 2, hd).transpose(0, 2, 3, 1, 4)
    eye = jnp.eye(2 * nh, dtype=q.dtype).reshape(nh, 2, 1, 2 * nh, 1)
    qbd = (q4.reshape(bz, nh, 2, nq, 1, hd) * eye[None]).reshape(bz, r, qw)
    knt = jnp.pad(k_new.reshape(bz, nq, qw).transpose(0, 2, 1), ((0, 0), (0, 0), (0, page - nq)))
    vnf = jnp.pad(v_new.reshape(bz, nq * nh, vd), ((0, 0), (0, (page - nq) * nh), (0, 0)))
    qidx = (jnp.arange(r, dtype=jnp.int32) % nq).reshape(r, 1)
    expand = jnp.repeat(jnp.eye(page, dtype=BF16), nh, axis=1)
    own = (jnp.arange(page * nh)[None, :] % nh == jnp.arange(r)[:, None] // rpb).astype(BF16)
    pt = page_table.reshape(-1).astype(jnp.int32)

    def page_map(c):
        def index(b, p, pt):
            return (layer, pt[b * n_pages + jnp.minimum(p, n_steps - 1) * pp + c], 0, 0)
        return index

    grid_spec = pltpu.PrefetchScalarGridSpec(
        num_scalar_prefetch=1,
        grid=(bz, n_steps + 1),
        in_specs=([pl.BlockSpec((4, hd), lambda b, p, pt: (0, 0)),
                   pl.BlockSpec((1, vd), lambda b, p, pt: (0, 0)),
                   pl.BlockSpec((r, 1), lambda b, p, pt: (0, 0)),
                   pl.BlockSpec((None, r, qw), lambda b, p, pt: (b, 0, 0)),
                   pl.BlockSpec((page, page * nh), lambda b, p, pt: (0, 0)),
                   pl.BlockSpec((r, page * nh), lambda b, p, pt: (0, 0))]
                  + [pl.BlockSpec((None, None, qw, page), page_map(c)) for c in range(pp)]
                  + [pl.BlockSpec((None, None, page * nh, vd), page_map(c)) for c in range(pp)]
                  + [pl.BlockSpec((None, qw, page), lambda b, p, pt: (b, 0, 0)),
                     pl.BlockSpec((None, page * nh, vd), lambda b, p, pt: (b, 0, 0))]),
        out_specs=pl.BlockSpec((None, r, vd), lambda b, p, pt: (b, 0, 0)),
        scratch_shapes=[pltpu.VMEM((r, 1), F32), pltpu.VMEM((r, 1), F32), pltpu.VMEM((r, vd), F32)],
    )
    o = pl.pallas_call(
        functools.partial(_paged_kernel, n_steps=n_steps, pp=pp, rpb=rpb, lam_init=lam_init),
        grid_spec=grid_spec,
        out_shape=jax.ShapeDtypeStruct((bz, r, vd), F32),
        compiler_params=_params("parallel", "arbitrary"),
        name="sample_attention",
    )(pt, lam_vecs, subln_g.reshape(1, vd), qidx, qbd, expand, own,
      *([ckt] * pp), *([cvf] * pp), knt, vnf)
    o = o.reshape(bz, nh, 2, nq, vd)[:, :, 0]
    return o.transpose(0, 2, 1, 3).reshape(bz * nq, nh * vd).astype(BF16)


def kernel(x_prompt, x_sample, cache_k, cache_v, state_ssm_re, state_ssm_im, state_conv, page_table, norm_mix_pre, norm_mix_post, norm_ffn_pre, norm_ffn_post, w_in_even, ssm_lam_re, ssm_lam_im, ssm_log_dt, ssm_b_re, ssm_b_im, ssm_c_re, ssm_c_im, ssm_d, w_glu, conv_w, conv_b, conv_ln_g, conv_ln_b, w_out_even, w_qkv, lambda_q1, lambda_k1, lambda_q2, lambda_k2, subln_g, w_o, w_gate, w_up, w_down):
    depth = norm_mix_pre.shape[0]
    d_model = x_prompt.shape[-1]
    a_width = w_glu.shape[1]
    n_groups, n_state = ssm_lam_re.shape[1:]
    bw = conv_w.shape[2]
    page = cache_k.shape[2]
    head_dim = cache_k.shape[-1]
    q_width = cache_k.shape[-2] * head_dim
    past_len = page_table.shape[1] * page

    w_in_b, w_glu_b, w_out_b = w_in_even.astype(BF16), w_glu.astype(BF16), w_out_even.astype(BF16)
    w_qkv_b, w_o_b = w_qkv.astype(BF16), w_o.astype(BF16)
    w_gate_b, w_up_b, w_down_b = w_gate.astype(BF16), w_up.astype(BF16), w_down.astype(BF16)
    n_odd = w_qkv.shape[0]
    lam_vecs = jnp.stack([lambda_q1, lambda_k1, lambda_q2, lambda_k2], axis=1)

    def run_trunk(x, pos0, ssm_re0, ssm_im0, conv0, paged):
        bz, t, _ = x.shape
        h = x.reshape(bz * t, d_model)
        pos = jnp.tile(pos0 + jnp.arange(t, dtype=F32), bz).reshape(bz * t, 1)
        new_re, new_im, new_conv = [], [], []
        k_all = v_all = None
        for i in range(depth):
            j = i // 2
            if i % 2 == 0:
                proj = norm_matmul(h, norm_mix_pre[i], w_in_b, j)
                y_a, s_re, s_im = s5_mixer(
                    proj, bz, t, a_width, ssm_re0[j], ssm_im0[j],
                    ssm_lam_re[j], ssm_lam_im[j], ssm_log_dt[j], ssm_b_re[j], ssm_b_im[j],
                    ssm_c_re[j], ssm_c_im[j], ssm_d[j])
                y_a = s5_glu(y_a, w_glu_b, j)
                y_b, buf = conv_module(proj, conv0[j], conv_w[j], conv_b[j], conv_ln_g[j],
                                       conv_ln_b[j], bz, t, a_width)
                new_re.append(s_re)
                new_im.append(s_im)
                new_conv.append(buf)
                mix_in, w_mix = jnp.concatenate([y_a, y_b], axis=-1), w_out_b
            else:
                lam_init = 0.8 - 0.6 * math.exp(-0.3 * i)
                q, k_all, v_all, kb, vb = qkv_rope(h, norm_mix_pre[i], w_qkv_b, pos, q_width, head_dim,
                                                   j, n_odd, t, k_all, v_all)
                if paged:
                    assert k_all.ndim == 3
                    mix_in = sample_attention(q, k_all[j], v_all[j], cache_k, cache_v, j, page_table,
                                              lam_vecs[j], subln_g[j], bz, t, head_dim, lam_init)
                else:
                    mix_in = prompt_attention(q, kb, vb, lam_vecs[j], subln_g[j], bz, t, head_dim,
                                              lam_init)
                w_mix = w_o_b
            h = matmul_norm_residual(mix_in, w_mix, j, norm_mix_post[i], h)
            h = ffn(h, norm_ffn_pre[i], norm_ffn_post[i], w_gate_b, w_up_b, w_down_b, i)
        n_kh = q_width // head_dim
        if k_all.ndim == 4:
            new_k = k_all.reshape(n_odd, bz, n_kh, head_dim, t).transpose(0, 1, 4, 2, 3)
        else:
            new_k = k_all.reshape(n_odd, bz, t, n_kh, head_dim)
        new_v = v_all.reshape(n_odd, bz, t, -1, 2 * head_dim)
        return (h.reshape(bz, t, d_model), jnp.stack(new_re), jnp.stack(new_im), jnp.stack(new_conv),
                new_k, new_v)

    n_prompt = x_prompt.shape[0]
    n_even = state_ssm_re.shape[0]
    zero_ssm = jnp.zeros((n_even, n_prompt, n_groups, n_state), F32)
    zero_conv = jnp.zeros((n_even, n_prompt, conv_w.shape[1] - 1, bw), F32)
    y_p, re_p, im_p, conv_p, k_p, v_p = run_trunk(x_prompt, 0.0, zero_ssm, zero_ssm, zero_conv, False)
    y_s, re_s, im_s, conv_s, k_s, v_s = run_trunk(x_sample, float(past_len), state_ssm_re, state_ssm_im,
                                                  state_conv, True)
    return (y_p, y_s, re_p, im_p, conv_p, k_p, v_p, re_s, im_s, conv_s, k_s, v_s)
```

```python
import functools
import math

import jax
import jax.numpy as jnp
from jax import lax
from jax.experimental import pallas as pl
from jax.experimental.pallas import tpu as pltpu

F32 = jnp.float32
BF16 = jnp.bfloat16
EPS = 1e-6
NEG_INF = -1e30
ROPE_THETA = 500000.0
S5_CHUNK = 16
S5_ROWS = 256
CONV_HALO = 32
LANES = 128
FLASH_ROWS = 256
PAGE_GROUP = 4
VMEM_LIMIT = 56 * 1024 * 1024
HI = lax.Precision.HIGHEST


def _tile(n, pref):
    if n <= pref:
        return n
    t = pref
    while t >= 8:
        if n % t == 0:
            return t
        t //= 2
    return n


def _params(*sem):
    return pltpu.CompilerParams(dimension_semantics=sem, vmem_limit_bytes=VMEM_LIMIT)


def _rms_rows(x, g, eps=EPS):
    ms = jnp.mean(x * x, axis=-1, keepdims=True)
    return x * lax.rsqrt(ms + eps) * g


def _norm_matmul_kernel(x_ref, g_ref, w_ref, o_ref, hn_ref):
    @pl.when(pl.program_id(1) == 0)
    def _():
        hn_ref[...] = _rms_rows(x_ref[...], g_ref[...]).astype(BF16)

    o_ref[...] = jnp.dot(hn_ref[...], w_ref[...], preferred_element_type=F32)


def norm_matmul(x, g, w, layer):
    t, d = x.shape
    n = w.shape[2]
    tm, tn = _tile(t, 1024), _tile(n, 1024)
    return pl.pallas_call(
        _norm_matmul_kernel,
        grid=(t // tm, n // tn),
        in_specs=[pl.BlockSpec((tm, d), lambda i, j: (i, 0)),
                  pl.BlockSpec((1, d), lambda i, j: (0, 0)),
                  pl.BlockSpec((None, d, tn), lambda i, j: (layer, 0, j))],
        out_specs=pl.BlockSpec((tm, tn), lambda i, j: (i, j)),
        out_shape=jax.ShapeDtypeStruct((t, n), F32),
        scratch_shapes=[pltpu.VMEM((tm, d), BF16)],
        compiler_params=_params("parallel", "arbitrary"),
        name="norm_matmul",
    )(x, g.reshape(1, d), w)


def _qkv_kernel(x_ref, g_ref, pos_ref, invf_ref, sel_ref, w_ref, *rest, nq_tiles, scale, n_prev, k_major):
    q_ref, k_ref, v_ref, kb_ref, vb_ref, hn_ref, tab_ref = rest[n_prev:]
    j = pl.program_id(1)

    @pl.when(j == 0)
    def _():
        hn_ref[...] = _rms_rows(x_ref[...], g_ref[...]).astype(BF16)
        ang = pos_ref[...] * invf_ref[...]
        c, s = jnp.cos(ang), jnp.sin(ang)
        tab_ref[0] = jnp.where(sel_ref[0:1, :] > 0.5, c, 1.0)
        tab_ref[1] = s * sel_ref[1:2, :]
        tab_ref[2] = s * sel_ref[2:3, :]

    tn = w_ref.shape[1]
    cw = min(tn, 2 * LANES)
    half = LANES // 16

    def rope(y):
        parts = []
        for c0 in range(0, y.shape[1], LANES):
            x = y[:, c0:c0 + LANES]
            parts.append(x * tab_ref[0] + pltpu.roll(x, half, axis=1) * tab_ref[1]
                         + pltpu.roll(x, LANES - half, axis=1) * tab_ref[2])
        return jnp.concatenate(parts, axis=1) if len(parts) > 1 else parts[0]

    def sweep(emit):
        hn = hn_ref[...]
        chunks = list(range(0, tn, cw))
        y_next = jnp.dot(hn, w_ref[:, 0:cw], preferred_element_type=F32)
        for n, c0 in enumerate(chunks):
            y = y_next
            if n + 1 < len(chunks):
                y_next = jnp.dot(hn, w_ref[:, c0 + cw:c0 + 2 * cw], preferred_element_type=F32)
            emit(slice(c0, c0 + cw), y)

    @pl.when(j < nq_tiles)
    def _():
        def emit(cols, y):
            q_ref[:, cols] = (rope(y) * scale).astype(BF16)
        sweep(emit)

    @pl.when(jnp.logical_and(j >= nq_tiles, j < 2 * nq_tiles))
    def _():
        def emit(cols, y):
            r = rope(y)
            if k_major:
                k_ref[cols, :] = r.T
            else:
                k_ref[:, cols] = r
            kb_ref[:, cols] = r.astype(BF16)
        sweep(emit)

    @pl.when(j >= 2 * nq_tiles)
    def _():
        def emit(cols, y):
            v_ref[:, cols] = y
            vb_ref[:, cols] = y.astype(BF16)
        sweep(emit)


def qkv_rope(x, g, w, pos, q_width, head_dim, layer, n_layers, seq, k_all=None, v_all=None):
    t, d = x.shape
    n = w.shape[2]
    v_width = n - 2 * q_width
    rot = head_dim // 4
    assert rot == 16 and 128 % head_dim == 0
    tm = _tile(t, 512)
    tn = _tile(math.gcd(q_width, v_width), 1024)
    nq, nv = q_width // tn, v_width // tn
    k_major = seq % tm == 0 and tm % LANES == 0
    tps = seq // tm if k_major else 1
    prev = [] if k_all is None else [k_all, v_all]
    lane = jnp.arange(128) % head_dim
    inv_freq = ROPE_THETA ** (-jnp.arange(rot // 2, dtype=F32) * 2.0 / rot)
    invf = jnp.where(lane < rot, inv_freq[lane % (rot // 2)], 0.0).reshape(1, 128).astype(F32)
    sel = jnp.stack([(lane < rot).astype(F32),
                     jnp.logical_and(lane >= rot // 2, lane < rot).astype(F32),
                     -(lane < rot // 2).astype(F32)])

    def clip(j, lo, cnt):
        return jnp.clip(j - lo, 0, cnt - 1)

    if k_major:
        k_spec = pl.BlockSpec((None, None, tn, tm), lambda i, j: (layer, i // tps, clip(j, nq, nq), i % tps))
        k_shape = jax.ShapeDtypeStruct((n_layers, t // seq, q_width, seq), F32)
    else:
        k_spec = pl.BlockSpec((None, tm, tn), lambda i, j: (layer, i, clip(j, nq, nq)))
        k_shape = jax.ShapeDtypeStruct((n_layers, t, q_width), F32)
    outs = pl.pallas_call(
        functools.partial(_qkv_kernel, nq_tiles=nq, scale=head_dim ** -0.5, n_prev=len(prev), k_major=k_major),
        grid=(t // tm, 2 * nq + nv),
        in_specs=[pl.BlockSpec((tm, d), lambda i, j: (i, 0)),
                  pl.BlockSpec((1, d), lambda i, j: (0, 0)),
                  pl.BlockSpec((tm, 1), lambda i, j: (i, 0)),
                  pl.BlockSpec((1, 128), lambda i, j: (0, 0)),
                  pl.BlockSpec((3, 128), lambda i, j: (0, 0)),
                  pl.BlockSpec((None, d, tn), lambda i, j: (layer, 0, j))]
                 + [pl.BlockSpec(memory_space=pl.ANY)] * len(prev),
        out_specs=[pl.BlockSpec((tm, tn), lambda i, j: (i, clip(j, 0, nq))),
                   k_spec,
                   pl.BlockSpec((None, tm, tn), lambda i, j: (layer, i, clip(j, 2 * nq, nv))),
                   pl.BlockSpec((tm, tn), lambda i, j: (i, clip(j, nq, nq))),
                   pl.BlockSpec((tm, tn), lambda i, j: (i, clip(j, 2 * nq, nv)))],
        out_shape=[jax.ShapeDtypeStruct((t, q_width), BF16),
                   k_shape,
                   jax.ShapeDtypeStruct((n_layers, t, v_width), F32),
                   jax.ShapeDtypeStruct((t, q_width), BF16),
                   jax.ShapeDtypeStruct((t, v_width), BF16)],
        scratch_shapes=[pltpu.VMEM((tm, d), BF16), pltpu.VMEM((3, tm, 128), F32)],
        input_output_aliases={6 + n: 1 + n for n in range(len(prev))},
        compiler_params=_params("parallel", "arbitrary"),
        name="qkv_rope",
    )(x, g.reshape(1, d), pos, invf, sel, w, *prev)
    return outs


def _matmul_norm_res_kernel(*refs, nj, tn, n_in):
    a_refs = refs[:n_in]
    w_ref, g_ref, h_ref, o_ref, acc_ref = refs[n_in:]
    j = pl.program_id(1)
    y, k0 = None, 0
    for a_ref in a_refs:
        kp = a_ref.shape[1]
        part = jnp.dot(a_ref[...], w_ref[k0:k0 + kp, :], preferred_element_type=F32)
        y = part if y is None else y + part
        k0 += kp
    acc_ref[j] = y

    @pl.when(j == nj - 1)
    def _():
        ss = jnp.zeros((acc_ref.shape[1], 1), F32)
        for jj in range(nj):
            y = acc_ref[jj]
            ss = ss + jnp.sum(y * y, axis=-1, keepdims=True)
        inv = lax.rsqrt(ss / (nj * tn) + EPS)
        for jj in range(nj):
            sl = slice(jj * tn, (jj + 1) * tn)
            o_ref[:, sl] = h_ref[:, sl] + acc_ref[jj] * inv * g_ref[:, sl]


def matmul_norm_residual(parts, w, layer, g, h):
    t = parts[0].shape[0]
    k = sum(p.shape[1] for p in parts)
    d = w.shape[2]
    assert w.shape[1] == k
    tm, tn = _tile(t, 512), _tile(d, 1024)
    nj = d // tn
    return pl.pallas_call(
        functools.partial(_matmul_norm_res_kernel, nj=nj, tn=tn, n_in=len(parts)),
        grid=(t // tm, nj),
        in_specs=[pl.BlockSpec((tm, p.shape[1]), lambda i, j: (i, 0)) for p in parts]
                 + [pl.BlockSpec((None, k, tn), lambda i, j: (layer, 0, j)),
                  pl.BlockSpec((1, d), lambda i, j: (0, 0)),
                  pl.BlockSpec((tm, d), lambda i, j: (i, 0))],
        out_specs=pl.BlockSpec((tm, d), lambda i, j: (i, 0)),
        out_shape=jax.ShapeDtypeStruct((t, d), F32),
        scratch_shapes=[pltpu.VMEM((nj, tm, tn), F32)],
        compiler_params=_params("parallel", "arbitrary"),
        name="matmul_norm_residual",
    )(*parts, w, g.reshape(1, d), h)


def _ffn_kernel(h_ref, gpre_ref, gpost_ref, wg_ref, wu_ref, wd_ref, o_ref, hn_ref, *, nf):
    f = pl.program_id(1)

    @pl.when(f == 0)
    def _():
        hn_ref[...] = _rms_rows(h_ref[...], gpre_ref[...]).astype(BF16)
        o_ref[...] = jnp.zeros(o_ref.shape, F32)

    hn = hn_ref[...]
    gate = jnp.dot(hn, wg_ref[...], preferred_element_type=F32)
    up = jnp.dot(hn, wu_ref[...], preferred_element_type=F32)
    act = (gate * jax.nn.sigmoid(gate) * up).astype(BF16)
    o_ref[...] += jnp.dot(act, wd_ref[...], preferred_element_type=F32)

    @pl.when(f == nf - 1)
    def _():
        o_ref[...] = h_ref[...] + _rms_rows(o_ref[...], gpost_ref[...])


def ffn(h, g_pre, g_post, wg, wu, wd, layer):
    t, d = h.shape
    fh = wg.shape[2]
    tm = _tile(t, 1024)
    tf = next((c for c in (512, 256) if fh % c == 0), fh)
    nf = fh // tf
    return pl.pallas_call(
        functools.partial(_ffn_kernel, nf=nf),
        grid=(t // tm, nf),
        in_specs=[pl.BlockSpec((tm, d), lambda i, f: (i, 0), pipeline_mode=pl.Buffered(1)),
                  pl.BlockSpec((1, d), lambda i, f: (0, 0)),
                  pl.BlockSpec((1, d), lambda i, f: (0, 0)),
                  pl.BlockSpec((None, d, tf), lambda i, f: (layer, 0, f)),
                  pl.BlockSpec((None, d, tf), lambda i, f: (layer, 0, f)),
                  pl.BlockSpec((None, tf, d), lambda i, f: (layer, f, 0))],
        out_specs=pl.BlockSpec((tm, d), lambda i, f: (i, 0)),
        out_shape=jax.ShapeDtypeStruct((t, d), F32),
        scratch_shapes=[pltpu.VMEM((tm, d), BF16)],
        compiler_params=_params("parallel", "arbitrary"),
        name="ffn",
    )(h, g_pre.reshape(1, d), g_post.reshape(1, d), wg, wu, wd)


def _glu_kernel(y_ref, w_ref, o_ref):
    y = y_ref[...]
    z = jnp.dot(y.astype(BF16), w_ref[...], preferred_element_type=F32)
    o_ref[...] = (y * jax.nn.sigmoid(z)).astype(o_ref.dtype)


def s5_glu(y, w, layer):
    t, a = y.shape
    tm = _tile(t, 1024)
    return pl.pallas_call(
        _glu_kernel,
        grid=(t // tm,),
        in_specs=[pl.BlockSpec((tm, a), lambda i: (i, 0)),
                  pl.BlockSpec((None, a, a), lambda i: (layer, 0, 0))],
        out_specs=pl.BlockSpec((tm, a), lambda i: (i, 0)),
        out_shape=jax.ShapeDtypeStruct((t, a), BF16),
        compiler_params=_params("parallel"),
        name="s5_glu",
    )(y, w)


def _s5_prep_kernel(lam_row_ref, lam_col_ref, bt_ref, ct_ref, e_ref, tt_ref,
                    g0_ref, p_ref, q_ref, al_ref, *, chunk, gq):
    for gi in range(gq):
        _s5_prep_group(lam_row_ref.at[gi], lam_col_ref.at[gi], bt_ref.at[gi], ct_ref.at[gi], e_ref, tt_ref,
                       g0_ref.at[gi], p_ref.at[gi], q_ref.at[gi], al_ref.at[gi], chunk=chunk)


def _s5_prep_group(lam_row_ref, lam_col_ref, bt_ref, ct_ref, e_ref, tt_ref,
                   g0_ref, p_ref, q_ref, al_ref, *, chunk):
    lr, li, ldt = lam_row_ref[0:1, :], lam_row_ref[1:2, :], lam_row_ref[2:3, :]
    dt = jnp.exp(ldt)
    mag = jnp.exp(lr * dt)
    ab_re, ab_im = mag * jnp.cos(li * dt), mag * jnp.sin(li * dt)
    den = lr * lr + li * li
    num_re = ab_re - 1.0
    coef_re = (num_re * lr + ab_im * li) / den
    coef_im = (ab_im * lr - num_re * li) / den
    bt_re, bt_im = bt_ref[0], bt_ref[1]
    bb_re = coef_re * bt_re - coef_im * bt_im
    bb_im = coef_re * bt_im + coef_im * bt_re

    magl = jnp.exp(lr * dt * chunk)
    al_ref[0:1, :] = magl * jnp.cos(li * dt * chunk)
    al_ref[1:2, :] = magl * jnp.sin(li * dt * chunk)

    e = e_ref[...]
    pm = jnp.exp(lr * dt * e)
    pw_re, pw_im = pm * jnp.cos(li * dt * e), pm * jnp.sin(li * dt * e)
    bbt_re = jnp.concatenate([bb_re] * chunk, axis=0)
    bbt_im = jnp.concatenate([bb_im] * chunk, axis=0)
    n_state = pw_re.shape[1]
    p_ref[:, 0:n_state] = pw_re * bbt_re - pw_im * bbt_im
    p_ref[:, n_state:2 * n_state] = pw_re * bbt_im + pw_im * bbt_re

    lrc, lic, ldtc = lam_col_ref[0], lam_col_ref[1], lam_col_ref[2]
    dtc = jnp.exp(ldtc)
    tt = tt_ref[...]
    fm = jnp.exp(lrc * dtc * tt)
    fw_re, fw_im = fm * jnp.cos(lic * dtc * tt), fm * jnp.sin(lic * dtc * tt)
    ct_re, ct_im = ct_ref[0], ct_ref[1]
    f_re = ct_re * fw_re - ct_im * fw_im
    f_im = ct_re * fw_im + ct_im * fw_re
    g0_ref[...] = (jnp.dot(bb_re, f_re, precision=HI, preferred_element_type=F32)
                   - jnp.dot(bb_im, f_im, precision=HI, preferred_element_type=F32))
    magc = jnp.exp(lrc * dtc)
    abc_re, abc_im = magc * jnp.cos(lic * dtc), magc * jnp.sin(lic * dtc)
    q_ref[0] = (f_re * abc_re - f_im * abc_im).astype(q_ref.dtype)
    q_ref[1] = (-(f_re * abc_im + f_im * abc_re)).astype(q_ref.dtype)


def _s5_assemble_kernel(a_ref, p_ref, q_ref, tw_ref, tp_ref, rgw_ref, rgs_ref, cgw_ref, cgs_ref,
                        wp_ref, qb_ref, *, nw, ns):
    a = a_ref[...].astype(BF16)
    pm = p_ref[...].astype(BF16)
    q = q_ref[...]
    cs = min(nw, 512)
    for c0 in range(0, nw, cs):
        cols = slice(c0, c0 + cs)
        w = jnp.dot(a, tw_ref[:, cols], preferred_element_type=F32)
        wp_ref[:, cols] = jnp.where(rgw_ref[...] == cgw_ref[:, cols], w, 0.0).astype(BF16)
        qv = jnp.dot(q, tw_ref[:, cols], preferred_element_type=F32)
        qb_ref[:, cols] = jnp.where(rgs_ref[...] == cgw_ref[:, cols], qv, 0.0).astype(BF16)
    cs = min(ns, 512)
    for c0 in range(0, ns, cs):
        cols = slice(c0, c0 + cs)
        pv = jnp.dot(pm, tp_ref[:, cols], preferred_element_type=F32)
        wp_ref[:, nw + c0:nw + c0 + cs] = jnp.where(rgw_ref[...] == cgs_ref[:, cols], pv, 0.0).astype(BF16)


def _s5_main_kernel(u_ref, w_ref, q_ref, al_ref, h0_ref, d_ref,
                    y_ref, hn_ref, bc_ref, xs_ref, *, chunk, nct, bpt):
    rows = bc_ref.shape[0]
    nw = chunk * LANES
    half = bc_ref.shape[1] // 2
    us = [u_ref[pl.ds(s, rows, stride=chunk), :] for s in range(chunk)]
    ucat = jnp.concatenate(us, axis=1).astype(BF16)
    r = jnp.dot(ucat, w_ref[...], preferred_element_type=F32)
    bc_ref[...] = r[:, nw:]
    a_re, a_im = al_ref[:, 0:half], al_ref[:, half:]

    def step(j, carry):
        nxt = []
        for bl in range(bpt):
            x_re, x_im = carry[2 * bl], carry[2 * bl + 1]
            row = pl.ds(bl * nct + j, 1)
            xs_ref[row, 0:half] = x_re
            xs_ref[row, half:] = x_im
            nxt.append(a_re * x_re - a_im * x_im + bc_ref[row, 0:half])
            nxt.append(a_re * x_im + a_im * x_re + bc_ref[row, half:])
        return tuple(nxt)

    init = tuple(h0_ref[bl:bl + 1, sl] for bl in range(bpt) for sl in (slice(0, half), slice(half, 2 * half)))
    fin = lax.fori_loop(0, nct, step, init)
    for bl in range(bpt):
        hn_ref[bl:bl + 1, 0:half] = fin[2 * bl]
        hn_ref[bl:bl + 1, half:] = fin[2 * bl + 1]
    yc = jnp.dot(xs_ref[...].astype(BF16), q_ref[...], preferred_element_type=F32)
    for t in range(chunk):
        sl = slice(t * LANES, (t + 1) * LANES)
        y_ref[pl.ds(t, rows, stride=chunk), :] = jax.nn.gelu(r[:, sl] + yc[:, sl] + d_ref[...] * us[t])


def s5_mixer(proj, bz, t, a, h0_re, h0_im, lam_re, lam_im, log_dt, b_re, b_im, c_re, c_im, d_skip):
    g, p = lam_re.shape
    gc = a // g
    chunk = S5_CHUNK if t % S5_CHUNK == 0 else t
    nct = t // chunk
    lc = chunk * gc

    lam_row = jnp.stack([lam_re, lam_im, jnp.broadcast_to(log_dt[:, None], (g, p))], axis=1)
    lam_col = jnp.broadcast_to(lam_row[:, :, :, None], (g, 3, p, lc))
    bt = jnp.stack([b_re, b_im], axis=1).transpose(0, 1, 3, 2)
    ct = jnp.tile(jnp.stack([c_re, c_im], axis=1).transpose(0, 1, 3, 2), (1, 1, 1, chunk))

    step = jnp.arange(lc, dtype=jnp.int32) // gc
    e_col = (chunk - 1 - step).astype(F32).reshape(lc, 1)
    tt_row = step.astype(F32).reshape(1, lc)

    gq = next(c for c in (8, 4, 2, 1) if g % c == 0)
    g0, pmat, qmat, al = pl.pallas_call(
        functools.partial(_s5_prep_kernel, chunk=chunk, gq=gq),
        grid=(g // gq,),
        in_specs=[pl.BlockSpec((gq, 3, p), lambda i: (i, 0, 0)),
                  pl.BlockSpec((gq, 3, p, lc), lambda i: (i, 0, 0, 0)),
                  pl.BlockSpec((gq, 2, gc, p), lambda i: (i, 0, 0, 0)),
                  pl.BlockSpec((gq, 2, p, lc), lambda i: (i, 0, 0, 0)),
                  pl.BlockSpec((lc, 1), lambda i: (0, 0)),
                  pl.BlockSpec((1, lc), lambda i: (0, 0))],
        out_specs=[pl.BlockSpec((gq, gc, lc), lambda i: (i, 0, 0)),
                   pl.BlockSpec((gq, lc, 2 * p), lambda i: (i, 0, 0)),
                   pl.BlockSpec((gq, 2, p, lc), lambda i: (i, 0, 0, 0)),
                   pl.BlockSpec((gq, 2, p), lambda i: (i, 0, 0))],
        out_shape=[jax.ShapeDtypeStruct((g, gc, lc), F32),
                   jax.ShapeDtypeStruct((g, lc, 2 * p), F32),
                   jax.ShapeDtypeStruct((g, 2, p, lc), BF16),
                   jax.ShapeDtypeStruct((g, 2, p), F32)],
        compiler_params=_params("parallel"),
        name="s5_prep",
    )(lam_row, lam_col, bt, ct, e_col, tt_row)

    g0p = jnp.pad(g0.reshape(g, gc, chunk, gc), ((0, 0), (0, 0), (chunk, 0), (0, 0)))
    wt = jnp.stack([g0p[:, :, chunk - s:2 * chunk - s, :] for s in range(chunk)], axis=1)

    assert LANES % gc == 0 and g % (LANES // gc) == 0
    gb = LANES // gc
    nb = g // gb
    nw, ns = chunk * LANES, 2 * gb * p
    a_rows = wt.reshape(nb, gb, chunk, gc, lc).transpose(0, 2, 1, 3, 4).reshape(nb, nw, lc)
    p_rows = pmat.reshape(nb, gb, chunk, gc, 2 * p).transpose(0, 2, 1, 3, 4).reshape(nb, nw, 2 * p)
    q_rows = qmat.reshape(nb, gb, 2, p, lc).transpose(0, 2, 1, 3, 4).reshape(nb, ns, lc)
    tile_w = jnp.kron(jnp.eye(chunk, dtype=BF16), jnp.tile(jnp.eye(gc, dtype=BF16), (1, gb)))
    tile_p = jnp.kron(jnp.eye(2, dtype=BF16), jnp.tile(jnp.eye(p, dtype=BF16), (1, gb)))
    grp_w = (jnp.arange(nw, dtype=jnp.int32) // gc) % gb
    grp_s = (jnp.arange(ns, dtype=jnp.int32) // p) % gb
    wp, qbig = pl.pallas_call(
        functools.partial(_s5_assemble_kernel, nw=nw, ns=ns),
        grid=(nb,),
        in_specs=[pl.BlockSpec((None, nw, lc), lambda i: (i, 0, 0)),
                  pl.BlockSpec((None, nw, 2 * p), lambda i: (i, 0, 0)),
                  pl.BlockSpec((None, ns, lc), lambda i: (i, 0, 0)),
                  pl.BlockSpec((lc, nw), lambda i: (0, 0)),
                  pl.BlockSpec((2 * p, ns), lambda i: (0, 0)),
                  pl.BlockSpec((nw, 1), lambda i: (0, 0)),
                  pl.BlockSpec((ns, 1), lambda i: (0, 0)),
                  pl.BlockSpec((1, nw), lambda i: (0, 0)),
                  pl.BlockSpec((1, ns), lambda i: (0, 0))],
        out_specs=[pl.BlockSpec((None, nw, nw + ns), lambda i: (i, 0, 0)),
                   pl.BlockSpec((None, ns, nw), lambda i: (i, 0, 0))],
        out_shape=[jax.ShapeDtypeStruct((nb, nw, nw + ns), BF16),
                   jax.ShapeDtypeStruct((nb, ns, nw), BF16)],
        compiler_params=_params("parallel"),
        name="s5_assemble",
    )(a_rows, p_rows, q_rows, tile_w, tile_p, grp_w.reshape(nw, 1), grp_s.reshape(ns, 1),
      grp_w.reshape(1, nw), grp_s.reshape(1, ns))
    al_big = al.reshape(nb, gb, 2, p).transpose(0, 2, 1, 3).reshape(nb, 1, ns)

    bpt = max(1, min(bz, S5_ROWS // nct))
    assert bz % bpt == 0
    rt = bz // bpt
    rows = bpt * nct
    h0 = jnp.stack([h0_re, h0_im], axis=0).reshape(2, rt, bpt, nb, gb, p)
    h0 = h0.transpose(3, 1, 2, 0, 4, 5).reshape(nb, rt, bpt, ns)

    y, hn = pl.pallas_call(
        functools.partial(_s5_main_kernel, chunk=chunk, nct=nct, bpt=bpt),
        grid=(nb, rt),
        in_specs=[pl.BlockSpec((rows * chunk, LANES), lambda i, r: (r, i)),
                  pl.BlockSpec((None, nw, nw + ns), lambda i, r: (i, 0, 0)),
                  pl.BlockSpec((None, ns, nw), lambda i, r: (i, 0, 0)),
                  pl.BlockSpec((None, 1, ns), lambda i, r: (i, 0, 0)),
                  pl.BlockSpec((None, None, bpt, ns), lambda i, r: (i, r, 0, 0)),
                  pl.BlockSpec((None, 1, LANES), lambda i, r: (i, 0, 0))],
        out_specs=[pl.BlockSpec((rows * chunk, LANES), lambda i, r: (r, i)),
                   pl.BlockSpec((None, None, bpt, ns), lambda i, r: (i, r, 0, 0))],
        out_shape=[jax.ShapeDtypeStruct((bz * t, a), F32),
                   jax.ShapeDtypeStruct((nb, rt, bpt, ns), F32)],
        scratch_shapes=[pltpu.VMEM((rows, ns), F32)] * 2,
        compiler_params=_params("parallel", "arbitrary"),
        name="s5_main",
    )(proj, wp, qbig, al_big, h0, d_skip.reshape(nb, 1, LANES))

    hn = hn.reshape(nb, bz, 2, gb, p).transpose(2, 1, 0, 3, 4).reshape(2, bz, g, p)
    return y, hn[0], hn[1]


def _conv_kernel(v_ref, g_ref, buf_ref, w_ref, b_ref, lg_ref, lb_ref, y_ref, nbuf_ref, win_ref, acc_ref,
                 *, tt, width, nt):
    i = pl.program_id(1)
    keep = width - 1
    off = CONV_HALO - keep

    @pl.when(i == 0)
    def _():
        win_ref[0:off, :] = jnp.zeros((off, win_ref.shape[1]), F32)
        win_ref[off:CONV_HALO, :] = buf_ref[...]

    v = v_ref[...]
    win_ref[CONV_HALO:CONV_HALO + tt, :] = v * jax.nn.sigmoid(g_ref[...])

    bw = win_ref.shape[1]
    rb = min(tt, 64)
    for r0 in range(0, tt, rb):
        for c0 in range(0, bw, 128):
            acc = jnp.zeros((rb, 128), F32)
            for k in range(width):
                acc = acc + w_ref[k:k + 1, c0:c0 + 128] * win_ref[off + r0 + k:off + r0 + k + rb, c0:c0 + 128]
            acc_ref[r0:r0 + rb, c0:c0 + 128] = acc

    y = acc_ref[...] + b_ref[...]
    mu = jnp.mean(y, axis=-1, keepdims=True)
    var = jnp.mean(jnp.square(y - mu), axis=-1, keepdims=True)
    z = (y - mu) * lax.rsqrt(var + EPS) * lg_ref[...] + lb_ref[...]
    y_ref[...] = (z * jax.nn.sigmoid(z)).astype(y_ref.dtype)

    @pl.when(i == nt - 1)
    def _():
        nbuf_ref[...] = win_ref[tt + off:tt + CONV_HALO, :]

    tail = win_ref[tt:tt + CONV_HALO, :]
    win_ref[0:CONV_HALO, :] = tail


def conv_module(proj, buf, conv_w, conv_b, ln_g, ln_b, bz, t, a_width):
    width, bw = conv_w.shape
    assert a_width % bw == 0 and width - 1 <= CONV_HALO
    tt = _tile(t, 128)
    nt = t // tt
    vcol = a_width // bw
    proj = proj.reshape(bz, t, proj.shape[-1])
    y, nbuf = pl.pallas_call(
        functools.partial(_conv_kernel, tt=tt, width=width, nt=nt),
        grid=(bz, nt),
        in_specs=[pl.BlockSpec((None, tt, bw), lambda b, i: (b, i, vcol)),
                  pl.BlockSpec((None, tt, bw), lambda b, i: (b, i, vcol + 1)),
                  pl.BlockSpec((None, width - 1, bw), lambda b, i: (b, 0, 0)),
                  pl.BlockSpec((width, bw), lambda b, i: (0, 0)),
                  pl.BlockSpec((1, bw), lambda b, i: (0, 0)),
                  pl.BlockSpec((1, bw), lambda b, i: (0, 0)),
                  pl.BlockSpec((1, bw), lambda b, i: (0, 0))],
        out_specs=[pl.BlockSpec((None, tt, bw), lambda b, i: (b, i, 0)),
                   pl.BlockSpec((None, width - 1, bw), lambda b, i: (b, 0, 0))],
        out_shape=[jax.ShapeDtypeStruct((bz, t, bw), BF16),
                   jax.ShapeDtypeStruct((bz, width - 1, bw), F32)],
        scratch_shapes=[pltpu.VMEM((CONV_HALO + tt, bw), F32), pltpu.VMEM((tt, bw), F32)],
        compiler_params=_params("parallel", "arbitrary"),
        name="conv_module",
    )(proj, proj, buf, conv_w, conv_b.reshape(1, bw), ln_g.reshape(1, bw), ln_b.reshape(1, bw))
    return y.reshape(bz * t, bw), nbuf


def _diff_lambda(lam_ref, lam_init):
    s1 = jnp.sum(lam_ref[0:1, :] * lam_ref[1:2, :], axis=-1, keepdims=True)
    s2 = jnp.sum(lam_ref[2:3, :] * lam_ref[3:4, :], axis=-1, keepdims=True)
    return jnp.exp(s1) - jnp.exp(s2) + lam_init


def _flash_kernel(qi_ref, ki_ref, lam_ref, g_ref, q_ref, k_ref, v_ref, o_ref, qs_ref, m_ref, l_ref, acc_ref,
                  *, tq, hd, hps, lam_init):
    pair = pl.program_id(2)
    qi, ki = qi_ref[pair], ki_ref[pair]
    rc = min(tq, FLASH_ROWS)

    @pl.when(ki == 0)
    def _():
        for hh in range(hps):
            q = q_ref[:, hh * LANES:(hh + 1) * LANES]
            lane = lax.broadcasted_iota(jnp.int32, q.shape, 1)
            zero = jnp.zeros_like(q)
            qs_ref[hh, 0:tq, :] = jnp.where(lane < hd, q, zero)
            qs_ref[hh, tq:2 * tq, :] = jnp.where(lane >= hd, q, zero)
        m_ref[...] = jnp.full(m_ref.shape, NEG_INF, F32)
        l_ref[...] = jnp.zeros(l_ref.shape, F32)
        acc_ref[...] = jnp.zeros(acc_ref.shape, F32)

    def step(masked):
        items = [(hh, c) for hh in range(hps) for c in range(2 * tq // rc)]

        def scores(hh, c):
            q0 = (c * rc) % tq
            kc = q0 + rc if masked else tq
            return lax.dot_general(qs_ref[hh, c * rc:(c + 1) * rc, :],
                                   k_ref[0:kc, hh * LANES:(hh + 1) * LANES],
                                   (((1,), (1,)), ((), ())), preferred_element_type=F32)

        s_next = scores(*items[0])
        for n, (hh, c) in enumerate(items):
            rows = slice(c * rc, (c + 1) * rc)
            q0 = (c * rc) % tq
            kc = q0 + rc if masked else tq
            s = s_next
            if n + 1 < len(items):
                s_next = scores(*items[n + 1])
            if masked:
                row = lax.broadcasted_iota(jnp.int32, s.shape, 0) + q0
                col = lax.broadcasted_iota(jnp.int32, s.shape, 1)
                s = jnp.where(col <= row, s, NEG_INF)
            cols = [s[:, j:j + LANES] for j in range(0, kc, LANES)]
            m_old = m_ref[hh, rows, :]
            m_new = jnp.maximum(m_old, jnp.max(functools.reduce(jnp.maximum, cols), axis=-1, keepdims=True))
            alpha = jnp.exp(m_old - m_new)
            ps = [jnp.exp(cj - m_new) for cj in cols]
            l_ref[hh, rows, :] = alpha * l_ref[hh, rows, :] + jnp.sum(functools.reduce(jnp.add, ps), axis=-1,
                                                                       keepdims=True)
            p = jnp.concatenate(ps, axis=1) if len(ps) > 1 else ps[0]
            pv = jnp.dot(p.astype(BF16), v_ref[0:kc, hh * LANES:(hh + 1) * LANES],
                         preferred_element_type=F32)
            acc_ref[hh, rows, :] = alpha * acc_ref[hh, rows, :] + pv
            m_ref[hh, rows, :] = m_new

    @pl.when(ki < qi)
    def _():
        step(False)

    @pl.when(ki == qi)
    def _():
        step(True)
        lam = _diff_lambda(lam_ref, lam_init)
        for hh in range(hps):
            o = acc_ref[hh] / l_ref[hh]
            d = o[0:tq] - lam * o[tq:2 * tq]
            o_ref[:, hh * LANES:(hh + 1) * LANES] = (_rms_rows(d, g_ref[...])
                                                     * (1.0 - lam_init)).astype(o_ref.dtype)


def prompt_attention(q, k, v, lam_vecs, subln_g, bz, s, hd, lam_init):
    t, qw = q.shape
    vd = 2 * hd
    nh = qw // vd
    tq = _tile(s, 512)
    nq = s // tq
    assert vd == LANES and tq % min(tq, FLASH_ROWS) == 0 and min(tq, FLASH_ROWS) % LANES == 0
    hps = 2 if nh % 2 == 0 else 1
    bw = hps * vd
    pairs = [(a, c) for a in range(nq) for c in range(a + 1)]
    qi_tab = jnp.asarray([a for a, _ in pairs], jnp.int32)
    ki_tab = jnp.asarray([c for _, c in pairs], jnp.int32)
    grid_spec = pltpu.PrefetchScalarGridSpec(
        num_scalar_prefetch=2,
        grid=(bz, nh // hps, len(pairs)),
        in_specs=[pl.BlockSpec((4, hd), lambda b, h, p, qt, kt: (0, 0)),
                  pl.BlockSpec((1, vd), lambda b, h, p, qt, kt: (0, 0)),
                  pl.BlockSpec((tq, bw), lambda b, h, p, qt, kt: (b * nq + qt[p], h)),
                  pl.BlockSpec((tq, bw), lambda b, h, p, qt, kt: (b * nq + kt[p], h)),
                  pl.BlockSpec((tq, bw), lambda b, h, p, qt, kt: (b * nq + kt[p], h))],
        out_specs=pl.BlockSpec((tq, bw), lambda b, h, p, qt, kt: (b * nq + qt[p], h)),
        scratch_shapes=[pltpu.VMEM((hps, 2 * tq, vd), BF16), pltpu.VMEM((hps, 2 * tq, LANES), F32),
                        pltpu.VMEM((hps, 2 * tq, LANES), F32), pltpu.VMEM((hps, 2 * tq, vd), F32)],
    )
    return pl.pallas_call(
        functools.partial(_flash_kernel, tq=tq, hd=hd, hps=hps, lam_init=lam_init),
        grid_spec=grid_spec,
        out_shape=jax.ShapeDtypeStruct((t, nh * vd), BF16),
        compiler_params=_params("parallel", "parallel", "arbitrary"),
        name="prompt_attention",
    )(qi_tab, ki_tab, lam_vecs, subln_g.reshape(1, vd), q, k, v)


def _paged_kernel(pt_ref, lam_ref, g_ref, qidx_ref, q_ref, e_ref, msk_ref, *refs,
                  n_steps, pp, rpb, lam_init):
    k_refs, v_refs = refs[:pp], refs[pp:2 * pp]
    kn_ref, vn_ref, o_ref, m_ref, l_ref, acc_ref = refs[2 * pp:]
    p = pl.program_id(1)

    @pl.when(p == 0)
    def _():
        m_ref[...] = jnp.full(m_ref.shape, NEG_INF, F32)
        l_ref[...] = jnp.zeros(l_ref.shape, F32)
        acc_ref[...] = jnp.zeros(acc_ref.shape, F32)

    def update(kts, vfs, masked):
        q = q_ref[...]
        ss = [jnp.dot(q, kt[...].astype(BF16), preferred_element_type=F32) for kt in kts]
        if masked:
            col = lax.broadcasted_iota(jnp.int32, ss[0].shape, 1)
            ss = [jnp.where(col <= qidx_ref[...], s, NEG_INF) for s in ss]
        m_old = m_ref[...]
        m_new = jnp.maximum(m_old, jnp.max(functools.reduce(jnp.maximum, ss), axis=-1, keepdims=True))
        alpha = jnp.exp(m_old - m_new)
        prs = [jnp.exp(s - m_new) for s in ss]
        l_ref[...] = alpha * l_ref[...] + jnp.sum(functools.reduce(jnp.add, prs), axis=-1, keepdims=True)
        pv = None
        for pr, vf in zip(prs, vfs):
            pe = jnp.dot(pr.astype(BF16), e_ref[...], preferred_element_type=F32)
            pe = pe.astype(BF16) * msk_ref[...]
            part = jnp.dot(pe, vf[...].astype(BF16), preferred_element_type=F32)
            pv = part if pv is None else pv + part
        acc_ref[...] = alpha * acc_ref[...] + pv
        m_ref[...] = m_new

    @pl.when(p < n_steps)
    def _():
        for c0 in range(0, pp, PAGE_GROUP):
            update(k_refs[c0:c0 + PAGE_GROUP], v_refs[c0:c0 + PAGE_GROUP], False)

    @pl.when(p == n_steps)
    def _():
        update([kn_ref], [vn_ref], True)
        o = acc_ref[...] / l_ref[...]
        lam = _diff_lambda(lam_ref, lam_init)
        r = o.shape[0]
        d = o - lam * pltpu.roll(o, r - rpb // 2, axis=0)
        o_ref[...] = _rms_rows(d, g_ref[...]) * (1.0 - lam_init)


def sample_attention(q, k_new, v_new, cache_k, cache_v, layer, page_table, lam_vecs, subln_g,
                     bz, nq, hd, lam_init):
    vd = 2 * hd
    qw = q.shape[1]
    nh = qw // vd
    n_odd, n_pool, page = cache_k.shape[:3]
    n_pages = page_table.shape[1]
    assert nq <= page
    pp = next(c for c in (8, 4, 2, 1) if n_pages % c == 0 and n_pages // c >= min(2, n_pages))
    n_steps = n_pages // pp
    rpb = 2 * nq
    r = nh * rpb
    ckt = cache_k.transpose(0, 1, 3, 4, 2).reshape(n_odd, n_pool, qw, page)
    cvf = cache_v.reshape(n_odd, n_pool, page * nh, vd)
    q4 = q.reshape(bz, nq, nh, 2, hd).transpose(0, 2, 3, 1, 4)
    eye = jnp.eye(2 * nh, dtype=q.dtype).reshape(nh, 2, 1, 2 * nh, 1)
    qbd = (q4.reshape(bz, nh, 2, nq, 1, hd) * eye[None]).reshape(bz, r, qw)
    knt = jnp.pad(k_new.reshape(bz, nq, qw).transpose(0, 2, 1), ((0, 0), (0, 0), (0, page - nq)))
    vnf = jnp.pad(v_new.reshape(bz, nq * nh, vd), ((0, 0), (0, (page - nq) * nh), (0, 0)))
    qidx = (jnp.arange(r, dtype=jnp.int32) % nq).reshape(r, 1)
    expand = jnp.repeat(jnp.eye(page, dtype=BF16), nh, axis=1)
    own = (jnp.arange(page * nh)[None, :] % nh == jnp.arange(r)[:, None] // rpb).astype(BF16)
    pt = page_table.reshape(-1).astype(jnp.int32)

    def page_map(c):
        def index(b, p, pt):
            return (layer, pt[b * n_pages + jnp.minimum(p, n_steps - 1) * pp + c], 0, 0)
        return index

    grid_spec = pltpu.PrefetchScalarGridSpec(
        num_scalar_prefetch=1,
        grid=(bz, n_steps + 1),
        in_specs=([pl.BlockSpec((4, hd), lambda b, p, pt: (0, 0)),
                   pl.BlockSpec((1, vd), lambda b, p, pt: (0, 0)),
                   pl.BlockSpec((r, 1), lambda b, p, pt: (0, 0)),
                   pl.BlockSpec((None, r, qw), lambda b, p, pt: (b, 0, 0)),
                   pl.BlockSpec((page, page * nh), lambda b, p, pt: (0, 0)),
                   pl.BlockSpec((r, page * nh), lambda b, p, pt: (0, 0))]
                  + [pl.BlockSpec((None, None, qw, page), page_map(c)) for c in range(pp)]
                  + [pl.BlockSpec((None, None, page * nh, vd), page_map(c)) for c in range(pp)]
                  + [pl.BlockSpec((None, qw, page), lambda b, p, pt: (b, 0, 0)),
                     pl.BlockSpec((None, page * nh, vd), lambda b, p, pt: (b, 0, 0))]),
        out_specs=pl.BlockSpec((None, r, vd), lambda b, p, pt: (b, 0, 0)),
        scratch_shapes=[pltpu.VMEM((r, 1), F32), pltpu.VMEM((r, 1), F32), pltpu.VMEM((r, vd), F32)],
    )
    o = pl.pallas_call(
        functools.partial(_paged_kernel, n_steps=n_steps, pp=pp, rpb=rpb, lam_init=lam_init),
        grid_spec=grid_spec,
        out_shape=jax.ShapeDtypeStruct((bz, r, vd), F32),
        compiler_params=_params("parallel", "arbitrary"),
        name="sample_attention",
    )(pt, lam_vecs, subln_g.reshape(1, vd), qidx, qbd, expand, own,
      *([ckt] * pp), *([cvf] * pp), knt, vnf)
    o = o.reshape(bz, nh, 2, nq, vd)[:, :, 0]
    return o.transpose(0, 2, 1, 3).reshape(bz * nq, nh * vd).astype(BF16)


def kernel(x_prompt, x_sample, cache_k, cache_v, state_ssm_re, state_ssm_im, state_conv, page_table, norm_mix_pre, norm_mix_post, norm_ffn_pre, norm_ffn_post, w_in_even, ssm_lam_re, ssm_lam_im, ssm_log_dt, ssm_b_re, ssm_b_im, ssm_c_re, ssm_c_im, ssm_d, w_glu, conv_w, conv_b, conv_ln_g, conv_ln_b, w_out_even, w_qkv, lambda_q1, lambda_k1, lambda_q2, lambda_k2, subln_g, w_o, w_gate, w_up, w_down):
    depth = norm_mix_pre.shape[0]
    d_model = x_prompt.shape[-1]
    a_width = w_glu.shape[1]
    n_groups, n_state = ssm_lam_re.shape[1:]
    bw = conv_w.shape[2]
    page = cache_k.shape[2]
    head_dim = cache_k.shape[-1]
    q_width = cache_k.shape[-2] * head_dim
    past_len = page_table.shape[1] * page

    w_in_b, w_glu_b, w_out_b = w_in_even.astype(BF16), w_glu.astype(BF16), w_out_even.astype(BF16)
    w_qkv_b, w_o_b = w_qkv.astype(BF16), w_o.astype(BF16)
    w_gate_b, w_up_b, w_down_b = w_gate.astype(BF16), w_up.astype(BF16), w_down.astype(BF16)
    n_odd = w_qkv.shape[0]
    lam_vecs = jnp.stack([lambda_q1, lambda_k1, lambda_q2, lambda_k2], axis=1)

    def run_trunk(x, pos0, ssm_re0, ssm_im0, conv0, paged):
        bz, t, _ = x.shape
        h = x.reshape(bz * t, d_model)
        pos = jnp.tile(pos0 + jnp.arange(t, dtype=F32), bz).reshape(bz * t, 1)
        new_re, new_im, new_conv = [], [], []
        k_all = v_all = None
        for i in range(depth):
            j = i // 2
            if i % 2 == 0:
                proj = norm_matmul(h, norm_mix_pre[i], w_in_b, j)
                y_a, s_re, s_im = s5_mixer(
                    proj, bz, t, a_width, ssm_re0[j], ssm_im0[j],
                    ssm_lam_re[j], ssm_lam_im[j], ssm_log_dt[j], ssm_b_re[j], ssm_b_im[j],
                    ssm_c_re[j], ssm_c_im[j], ssm_d[j])
                y_a = s5_glu(y_a, w_glu_b, j)
                y_b, buf = conv_module(proj, conv0[j], conv_w[j], conv_b[j], conv_ln_g[j],
                                       conv_ln_b[j], bz, t, a_width)
                new_re.append(s_re)
                new_im.append(s_im)
                new_conv.append(buf)
                mix_in, w_mix = (y_a, y_b), w_out_b
            else:
                lam_init = 0.8 - 0.6 * math.exp(-0.3 * i)
                q, k_all, v_all, kb, vb = qkv_rope(h, norm_mix_pre[i], w_qkv_b, pos, q_width, head_dim,
                                                   j, n_odd, t, k_all, v_all)
                if paged:
                    assert k_all.ndim == 3
                    mix_in = sample_attention(q, k_all[j], v_all[j], cache_k, cache_v, j, page_table,
                                              lam_vecs[j], subln_g[j], bz, t, head_dim, lam_init)
                else:
                    mix_in = prompt_attention(q, kb, vb, lam_vecs[j], subln_g[j], bz, t, head_dim,
                                              lam_init)
                w_mix = w_o_b
            parts = mix_in if isinstance(mix_in, tuple) else (mix_in,)
            h = matmul_norm_residual(parts, w_mix, j, norm_mix_post[i], h)
            h = ffn(h, norm_ffn_pre[i], norm_ffn_post[i], w_gate_b, w_up_b, w_down_b, i)
        n_kh = q_width // head_dim
        if k_all.ndim == 4:
            new_k = k_all.reshape(n_odd, bz, n_kh, head_dim, t).transpose(0, 1, 4, 2, 3)
        else:
            new_k = k_all.reshape(n_odd, bz, t, n_kh, head_dim)
        new_v = v_all.reshape(n_odd, bz, t, -1, 2 * head_dim)
        return (h.reshape(bz, t, d_model), jnp.stack(new_re), jnp.stack(new_im), jnp.stack(new_conv),
                new_k, new_v)

    n_prompt = x_prompt.shape[0]
    n_even = state_ssm_re.shape[0]
    zero_ssm = jnp.zeros((n_even, n_prompt, n_groups, n_state), F32)
    zero_conv = jnp.zeros((n_even, n_prompt, conv_w.shape[1] - 1, bw), F32)
    y_p, re_p, im_p, conv_p, k_p, v_p = run_trunk(x_prompt, 0.0, zero_ssm, zero_ssm, zero_conv, False)
    y_s, re_s, im_s, conv_s, k_s, v_s = run_trunk(x_sample, float(past_len), state_ssm_re, state_ssm_im,
                                                  state_conv, True)
    return (y_p, y_s, re_p, im_p, conv_p, k_p, v_p, re_s, im_s, conv_s, k_s, v_s)
```
